```python
import jax, jax.numpy as jnp
from jax import lax
import numpy as np

D_MODEL = 2048
BATCH = 1
SEQ = 16384
DEPTH = 1

EPS = 1e-6
N_MOD = 6
POOL_WINDOWS = (2, 4, 8, 16)
POOL_GROUPS = 4
POOL_WIDTH = D_MODEL
POOL_GDIM = POOL_WIDTH // POOL_GROUPS
RET_HEADS = 8
RET_DK = D_MODEL // RET_HEADS
RET_DV = 2 * RET_DK
RET_QK_WIDTH = RET_HEADS * RET_DK
RET_V_WIDTH = RET_HEADS * RET_DV
RET_CHUNK = 128
ROPE_BASE = 10000.0
SPLITS = (POOL_WIDTH, RET_QK_WIDTH, RET_QK_WIDTH, RET_V_WIDTH, RET_V_WIDTH, D_MODEL, D_MODEL)
IN_COLS = sum(SPLITS)
PEER_HEADS = 8
PEER_NKEYS = 128
PEER_N_EXPERTS = PEER_NKEYS * PEER_NKEYS
PEER_QDIM = 256
PEER_HALF = PEER_QDIM // 2
PEER_TOPK_HALF = 16
PEER_TOPK = 16
PEER_TOKEN_BLOCK = 128

kernel_name = 'hybrid_pool_retention_peer_adaln'


def rmsnorm(x, gain):
    xf = x.astype(jnp.float32)
    y = xf * lax.rsqrt(jnp.mean(xf * xf, axis=-1, keepdims=True) + EPS)
    return (y * gain.astype(jnp.float32)).astype(x.dtype)


def modulate(xn, shift, scale):
    return xn * (1 + scale[:, None, :]) + shift[:, None, :]


def head_layernorm(y, gain):
    B, S, H, dv = y.shape
    yf = y.astype(jnp.float32)
    mu = jnp.mean(yf, axis=-1, keepdims=True)
    var = jnp.mean(jnp.square(yf - mu), axis=-1, keepdims=True)
    yn = ((yf - mu) * lax.rsqrt(var + EPS)).reshape(B, S, H * dv)
    return (yn * gain.astype(jnp.float32)).astype(y.dtype)


def causal_window_mean(u, w):
    S = u.shape[1]
    cs = jnp.cumsum(u.astype(jnp.float32), axis=1)
    lag = jnp.pad(cs, ((0, 0), (w, 0), (0, 0)))[:, :S]
    count = jnp.minimum(jnp.arange(1, S + 1), w).astype(jnp.float32)
    return ((cs - lag) / count[None, :, None]).astype(u.dtype)


def pool_mixer(u, pool_w, pool_scale):
    B, S, _ = u.shape
    ug = u.reshape(B, S, POOL_GROUPS, POOL_GDIM)
    pooled = jnp.stack(
        [causal_window_mean(ug[:, :, g], w) - ug[:, :, g] for g, w in enumerate(POOL_WINDOWS)], axis=2)
    mixed = jnp.einsum('bsgc,gcd->bsgd', pooled, pool_w)
    return mixed.reshape(B, S, POOL_WIDTH) * pool_scale


def rotary(t, positions):
    dh = t.shape[-1]
    inv_freq = ROPE_BASE ** (-jnp.arange(0, dh, 2, dtype=jnp.float32) / dh)
    ang = positions.astype(jnp.float32)[..., None] * inv_freq
    cos = jnp.cos(ang)[:, :, None, :].astype(t.dtype)
    sin = jnp.sin(ang)[:, :, None, :].astype(t.dtype)
    t1, t2 = jnp.split(t, 2, axis=-1)
    return jnp.concatenate([t1 * cos - t2 * sin, t2 * cos + t1 * sin], axis=-1)


def retention(q, k, v):
    B, S, H, dk = q.shape
    dv = v.shape[-1]
    C = RET_CHUNK
    nC = S // C
    log_g = jnp.log(1.0 - 2.0 ** (-5.0 - jnp.arange(H, dtype=jnp.float32)))
    idx = jnp.arange(C, dtype=jnp.float32)
    diff = idx[:, None] - idx[None, :]
    decay_intra = jnp.where(diff[None] >= 0,
                            jnp.exp(jnp.maximum(diff, 0.0)[None] * log_g[:, None, None]), 0.0)
    q_dec = jnp.exp((idx + 1.0)[None, :] * log_g[:, None])
    k_dec = jnp.exp((C - 1.0 - idx)[None, :] * log_g[:, None])
    chunk_dec = jnp.exp(C * log_g)

    def to_chunks(t):
        return t.reshape(B, nC, C, H, t.shape[-1]).transpose(1, 0, 3, 2, 4)

    def step(state, inp):
        qi, ki, vi = inp
        scores = jnp.einsum('bhid,bhjd->bhij', qi, ki) * decay_intra
        inner = jnp.einsum('bhij,bhjv->bhiv', scores, vi)
        cross = jnp.einsum('bhid,bhdv->bhiv', qi * q_dec[None, :, :, None], state)
        new_state = state * chunk_dec[None, :, None, None] + jnp.einsum(
            'bhjd,bhjv->bhdv', ki * k_dec[None, :, :, None], vi)
        return new_state, inner + cross

    state0 = jnp.zeros((B, H, dk, dv), jnp.float32)
    _, out = lax.scan(step, state0, (to_chunks(q), to_chunks(k), to_chunks(v)))
    return out.transpose(1, 0, 3, 2, 4).reshape(B, S, H, dv).astype(v.dtype)


def peer_ffn(xn, w_query, sub_keys, expert_u, expert_v):
    B, S, D = xn.shape
    q = (xn @ w_query).reshape(B, S, PEER_HEADS, 2, PEER_HALF)
    s = jnp.einsum('bshpc,hpnc->bshpn', q, sub_keys)
    top_s, top_i = lax.top_k(s, PEER_TOPK_HALF)
    cand_s = (top_s[..., 0, :, None] + top_s[..., 1, None, :]).reshape(B, S, PEER_HEADS, -1)
    cand_e = (top_i[..., 0, :, None] * PEER_NKEYS + top_i[..., 1, None, :]).reshape(B, S, PEER_HEADS, -1)
    best_s, best_pos = lax.top_k(cand_s, PEER_TOPK)
    experts = jnp.take_along_axis(cand_e, best_pos, axis=-1)
    gates = jax.nn.softmax(best_s.astype(jnp.float32), axis=-1).astype(xn.dtype)
    T = B * S
    nb = T // PEER_TOKEN_BLOCK
    xt = xn.reshape(nb, PEER_TOKEN_BLOCK, D)
    et = experts.reshape(nb, PEER_TOKEN_BLOCK, PEER_HEADS * PEER_TOPK)
    gt = gates.reshape(nb, PEER_TOKEN_BLOCK, PEER_HEADS * PEER_TOPK)

    def block(args):
        xb, eb, gb = args
        u = expert_u[eb]
        v = expert_v[eb]
        act = jax.nn.gelu(jnp.einsum('td,tkd->tk', xb, u))
        return jnp.einsum('tk,tkd->td', gb * act, v)

    out = lax.map(block, (xt, et, gt))
    return out.reshape(B, S, D)


def setup_inputs(seed: int = 0) -> dict:
    key = jax.random.key(seed)
    ks = jax.random.split(key, 18)
    L = DEPTH
    D = D_MODEL

    def nrm(k, shape, scale):
        return jax.random.normal(k, shape, jnp.float32) * scale

    return {
        'x': nrm(ks[0], (BATCH, SEQ, D), 1.0),
        'c': nrm(ks[1], (BATCH, D), 1.0),
        'positions': jnp.broadcast_to(jnp.arange(SEQ, dtype=jnp.int32), (BATCH, SEQ)),
        'norm_mix_gain': 1.0 + nrm(ks[2], (L, D), 0.02),
        'w_ada': nrm(ks[3], (L, D, N_MOD * D), 0.5 * D ** -0.5),
        'b_ada': nrm(ks[4], (L, N_MOD * D), 0.02),
        'w_in': nrm(ks[5], (L, D, IN_COLS), D ** -0.5),
        'pool_w': nrm(ks[6], (L, POOL_GROUPS, POOL_GDIM, POOL_GDIM), POOL_GDIM ** -0.5),
        'pool_scale': 1.0 + nrm(ks[7], (L, POOL_WIDTH), 0.02),
        'ret_norm_gain': 1.0 + nrm(ks[8], (L, RET_V_WIDTH), 0.02),
        'w_branch_pool': nrm(ks[9], (L, POOL_WIDTH, D), POOL_WIDTH ** -0.5),
        'w_branch_ret': nrm(ks[10], (L, RET_V_WIDTH, D), RET_V_WIDTH ** -0.5),
        'w_out': nrm(ks[11], (L, D, D), D ** -0.5),
        'norm_ffn_gain': 1.0 + nrm(ks[12], (L, D), 0.02),
        'peer_w_query': nrm(ks[13], (L, D, PEER_HEADS * PEER_QDIM), D ** -0.5),
        'peer_sub_keys': nrm(ks[14], (L, PEER_HEADS, 2, PEER_NKEYS, PEER_HALF), PEER_HALF ** -0.5),
        'peer_u': nrm(ks[15], (L, PEER_N_EXPERTS, D), D ** -0.5),
        'peer_v': nrm(ks[16], (L, PEER_N_EXPERTS, D), PEER_HEADS ** -0.5),
        'final_norm_gain': 1.0 + nrm(ks[17], (D,), 0.02),
    }


def reference(x, c, positions, norm_mix_gain, w_ada, b_ada, w_in, pool_w, pool_scale, ret_norm_gain,
              w_branch_pool, w_branch_ret, w_out, norm_ffn_gain, peer_w_query, peer_sub_keys, peer_u, peer_v,
              final_norm_gain):
    B, S, _ = x.shape
    cond = jax.nn.silu(c)
    split_points = [int(p) for p in np.cumsum(SPLITS)[:-1]]
    for l in range(DEPTH):
        mod = cond @ w_ada[l] + b_ada[l]
        sh1, sc1, g1, sh2, sc2, g2 = jnp.split(mod, N_MOD, axis=-1)
        hn = modulate(rmsnorm(x, norm_mix_gain[l]), sh1, sc1)
        proj = hn @ w_in[l]
        u_pool, q, k, v, g_ret, a_pool, a_ret = jnp.split(proj, split_points, axis=-1)
        pool_out = pool_mixer(u_pool, pool_w[l], pool_scale[l])
        q = rotary(q.reshape(B, S, RET_HEADS, RET_DK), positions)
        k = rotary(k.reshape(B, S, RET_HEADS, RET_DK), positions) * (RET_DK ** -0.5)
        y = retention(q, k, v.reshape(B, S, RET_HEADS, RET_DV))
        ret_out = jax.nn.silu(g_ret) * head_layernorm(y, ret_norm_gain[l])
        merged = (jax.nn.sigmoid(a_pool) * (pool_out @ w_branch_pool[l])
                  + jax.nn.sigmoid(a_ret) * (ret_out @ w_branch_ret[l]))
        x = x + g1[:, None, :] * (merged @ w_out[l])
        fn = modulate(rmsnorm(x, norm_ffn_gain[l]), sh2, sc2)
        x = x + g2[:, None, :] * peer_ffn(fn, peer_w_query[l], peer_sub_keys[l], peer_u[l], peer_v[l])
    return rmsnorm(x, final_norm_gain)
```

```python
import functools

import jax
import jax.numpy as jnp
from jax import lax
from jax.experimental import pallas as pl
from jax.experimental.pallas import tpu as pltpu

F32 = jnp.float32
BF16 = jnp.bfloat16

D_MODEL = 2048
EPS = 1e-6
POOL_GROUPS = 4
POOL_GDIM = 512
RET_HEADS = 8
RET_DK = 256
RET_DV = 512
ROPE_BASE = 10000.0
PEER_HEADS = 8
PEER_NKEYS = 128
PEER_HALF = 128
PEER_TOPK = 16
N_EXPERTS = PEER_NKEYS * PEER_NKEYS

COL_POOL, COL_Q, COL_K, COL_V, COL_G, COL_APOOL, COL_ARET = 0, 2048, 4096, 6144, 10240, 14336, 16384
IN_COLS = 18432

LANES = 128
MIB = 1024 * 1024

RET_CHUNK = 256


def _params(n_axes, vmem_mib):
    return pltpu.CompilerParams(dimension_semantics=("arbitrary",) * n_axes,
                                vmem_limit_bytes=vmem_mib * MIB)


def _dot(a, b):
    return jnp.dot(a, b, preferred_element_type=F32)


def _sigmoid(x):
    return 1.0 / (1.0 + jnp.exp(-x))


def _ada_kernel(c_ref, w_ref, b_ref, o_ref):
    c = c_ref[...]
    cond = c * _sigmoid(c)
    o_ref[...] = jnp.dot(cond, w_ref[...], preferred_element_type=F32,
                         precision=lax.Precision.HIGHEST) + b_ref[...]


def _ada(c, w, b):
    d, n = w.shape
    tn = 1024
    c8 = jnp.broadcast_to(c, (8, d))
    out = pl.pallas_call(
        _ada_kernel,
        grid=(n // tn,),
        in_specs=[pl.BlockSpec((8, d), lambda j: (0, 0)),
                  pl.BlockSpec((d, tn), lambda j: (0, j)),
                  pl.BlockSpec((1, tn), lambda j: (0, j))],
        out_specs=pl.BlockSpec((8, tn), lambda j: (0, j)),
        out_shape=jax.ShapeDtypeStruct((8, n), F32),
        compiler_params=_params(1, 40),
        name="ada",
    )(c8, w, b.reshape(1, n))
    return out[0].reshape(6, d)


def _in_kernel(x_ref, gain_ref, mod_ref, w_ref, o_ref, hn_ref):
    tm = x_ref.shape[0]
    rc = 256

    @pl.when(pl.program_id(1) == 0)
    def _():
        scale = gain_ref[...] * (1.0 + mod_ref[1:2, :])
        shift = mod_ref[0:1, :]

        def body(r, carry):
            rows = pl.ds(pl.multiple_of(r * rc, rc), rc)
            x = x_ref[rows, :]
            ms = jnp.mean(x * x, axis=-1, keepdims=True)
            hn_ref[rows, :] = (x * lax.rsqrt(ms + EPS) * scale + shift).astype(BF16)
            return carry

        lax.fori_loop(0, tm // rc, body, 0)

    o_ref[...] = _dot(hn_ref[...], w_ref[...]).astype(BF16)


def _in_proj(x2d, gain, mod, w_bf16):
    s, d = x2d.shape
    n = w_bf16.shape[1]
    tm, tn = 1024, 1024
    return pl.pallas_call(
        _in_kernel,
        grid=(s // tm, n // tn),
        in_specs=[pl.BlockSpec((tm, d), lambda i, j: (i, 0)),
                  pl.BlockSpec((1, d), lambda i, j: (0, 0)),
                  pl.BlockSpec((6, d), lambda i, j: (0, 0)),
                  pl.BlockSpec((d, tn), lambda i, j: (0, j))],
        out_specs=pl.BlockSpec((tm, tn), lambda i, j: (i, j)),
        out_shape=jax.ShapeDtypeStruct((s, n), BF16),
        scratch_shapes=[pltpu.VMEM((tm, d), BF16)],
        compiler_params=_params(2, 48),
        name="in_proj",
    )(x2d, gain.reshape(1, d), mod, w_bf16)


def _rope_kernel(pos_ref, inv_ref, cos_ref, sin_ref):
    ang = pos_ref[...].astype(F32) * inv_ref[...]
    cos_ref[...] = jnp.cos(ang)
    sin_ref[...] = jnp.sin(ang)


def _rope_tables(positions):
    s = positions.shape[0]
    half = RET_DK // 2
    inv_freq = (ROPE_BASE ** (-jnp.arange(0, RET_DK, 2, dtype=F32) / RET_DK)).reshape(1, half)
    tm = 1024
    return pl.pallas_call(
        _rope_kernel,
        grid=(s // tm,),
        in_specs=[pl.BlockSpec((tm, 1), lambda i: (i, 0)),
                  pl.BlockSpec((1, half), lambda i: (0, 0))],
        out_specs=[pl.BlockSpec((tm, half), lambda i: (i, 0)),
                   pl.BlockSpec((tm, half), lambda i: (i, 0))],
        out_shape=[jax.ShapeDtypeStruct((s, half), F32), jax.ShapeDtypeStruct((s, half), F32)],
        compiler_params=_params(1, 32),
        name="rope",
    )(positions.reshape(s, 1), inv_freq)


POOL_HALO = 16


def _pool_kernel(u_ref, halo_ref, pw_ref, ps_ref, o_ref):
    i = pl.program_id(0)
    g = pl.program_id(1)
    tm = u_ref.shape[0]
    w = jnp.left_shift(2, g)
    r = lax.broadcasted_iota(jnp.int32, (tm, tm), 0)
    c = lax.broadcasted_iota(jnp.int32, (tm, tm), 1)
    band = jnp.where(c <= r, jnp.where(c > r - w, 1.0, 0.0), 0.0).astype(BF16)
    rh = lax.broadcasted_iota(jnp.int32, (tm, POOL_HALO), 0)
    ch = lax.broadcasted_iota(jnp.int32, (tm, POOL_HALO), 1)
    halo_on = jnp.where(i > 0, 1.0, 0.0)
    bandh = (jnp.where(ch > rh + POOL_HALO - w, 1.0, 0.0) * halo_on).astype(BF16)
    u = u_ref[...]
    wsum = _dot(band, u) + _dot(bandh, halo_ref[...])
    t = i * tm + lax.broadcasted_iota(jnp.int32, (tm, 1), 0)
    cnt = jnp.minimum(t + 1, w).astype(F32)
    pooled = wsum / cnt - u.astype(F32)
    o_ref[...] = (_dot(pooled.astype(BF16), pw_ref[...]) * ps_ref[...]).astype(BF16)


def _pool(proj, pool_w_bf16, pool_scale):
    s = proj.shape[0]
    tm = 512
    hb = tm // POOL_HALO
    return pl.pallas_call(
        _pool_kernel,
        grid=(s // tm, POOL_GROUPS),
        in_specs=[pl.BlockSpec((tm, POOL_GDIM), lambda i, g: (i, g)),
                  pl.BlockSpec((POOL_HALO, POOL_GDIM), lambda i, g: (jnp.maximum(i * hb - 1, 0), g)),
                  pl.BlockSpec((None, POOL_GDIM, POOL_GDIM), lambda i, g: (g, 0, 0)),
                  pl.BlockSpec((1, POOL_GDIM), lambda i, g: (0, g))],
        out_specs=pl.BlockSpec((tm, POOL_GDIM), lambda i, g: (i, g)),
        out_shape=jax.ShapeDtypeStruct((s, POOL_GROUPS * POOL_GDIM), BF16),
        compiler_params=_params(2, 32),
        name="pool",
    )(proj, proj, pool_w_bf16, pool_scale.reshape(1, -1))


def _ret_kernel(cdec_ref, q_ref, k_ref, v_ref, g_ref, cos_ref, sin_ref, dec_ref, qd_ref, kd_ref,
                gain_ref, o_ref, state_ref):
    h = pl.program_id(0)

    @pl.when(pl.program_id(1) == 0)
    def _():
        state_ref[...] = jnp.zeros_like(state_ref)

    cos = cos_ref[...]
    sin = sin_ref[...]
    half = RET_DK // 2

    def rot(t_ref):
        t = t_ref[...].astype(F32)
        t1 = t[:, :half]
        t2 = t[:, half:]
        return t1 * cos - t2 * sin, t2 * cos + t1 * sin

    q1, q2 = rot(q_ref)
    k1, k2 = rot(k_ref)
    qd = qd_ref[...]
    kd = kd_ref[...]
    qb = jnp.concatenate([q1, q2], axis=1).astype(BF16)
    kb = jnp.concatenate([k1, k2], axis=1).astype(BF16)
    qdb = jnp.concatenate([q1 * qd, q2 * qd], axis=1).astype(BF16)
    kdb = jnp.concatenate([k1 * kd, k2 * kd], axis=1).astype(BF16)
    v = v_ref[...]
    scores = lax.dot_general(qb, kb, (((1,), (1,)), ((), ())), preferred_element_type=F32) * dec_ref[...]
    inner = _dot(scores.astype(BF16), v)
    state = state_ref[...]
    cross = _dot(qdb, state.astype(BF16))
    state_ref[...] = state * cdec_ref[h] + lax.dot_general(
        kdb, v, (((0,), (0,)), ((), ())), preferred_element_type=F32)
    y = inner + cross
    mu = jnp.mean(y, axis=-1, keepdims=True)
    yc = y - mu
    var = jnp.mean(yc * yc, axis=-1, keepdims=True)
    yn = yc * lax.rsqrt(var + EPS) * gain_ref[...]
    gate = g_ref[...].astype(F32)
    o_ref[...] = (gate * _sigmoid(gate) * yn).astype(BF16)


def _ret_tables():
    c = RET_CHUNK
    log_g = jnp.log(1.0 - 2.0 ** (-5.0 - jnp.arange(RET_HEADS, dtype=F32)))
    idx = jnp.arange(c, dtype=F32)
    diff = idx[:, None] - idx[None, :]
    kscale = RET_DK ** -0.5
    dec = jnp.where(diff[None] >= 0, jnp.exp(jnp.maximum(diff, 0.0)[None] * log_g[:, None, None]), 0.0) * kscale
    qd = jnp.exp((idx + 1.0)[None, :] * log_g[:, None])
    kd = jnp.exp((c - 1.0 - idx)[None, :] * log_g[:, None]) * kscale
    half = RET_DK // 2
    qd = jnp.broadcast_to(qd[:, :, None], (RET_HEADS, c, half))
    kd = jnp.broadcast_to(kd[:, :, None], (RET_HEADS, c, half))
    cdec = jnp.exp(c * log_g)
    return cdec, dec, qd, kd


def _retention(proj, cos, sin, ret_gain):
    s = proj.shape[0]
    c = RET_CHUNK
    half = RET_DK // 2
    cdec, dec, qd, kd = _ret_tables()
    qb0, kb0 = COL_Q // RET_DK, COL_K // RET_DK
    vb0, gb0 = COL_V // RET_DV, COL_G // RET_DV
    return pl.pallas_call(
        _ret_kernel,
        grid=(RET_HEADS, s // c),
        in_specs=[pl.BlockSpec(memory_space=pltpu.SMEM),
                  pl.BlockSpec((c, RET_DK), lambda h, i: (i, qb0 + h)),
                  pl.BlockSpec((c, RET_DK), lambda h, i: (i, kb0 + h)),
                  pl.BlockSpec((c, RET_DV), lambda h, i: (i, vb0 + h)),
                  pl.BlockSpec((c, RET_DV), lambda h, i: (i, gb0 + h)),
                  pl.BlockSpec((c, half), lambda h, i: (i, 0)),
                  pl.BlockSpec((c, half), lambda h, i: (i, 0)),
                  pl.BlockSpec((None, c, c), lambda h, i: (h, 0, 0)),
                  pl.BlockSpec((None, c, half), lambda h, i: (h, 0, 0)),
                  pl.BlockSpec((None, c, half), lambda h, i: (h, 0, 0)),
                  pl.BlockSpec((1, RET_DV), lambda h, i: (0, h))],
        out_specs=pl.BlockSpec((c, RET_DV), lambda h, i: (i, h)),
        out_shape=jax.ShapeDtypeStruct((s, RET_HEADS * RET_DV), BF16),
        scratch_shapes=[pltpu.VMEM((RET_DK, RET_DV), F32)],
        compiler_params=_params(2, 32),
        name="retention",
    )(cdec, proj, proj, proj, proj, cos, sin, dec, qd, kd, ret_gain.reshape(1, -1))


def _branch_kernel(p_ref, r_ref, ap_ref, ar_ref, wp_ref, wr_ref, o_ref):
    bp = _dot(p_ref[...], wp_ref[...])
    br = _dot(r_ref[...], wr_ref[...])
    ap = ap_ref[...].astype(F32)
    ar = ar_ref[...].astype(F32)
    o_ref[...] = (_sigmoid(ap) * bp + _sigmoid(ar) * br).astype(BF16)


def _branch(pool_out, ret_out, proj, wp_bf16, wr_bf16):
    s = pool_out.shape[0]
    d = wp_bf16.shape[1]
    tm, tn = 512, 512
    ap0, ar0 = COL_APOOL // tn, COL_ARET // tn
    return pl.pallas_call(
        _branch_kernel,
        grid=(s // tm, d // tn),
        in_specs=[pl.BlockSpec((tm, pool_out.shape[1]), lambda i, j: (i, 0)),
                  pl.BlockSpec((tm, ret_out.shape[1]), lambda i, j: (i, 0)),
                  pl.BlockSpec((tm, tn), lambda i, j: (i, ap0 + j)),
                  pl.BlockSpec((tm, tn), lambda i, j: (i, ar0 + j)),
                  pl.BlockSpec((wp_bf16.shape[0], tn), lambda i, j: (0, j)),
                  pl.BlockSpec((wr_bf16.shape[0], tn), lambda i, j: (0, j))],
        out_specs=pl.BlockSpec((tm, tn), lambda i, j: (i, j)),
        out_shape=jax.ShapeDtypeStruct((s, d), BF16),
        compiler_params=_params(2, 40),
        name="branch",
    )(pool_out, ret_out, proj, proj, wp_bf16, wr_bf16)


def _out_kernel(m_ref, w_ref, x_ref, mod_ref, gain_ref, x1_ref, fnt_ref):
    x1 = x_ref[...] + mod_ref[2:3, :] * _dot(m_ref[...], w_ref[...])
    x1_ref[...] = x1
    ms = jnp.mean(x1 * x1, axis=-1, keepdims=True)
    fn = x1 * lax.rsqrt(ms + EPS) * (gain_ref[...] * (1.0 + mod_ref[4:5, :])) + mod_ref[3:4, :]
    fnt_ref[...] = fn.T.astype(BF16)


def _out_proj(merged, w_bf16, x2d, mod, gain):
    s, d = x2d.shape
    tm = 256
    return pl.pallas_call(
        _out_kernel,
        grid=(s // tm,),
        in_specs=[pl.BlockSpec((tm, d), lambda i: (i, 0)),
                  pl.BlockSpec((d, d), lambda i: (0, 0)),
                  pl.BlockSpec((tm, d), lambda i: (i, 0)),
                  pl.BlockSpec((6, d), lambda i: (0, 0)),
                  pl.BlockSpec((1, d), lambda i: (0, 0))],
        out_specs=[pl.BlockSpec((tm, d), lambda i: (i, 0)),
                   pl.BlockSpec((d, tm), lambda i: (0, i))],
        out_shape=[jax.ShapeDtypeStruct((s, d), F32), jax.ShapeDtypeStruct((d, s), BF16)],
        compiler_params=_params(1, 48),
        name="out_proj",
    )(merged, w_bf16, x2d, mod, gain.reshape(1, d))


def _pscore_kernel(fnt_ref, wqt_ref, keys_ref, s_ref):
    qt = _dot(wqt_ref[...], fnt_ref[...]).astype(BF16)
    for hp in range(2 * PEER_HEADS):
        s_ref[hp] = _dot(keys_ref[hp], qt[hp * PEER_HALF:(hp + 1) * PEER_HALF, :])


def _peer_scores(fnt, wqt_bf16, keys_bf16):
    d, s = fnt.shape
    t = 512
    nq = wqt_bf16.shape[0]
    return pl.pallas_call(
        _pscore_kernel,
        grid=(s // t,),
        in_specs=[pl.BlockSpec((d, t), lambda i: (0, i)),
                  pl.BlockSpec((nq, d), lambda i: (0, 0)),
                  pl.BlockSpec((2 * PEER_HEADS, PEER_NKEYS, PEER_HALF), lambda i: (0, 0, 0))],
        out_specs=pl.BlockSpec((2 * PEER_HEADS, PEER_NKEYS, t), lambda i: (0, 0, i)),
        out_shape=jax.ShapeDtypeStruct((2 * PEER_HEADS, PEER_NKEYS, s), F32),
        compiler_params=_params(1, 48),
        name="peer_scores",
    )(fnt, wqt_bf16, keys_bf16)


def _top_values(w, k):
    vals = []
    for _ in range(k):
        m = jnp.max(w, axis=0, keepdims=True)
        vals.append(m)
        w = jnp.where(w == m, -jnp.inf, w)
    return vals


PEER_ETILE = 512
PEER_IB = PEER_ETILE // PEER_NKEYS


def _ptopk_kernel(s_ref, s1r_ref, ar_ref, b_ref, tau_ref):
    t = s_ref.shape[2]

    def body(lc, carry):
        sl = pl.ds(pl.multiple_of(lc * LANES, LANES), LANES)
        s1 = s_ref[0, :, sl]
        s2 = s_ref[1, :, sl]
        v1 = _top_values(s1, PEER_TOPK)
        v2 = _top_values(s2, PEER_TOPK)
        v2all = jnp.concatenate(v2, axis=0)
        cand = jnp.concatenate([v1[r] + v2all for r in range(PEER_TOPK)], axis=0)
        best = _top_values(cand, PEER_TOPK)
        z = jnp.ones_like(best[0])
        for r in range(1, PEER_TOPK):
            z = z + jnp.exp(best[r] - best[0])
        a = jnp.exp(s1 - v1[0]) / z
        for g in range(PEER_NKEYS // PEER_IB):
            s1r_ref[g, :, sl] = s1[g * PEER_IB:(g + 1) * PEER_IB, :]
            ar_ref[g, :, sl] = a[g * PEER_IB:(g + 1) * PEER_IB, :]
        b_ref[:, sl] = jnp.exp(s2 - v2[0])
        tau_ref[:, sl] = best[PEER_TOPK - 1]
        return carry

    lax.fori_loop(0, t // LANES, body, 0)


def _peer_topk(scores):
    hp, nk, s = scores.shape
    t = 512
    ng = nk // PEER_IB
    rows = jax.ShapeDtypeStruct((PEER_HEADS, ng, PEER_IB, s), F32)
    row_spec = pl.BlockSpec((None, ng, PEER_IB, t), lambda i, h: (h, 0, 0, i))
    return pl.pallas_call(
        _ptopk_kernel,
        grid=(s // t, PEER_HEADS),
        in_specs=[pl.BlockSpec((2, nk, t), lambda i, h: (h, 0, i))],
        out_specs=[row_spec, row_spec,
                   pl.BlockSpec((None, nk, t), lambda i, h: (h, 0, i)),
                   pl.BlockSpec((None, 1, t), lambda i, h: (h, 0, i))],
        out_shape=[rows, rows,
                   jax.ShapeDtypeStruct((PEER_HEADS, nk, s), F32),
                   jax.ShapeDtypeStruct((PEER_HEADS, 1, s), F32)],
        compiler_params=_params(2, 32),
        name="peer_topk",
    )(scores)


def _gelu_tanh(x):
    return 0.5 * x * (1.0 + jnp.tanh(0.7978845608028654 * (x + 0.044715 * (x * x * x))))


def _pdense_kernel(fnt_ref, u_ref, vt_ref, s1r_ref, ar_ref, s2_ref, b_ref, tau_ref, o_ref, g_ref):
    e = pl.program_id(1)
    t = fnt_ref.shape[1]
    z = _dot(u_ref[...], fnt_ref[...])
    for il in range(PEER_IB):
        for lc in range(t // LANES):
            sl = slice(lc * LANES, (lc + 1) * LANES)
            acc = jnp.zeros((PEER_NKEYS, LANES), F32)
            for h in range(PEER_HEADS):
                s1row = s1r_ref[h, il:il + 1, sl]
                arow = ar_ref[h, il:il + 1, sl]
                taurow = tau_ref[h, :, sl]
                s2 = s2_ref[h, :, sl]
                b = b_ref[h, :, sl]
                acc = acc + jnp.where(s1row + s2 >= taurow, arow * b, 0.0)
            g_ref[il * PEER_NKEYS:(il + 1) * PEER_NKEYS, sl] = acc
    a = (g_ref[...] * _gelu_tanh(z)).astype(BF16)
    contrib = _dot(vt_ref[...], a)

    @pl.when(e == 0)
    def _():
        o_ref[...] = contrib

    @pl.when(e > 0)
    def _():
        o_ref[...] += contrib


def _peer_dense(fnt, u_bf16, vt_bf16, scores, s1r, ar, b, tau):
    d, s = fnt.shape
    ne = u_bf16.shape[0]
    t = 512
    et = PEER_ETILE
    nk = scores.shape[1]
    scores4 = scores.reshape(PEER_HEADS, 2, nk, s)
    row_spec = pl.BlockSpec((PEER_HEADS, None, PEER_IB, t), lambda i, e: (0, e, 0, i))
    return pl.pallas_call(
        _pdense_kernel,
        grid=(s // t, ne // et),
        in_specs=[pl.BlockSpec((d, t), lambda i, e: (0, i)),
                  pl.BlockSpec((et, d), lambda i, e: (e, 0)),
                  pl.BlockSpec((d, et), lambda i, e: (0, e)),
                  row_spec, row_spec,
                  pl.BlockSpec((PEER_HEADS, None, nk, t), lambda i, e: (0, 1, 0, i)),
                  pl.BlockSpec((PEER_HEADS, nk, t), lambda i, e: (0, 0, i)),
                  pl.BlockSpec((PEER_HEADS, 1, t), lambda i, e: (0, 0, i))],
        out_specs=pl.BlockSpec((d, t), lambda i, e: (0, i)),
        out_shape=jax.ShapeDtypeStruct((d, s), F32),
        scratch_shapes=[pltpu.VMEM((et, t), F32)],
        compiler_params=_params(2, 48),
        name="peer_dense",
    )(fnt, u_bf16, vt_bf16, s1r, ar, scores4, b, tau)


def _final_kernel(x1_ref, ot_ref, mod_ref, gain_ref, y_ref, *, apply_norm):
    x2 = x1_ref[...] + mod_ref[5:6, :] * ot_ref[...].T
    if apply_norm:
        ms = jnp.mean(x2 * x2, axis=-1, keepdims=True)
        x2 = x2 * lax.rsqrt(ms + EPS) * gain_ref[...]
    y_ref[...] = x2


def _final(x1, out_t, mod, gain, apply_norm):
    s, d = x1.shape
    tm = 512
    return pl.pallas_call(
        functools.partial(_final_kernel, apply_norm=apply_norm),
        grid=(s // tm,),
        in_specs=[pl.BlockSpec((tm, d), lambda i: (i, 0)),
                  pl.BlockSpec((d, tm), lambda i: (0, i)),
                  pl.BlockSpec((6, d), lambda i: (0, 0)),
                  pl.BlockSpec((1, d), lambda i: (0, 0))],
        out_specs=pl.BlockSpec((tm, d), lambda i: (i, 0)),
        out_shape=jax.ShapeDtypeStruct((s, d), F32),
        compiler_params=_params(1, 48),
        name="final",
    )(x1, out_t, mod, gain.reshape(1, d))


def kernel(x, c, positions, norm_mix_gain, w_ada, b_ada, w_in, pool_w, pool_scale, ret_norm_gain,
           w_branch_pool, w_branch_ret, w_out, norm_ffn_gain, peer_w_query, peer_sub_keys, peer_u, peer_v,
           final_norm_gain):
    batch, s, d = x.shape
    assert batch == 1 and d == D_MODEL and s % 1024 == 0
    depth = w_in.shape[0]
    xs = x.reshape(s, d)
    cos, sin = _rope_tables(positions.reshape(s))
    for l in range(depth):
        mod = _ada(c, w_ada[l], b_ada[l])
        proj = _in_proj(xs, norm_mix_gain[l], mod, w_in[l].astype(BF16))
        pool_out = _pool(proj, pool_w[l].astype(BF16), pool_scale[l])
        ret_out = _retention(proj, cos, sin, ret_norm_gain[l])
        merged = _branch(pool_out, ret_out, proj, w_branch_pool[l].astype(BF16), w_branch_ret[l].astype(BF16))
        x1, fnt = _out_proj(merged, w_out[l].astype(BF16), xs, mod, norm_ffn_gain[l])
        keys = peer_sub_keys[l].reshape(2 * PEER_HEADS, PEER_NKEYS, PEER_HALF).astype(BF16)
        scores = _peer_scores(fnt, peer_w_query[l].T.astype(BF16), keys)
        s1r, ar, b, tau = _peer_topk(scores)
        out_t = _peer_dense(fnt, peer_u[l].astype(BF16), peer_v[l].T.astype(BF16), scores, s1r, ar, b, tau)
        xs = _final(x1, out_t, mod, final_norm_gain, apply_norm=(l == depth - 1))
    return xs.reshape(batch, s, d)
```

```python
import functools

import jax
import jax.numpy as jnp
from jax import lax
from jax.experimental import pallas as pl
from jax.experimental.pallas import tpu as pltpu

F32 = jnp.float32
BF16 = jnp.bfloat16

D_MODEL = 2048
EPS = 1e-6
POOL_GROUPS = 4
POOL_GDIM = 512
RET_HEADS = 8
RET_DK = 256
RET_DV = 512
ROPE_BASE = 10000.0
PEER_HEADS = 8
PEER_NKEYS = 128
PEER_HALF = 128
PEER_TOPK = 16
N_EXPERTS = PEER_NKEYS * PEER_NKEYS

COL_POOL, COL_Q, COL_K, COL_V, COL_G, COL_APOOL, COL_ARET = 0, 2048, 4096, 6144, 10240, 14336, 16384
IN_COLS = 18432

LANES = 128
MIB = 1024 * 1024

RET_CHUNK = 256


def _params(n_axes, vmem_mib, flags=None):
    return pltpu.CompilerParams(dimension_semantics=("arbitrary",) * n_axes,
                                vmem_limit_bytes=vmem_mib * MIB, flags=flags)


def _dot(a, b):
    return jnp.dot(a, b, preferred_element_type=F32)


def _sigmoid(x):
    return 1.0 / (1.0 + jnp.exp(-x))


def _ada_kernel(c_ref, w_ref, b_ref, o_ref):
    c = c_ref[...]
    cond = c * _sigmoid(c)
    o_ref[...] = jnp.dot(cond, w_ref[...], preferred_element_type=F32,
                         precision=lax.Precision.HIGHEST) + b_ref[...]


def _ada(c, w, b):
    d, n = w.shape
    tn = 1024
    c8 = jnp.broadcast_to(c, (8, d))
    out = pl.pallas_call(
        _ada_kernel,
        grid=(n // tn,),
        in_specs=[pl.BlockSpec((8, d), lambda j: (0, 0)),
                  pl.BlockSpec((d, tn), lambda j: (0, j)),
                  pl.BlockSpec((1, tn), lambda j: (0, j))],
        out_specs=pl.BlockSpec((8, tn), lambda j: (0, j)),
        out_shape=jax.ShapeDtypeStruct((8, n), F32),
        compiler_params=_params(1, 40),
        name="ada",
    )(c8, w, b.reshape(1, n))
    return out[0].reshape(6, d)


def _in_kernel(x_ref, gain_ref, mod_ref, w_ref, o_ref, hn_ref):
    tm = x_ref.shape[0]
    rc = 256

    @pl.when(pl.program_id(1) == 0)
    def _():
        scale = gain_ref[...] * (1.0 + mod_ref[1:2, :])
        shift = mod_ref[0:1, :]

        def body(r, carry):
            rows = pl.ds(pl.multiple_of(r * rc, rc), rc)
            x = x_ref[rows, :]
            ms = jnp.mean(x * x, axis=-1, keepdims=True)
            hn_ref[rows, :] = (x * lax.rsqrt(ms + EPS) * scale + shift).astype(BF16)
            return carry

        lax.fori_loop(0, tm // rc, body, 0)

    o_ref[...] = _dot(hn_ref[...], w_ref[...]).astype(BF16)


def _in_proj(x2d, gain, mod, w_bf16):
    s, d = x2d.shape
    n = w_bf16.shape[1]
    tm, tn = 1024, 1024
    return pl.pallas_call(
        _in_kernel,
        grid=(s // tm, n // tn),
        in_specs=[pl.BlockSpec((tm, d), lambda i, j: (i, 0)),
                  pl.BlockSpec((1, d), lambda i, j: (0, 0)),
                  pl.BlockSpec((6, d), lambda i, j: (0, 0)),
                  pl.BlockSpec((d, tn), lambda i, j: (0, j))],
        out_specs=pl.BlockSpec((tm, tn), lambda i, j: (i, j)),
        out_shape=jax.ShapeDtypeStruct((s, n), BF16),
        scratch_shapes=[pltpu.VMEM((tm, d), BF16)],
        compiler_params=_params(2, 48),
        name="in_proj",
    )(x2d, gain.reshape(1, d), mod, w_bf16)


def _rope_kernel(pos_ref, inv_ref, cos_ref, sin_ref):
    ang = pos_ref[...].astype(F32) * inv_ref[...]
    cos_ref[...] = jnp.cos(ang)
    sin_ref[...] = jnp.sin(ang)


def _rope_tables(positions):
    s = positions.shape[0]
    half = RET_DK // 2
    inv_freq = (ROPE_BASE ** (-jnp.arange(0, RET_DK, 2, dtype=F32) / RET_DK)).reshape(1, half)
    tm = 1024
    return pl.pallas_call(
        _rope_kernel,
        grid=(s // tm,),
        in_specs=[pl.BlockSpec((tm, 1), lambda i: (i, 0)),
                  pl.BlockSpec((1, half), lambda i: (0, 0))],
        out_specs=[pl.BlockSpec((tm, half), lambda i: (i, 0)),
                   pl.BlockSpec((tm, half), lambda i: (i, 0))],
        out_shape=[jax.ShapeDtypeStruct((s, half), F32), jax.ShapeDtypeStruct((s, half), F32)],
        compiler_params=_params(1, 32),
        name="rope",
    )(positions.reshape(s, 1), inv_freq)


POOL_HALO = 16


def _pool_kernel(u_ref, halo_ref, pw_ref, ps_ref, o_ref):
    i = pl.program_id(0)
    g = pl.program_id(1)
    tm = u_ref.shape[0]
    w = jnp.left_shift(2, g)
    r = lax.broadcasted_iota(jnp.int32, (tm, tm), 0)
    c = lax.broadcasted_iota(jnp.int32, (tm, tm), 1)
    band = jnp.where(c <= r, jnp.where(c > r - w, 1.0, 0.0), 0.0).astype(BF16)
    rh = lax.broadcasted_iota(jnp.int32, (tm, POOL_HALO), 0)
    ch = lax.broadcasted_iota(jnp.int32, (tm, POOL_HALO), 1)
    halo_on = jnp.where(i > 0, 1.0, 0.0)
    bandh = (jnp.where(ch > rh + POOL_HALO - w, 1.0, 0.0) * halo_on).astype(BF16)
    u = u_ref[...]
    wsum = _dot(band, u) + _dot(bandh, halo_ref[...])
    t = i * tm + lax.broadcasted_iota(jnp.int32, (tm, 1), 0)
    cnt = jnp.minimum(t + 1, w).astype(F32)
    pooled = wsum / cnt - u.astype(F32)
    o_ref[...] = (_dot(pooled.astype(BF16), pw_ref[...]) * ps_ref[...]).astype(BF16)


def _pool(proj, pool_w_bf16, pool_scale):
    s = proj.shape[0]
    tm = 512
    hb = tm // POOL_HALO
    return pl.pallas_call(
        _pool_kernel,
        grid=(s // tm, POOL_GROUPS),
        in_specs=[pl.BlockSpec((tm, POOL_GDIM), lambda i, g: (i, g)),
                  pl.BlockSpec((POOL_HALO, POOL_GDIM), lambda i, g: (jnp.maximum(i * hb - 1, 0), g)),
                  pl.BlockSpec((None, POOL_GDIM, POOL_GDIM), lambda i, g: (g, 0, 0)),
                  pl.BlockSpec((1, POOL_GDIM), lambda i, g: (0, g))],
        out_specs=pl.BlockSpec((tm, POOL_GDIM), lambda i, g: (i, g)),
        out_shape=jax.ShapeDtypeStruct((s, POOL_GROUPS * POOL_GDIM), BF16),
        compiler_params=_params(2, 32),
        name="pool",
    )(proj, proj, pool_w_bf16, pool_scale.reshape(1, -1))


def _ret_kernel(cdec_ref, q_ref, k_ref, v_ref, g_ref, cos_ref, sin_ref, dec_ref, qd_ref, kd_ref,
                gain_ref, o_ref, state_ref):
    h = pl.program_id(0)

    @pl.when(pl.program_id(1) == 0)
    def _():
        state_ref[...] = jnp.zeros_like(state_ref)

    cos = cos_ref[...]
    sin = sin_ref[...]
    half = RET_DK // 2

    def rot(t_ref):
        t = t_ref[...].astype(F32)
        t1 = t[:, :half]
        t2 = t[:, half:]
        return t1 * cos - t2 * sin, t2 * cos + t1 * sin

    q1, q2 = rot(q_ref)
    k1, k2 = rot(k_ref)
    qd = qd_ref[...]
    kd = kd_ref[...]
    qb = jnp.concatenate([q1, q2], axis=1).astype(BF16)
    kb = jnp.concatenate([k1, k2], axis=1).astype(BF16)
    qdb = jnp.concatenate([q1 * qd, q2 * qd], axis=1).astype(BF16)
    kdb = jnp.concatenate([k1 * kd, k2 * kd], axis=1).astype(BF16)
    v = v_ref[...]
    scores = lax.dot_general(qb, kb, (((1,), (1,)), ((), ())), preferred_element_type=F32) * dec_ref[...]
    inner = _dot(scores.astype(BF16), v)
    state = state_ref[...]
    cross = _dot(qdb, state.astype(BF16))
    state_ref[...] = state * cdec_ref[h] + lax.dot_general(
        kdb, v, (((0,), (0,)), ((), ())), preferred_element_type=F32)
    y = inner + cross
    mu = jnp.mean(y, axis=-1, keepdims=True)
    yc = y - mu
    var = jnp.mean(yc * yc, axis=-1, keepdims=True)
    yn = yc * lax.rsqrt(var + EPS) * gain_ref[...]
    gate = g_ref[...].astype(F32)
    o_ref[...] = (gate * _sigmoid(gate) * yn).astype(BF16)


def _ret_tables():
    c = RET_CHUNK
    log_g = jnp.log(1.0 - 2.0 ** (-5.0 - jnp.arange(RET_HEADS, dtype=F32)))
    idx = jnp.arange(c, dtype=F32)
    diff = idx[:, None] - idx[None, :]
    kscale = RET_DK ** -0.5
    dec = jnp.where(diff[None] >= 0, jnp.exp(jnp.maximum(diff, 0.0)[None] * log_g[:, None, None]), 0.0) * kscale
    qd = jnp.exp((idx + 1.0)[None, :] * log_g[:, None])
    kd = jnp.exp((c - 1.0 - idx)[None, :] * log_g[:, None]) * kscale
    half = RET_DK // 2
    qd = jnp.broadcast_to(qd[:, :, None], (RET_HEADS, c, half))
    kd = jnp.broadcast_to(kd[:, :, None], (RET_HEADS, c, half))
    cdec = jnp.exp(c * log_g)
    return cdec, dec, qd, kd


def _retention(proj, cos, sin, ret_gain):
    s = proj.shape[0]
    c = RET_CHUNK
    half = RET_DK // 2
    cdec, dec, qd, kd = _ret_tables()
    qb0, kb0 = COL_Q // RET_DK, COL_K // RET_DK
    vb0, gb0 = COL_V // RET_DV, COL_G // RET_DV
    return pl.pallas_call(
        _ret_kernel,
        grid=(RET_HEADS, s // c),
        in_specs=[pl.BlockSpec(memory_space=pltpu.SMEM),
                  pl.BlockSpec((c, RET_DK), lambda h, i: (i, qb0 + h)),
                  pl.BlockSpec((c, RET_DK), lambda h, i: (i, kb0 + h)),
                  pl.BlockSpec((c, RET_DV), lambda h, i: (i, vb0 + h)),
                  pl.BlockSpec((c, RET_DV), lambda h, i: (i, gb0 + h)),
                  pl.BlockSpec((c, half), lambda h, i: (i, 0)),
                  pl.BlockSpec((c, half), lambda h, i: (i, 0)),
                  pl.BlockSpec((None, c, c), lambda h, i: (h, 0, 0)),
                  pl.BlockSpec((None, c, half), lambda h, i: (h, 0, 0)),
                  pl.BlockSpec((None, c, half), lambda h, i: (h, 0, 0)),
                  pl.BlockSpec((1, RET_DV), lambda h, i: (0, h))],
        out_specs=pl.BlockSpec((c, RET_DV), lambda h, i: (i, h)),
        out_shape=jax.ShapeDtypeStruct((s, RET_HEADS * RET_DV), BF16),
        scratch_shapes=[pltpu.VMEM((RET_DK, RET_DV), F32)],
        compiler_params=_params(2, 32),
        name="retention",
    )(cdec, proj, proj, proj, proj, cos, sin, dec, qd, kd, ret_gain.reshape(1, -1))


def _branch_kernel(p_ref, r_ref, ap_ref, ar_ref, wp_ref, wr_ref, o_ref):
    bp = _dot(p_ref[...], wp_ref[...])
    br = _dot(r_ref[...], wr_ref[...])
    ap = ap_ref[...].astype(F32)
    ar = ar_ref[...].astype(F32)
    o_ref[...] = (_sigmoid(ap) * bp + _sigmoid(ar) * br).astype(BF16)


def _branch(pool_out, ret_out, proj, wp_bf16, wr_bf16):
    s = pool_out.shape[0]
    d = wp_bf16.shape[1]
    tm, tn = 512, 512
    ap0, ar0 = COL_APOOL // tn, COL_ARET // tn
    return pl.pallas_call(
        _branch_kernel,
        grid=(s // tm, d // tn),
        in_specs=[pl.BlockSpec((tm, pool_out.shape[1]), lambda i, j: (i, 0)),
                  pl.BlockSpec((tm, ret_out.shape[1]), lambda i, j: (i, 0)),
                  pl.BlockSpec((tm, tn), lambda i, j: (i, ap0 + j)),
                  pl.BlockSpec((tm, tn), lambda i, j: (i, ar0 + j)),
                  pl.BlockSpec((wp_bf16.shape[0], tn), lambda i, j: (0, j)),
                  pl.BlockSpec((wr_bf16.shape[0], tn), lambda i, j: (0, j))],
        out_specs=pl.BlockSpec((tm, tn), lambda i, j: (i, j)),
        out_shape=jax.ShapeDtypeStruct((s, d), BF16),
        compiler_params=_params(2, 40),
        name="branch",
    )(pool_out, ret_out, proj, proj, wp_bf16, wr_bf16)


def _out_kernel(m_ref, w_ref, x_ref, mod_ref, gain_ref, x1_ref, fnt_ref):
    x1 = x_ref[...] + mod_ref[2:3, :] * _dot(m_ref[...], w_ref[...])
    x1_ref[...] = x1
    ms = jnp.mean(x1 * x1, axis=-1, keepdims=True)
    fn = x1 * lax.rsqrt(ms + EPS) * (gain_ref[...] * (1.0 + mod_ref[4:5, :])) + mod_ref[3:4, :]
    fnt_ref[...] = fn.T.astype(BF16)


def _out_proj(merged, w_bf16, x2d, mod, gain):
    s, d = x2d.shape
    tm = 256
    return pl.pallas_call(
        _out_kernel,
        grid=(s // tm,),
        in_specs=[pl.BlockSpec((tm, d), lambda i: (i, 0)),
                  pl.BlockSpec((d, d), lambda i: (0, 0)),
                  pl.BlockSpec((tm, d), lambda i: (i, 0)),
                  pl.BlockSpec((6, d), lambda i: (0, 0)),
                  pl.BlockSpec((1, d), lambda i: (0, 0))],
        out_specs=[pl.BlockSpec((tm, d), lambda i: (i, 0)),
                   pl.BlockSpec((d, tm), lambda i: (0, i))],
        out_shape=[jax.ShapeDtypeStruct((s, d), F32), jax.ShapeDtypeStruct((d, s), BF16)],
        compiler_params=_params(1, 48),
        name="out_proj",
    )(merged, w_bf16, x2d, mod, gain.reshape(1, d))


def _pscore_kernel(fnt_ref, wqt_ref, keys_ref, s_ref):
    t = fnt_ref.shape[1]
    qt = _dot(wqt_ref[...], fnt_ref[...]).astype(BF16)
    for hp in range(2 * PEER_HEADS):
        sc = _dot(keys_ref[hp], qt[hp * PEER_HALF:(hp + 1) * PEER_HALF, :])
        for lc in range(t // LANES):
            s_ref[hp, lc] = sc[:, lc * LANES:(lc + 1) * LANES]


def _peer_scores(fnt, wqt_bf16, keys_bf16):
    d, s = fnt.shape
    t = 512
    nq = wqt_bf16.shape[0]
    return pl.pallas_call(
        _pscore_kernel,
        grid=(s // t,),
        in_specs=[pl.BlockSpec((d, t), lambda i: (0, i)),
                  pl.BlockSpec((nq, d), lambda i: (0, 0)),
                  pl.BlockSpec((2 * PEER_HEADS, PEER_NKEYS, PEER_HALF), lambda i: (0, 0, 0))],
        out_specs=pl.BlockSpec((2 * PEER_HEADS, t // LANES, PEER_NKEYS, LANES), lambda i: (0, i, 0, 0)),
        out_shape=jax.ShapeDtypeStruct((2 * PEER_HEADS, s // LANES, PEER_NKEYS, LANES), F32),
        compiler_params=_params(1, 48),
        name="peer_scores",
    )(fnt, wqt_bf16, keys_bf16)


def _top_values(w, k):
    vals = []
    for _ in range(k):
        m = jnp.max(w, axis=0, keepdims=True)
        vals.append(m)
        w = jnp.where(w == m, -jnp.inf, w)
    return vals


PEER_ETILE = 512
PEER_IB = PEER_ETILE // PEER_NKEYS


def _ptopk_kernel(s_ref, s1r_ref, ar_ref, b_ref, tau_ref):
    n_lc = s_ref.shape[1]

    def body(lc, carry):
        sl = pl.ds(pl.multiple_of(lc * LANES, LANES), LANES)
        s1 = s_ref[0, lc]
        s2 = s_ref[1, lc]
        v1 = _top_values(s1, PEER_TOPK)
        v2 = _top_values(s2, PEER_TOPK)
        v2all = jnp.concatenate(v2, axis=0)
        cand = jnp.concatenate([v1[r] + v2all for r in range(PEER_TOPK)], axis=0)
        best = _top_values(cand, PEER_TOPK)
        z = jnp.ones_like(best[0])
        for r in range(1, PEER_TOPK):
            z = z + jnp.exp(best[r] - best[0])
        a = jnp.exp(s1 - v1[0]) / z
        for g in range(PEER_NKEYS // PEER_IB):
            s1r_ref[g, :, sl] = s1[g * PEER_IB:(g + 1) * PEER_IB, :]
            ar_ref[g, :, sl] = a[g * PEER_IB:(g + 1) * PEER_IB, :]
        b_ref[lc] = jnp.exp(s2 - v2[0])
        tau_ref[:, sl] = best[PEER_TOPK - 1]
        return carry

    lax.fori_loop(0, n_lc, body, 0)


def _peer_topk(scores):
    hp, n_chunks, nk, _ = scores.shape
    s = n_chunks * LANES
    t = 512
    ng = nk // PEER_IB
    rows = jax.ShapeDtypeStruct((PEER_HEADS, ng, PEER_IB, s), F32)
    row_spec = pl.BlockSpec((None, ng, PEER_IB, t), lambda i, h: (h, 0, 0, i))
    slab_spec = pl.BlockSpec((None, t // LANES, nk, LANES), lambda i, h: (h, i, 0, 0))
    return pl.pallas_call(
        _ptopk_kernel,
        grid=(s // t, PEER_HEADS),
        in_specs=[pl.BlockSpec((2, t // LANES, nk, LANES), lambda i, h: (h, i, 0, 0))],
        out_specs=[row_spec, row_spec, slab_spec,
                   pl.BlockSpec((None, 1, t), lambda i, h: (h, 0, i))],
        out_shape=[rows, rows,
                   jax.ShapeDtypeStruct((PEER_HEADS, n_chunks, nk, LANES), F32),
                   jax.ShapeDtypeStruct((PEER_HEADS, 1, s), F32)],
        compiler_params=_params(2, 32),
        name="peer_topk",
    )(scores)


MXU_COLS = 256
Z_ROWS = 128
O_ROWS = 256


def _gelu_tanh(x):
    return 0.5 * x * (1.0 + jnp.tanh(0.7978845608028654 * (x + 0.044715 * (x * x * x))))


def _pdense_step(fnt_ref, u_ref, vt_ref, s1r_ref, ar_ref, s2_ref, b_ref, tau_ref, o_ref,
                 z_prev_ref, z_next_ref, a_prev_ref, a_next_ref):
    t = fnt_ref.shape[1]
    n_lc = t // LANES
    d_model = vt_ref.shape[0]
    n_exp = u_ref.shape[0]
    mxu_chunks = []
    for c0 in range(0, t, MXU_COLS):
        cols = slice(c0, c0 + MXU_COLS)
        for r0 in range(0, n_exp, Z_ROWS):
            mxu_chunks.append(("z", cols, slice(r0, r0 + Z_ROWS)))
        for r0 in range(0, d_model, O_ROWS):
            mxu_chunks.append(("o", cols, slice(r0, r0 + O_ROWS)))
    n_blocks = PEER_IB * n_lc
    place = {}
    for k, chunk in enumerate(mxu_chunks):
        place.setdefault(k * n_blocks // len(mxu_chunks), []).append(chunk)
    for il in range(PEER_IB):
        rows = slice(il * PEER_NKEYS, (il + 1) * PEER_NKEYS)
        for lc in range(n_lc):
            for kind, cols, mrows in place.get(il * n_lc + lc, ()):
                slabs = range(cols.start // LANES, cols.stop // LANES)
                if kind == "z":
                    zc = _dot(u_ref[mrows, :], fnt_ref[:, cols])
                    for k, slab in enumerate(slabs):
                        z_next_ref[slab, mrows, :] = zc[:, k * LANES:(k + 1) * LANES]
                else:
                    ac = jnp.concatenate([a_prev_ref[slab] for slab in slabs], axis=1)
                    o_ref[mrows, cols] += _dot(vt_ref[mrows, :], ac)
            sl = slice(lc * LANES, (lc + 1) * LANES)
            acc = jnp.zeros((PEER_NKEYS, LANES), F32)
            for h in range(PEER_HEADS):
                s1row = s1r_ref[h, il:il + 1, sl]
                arow = ar_ref[h, il:il + 1, sl]
                taurow = tau_ref[h, :, sl]
                s2 = s2_ref[h, lc]
                b = b_ref[h, lc]
                acc = acc + jnp.where(s1row + s2 >= taurow, arow * b, 0.0)
            a_next_ref[lc, rows, :] = (acc * _gelu_tanh(z_prev_ref[lc, rows, :])).astype(BF16)


def _pdense_kernel(fnt_ref, u_ref, vt_ref, s1r_ref, ar_ref, s2_ref, b_ref, tau_ref, o_ref,
                   z0_ref, z1_ref, a0_ref, a1_ref):
    e = pl.program_id(1)
    ins = (fnt_ref, u_ref, vt_ref, s1r_ref, ar_ref, s2_ref, b_ref, tau_ref, o_ref)

    @pl.when(e == 0)
    def _():
        z1_ref[...] = jnp.zeros_like(z1_ref)
        a1_ref[...] = jnp.zeros_like(a1_ref)
        o_ref[...] = jnp.zeros_like(o_ref)

    @pl.when(e % 2 == 0)
    def _():
        _pdense_step(*ins, z1_ref, z0_ref, a1_ref, a0_ref)

    @pl.when(e % 2 == 1)
    def _():
        _pdense_step(*ins, z0_ref, z1_ref, a0_ref, a1_ref)


def _peer_dense(fnt, u_bf16, vt_bf16, scores, s1r, ar, b, tau):
    d, s = fnt.shape
    ne = u_bf16.shape[0]
    t = 512
    et = PEER_ETILE
    n_et = ne // et
    n_chunks, nk = scores.shape[1], scores.shape[2]
    scores5 = scores.reshape(PEER_HEADS, 2, n_chunks, nk, LANES)
    n_lc = t // LANES
    last = n_et - 1

    def tile(e, lag):
        return jnp.clip(e - lag, 0, last)

    row_spec = pl.BlockSpec((PEER_HEADS, None, PEER_IB, t), lambda i, e: (0, tile(e, 1), 0, i))
    return pl.pallas_call(
        _pdense_kernel,
        grid=(s // t, n_et + 2),
        in_specs=[pl.BlockSpec((d, t), lambda i, e: (0, i)),
                  pl.BlockSpec((et, d), lambda i, e: (tile(e, 0), 0)),
                  pl.BlockSpec((d, et), lambda i, e: (0, tile(e, 2))),
                  row_spec, row_spec,
                  pl.BlockSpec((PEER_HEADS, None, n_lc, nk, LANES), lambda i, e: (0, 1, i, 0, 0)),
                  pl.BlockSpec((PEER_HEADS, n_lc, nk, LANES), lambda i, e: (0, i, 0, 0)),
                  pl.BlockSpec((PEER_HEADS, 1, t), lambda i, e: (0, 0, i))],
        out_specs=pl.BlockSpec((d, t), lambda i, e: (0, i)),
        out_shape=jax.ShapeDtypeStruct((d, s), F32),
        scratch_shapes=[pltpu.VMEM((n_lc, et, LANES), F32), pltpu.VMEM((n_lc, et, LANES), F32),
                        pltpu.VMEM((n_lc, et, LANES), BF16), pltpu.VMEM((n_lc, et, LANES), BF16)],
        compiler_params=_params(2, 48),
        name="peer_dense",
    )(fnt, u_bf16, vt_bf16, s1r, ar, scores5, b, tau)


def _final_kernel(x1_ref, ot_ref, mod_ref, gain_ref, y_ref, *, apply_norm):
    x2 = x1_ref[...] + mod_ref[5:6, :] * ot_ref[...].T
    if apply_norm:
        ms = jnp.mean(x2 * x2, axis=-1, keepdims=True)
        x2 = x2 * lax.rsqrt(ms + EPS) * gain_ref[...]
    y_ref[...] = x2


def _final(x1, out_t, mod, gain, apply_norm):
    s, d = x1.shape
    tm = 512
    return pl.pallas_call(
        functools.partial(_final_kernel, apply_norm=apply_norm),
        grid=(s // tm,),
        in_specs=[pl.BlockSpec((tm, d), lambda i: (i, 0)),
                  pl.BlockSpec((d, tm), lambda i: (0, i)),
                  pl.BlockSpec((6, d), lambda i: (0, 0)),
                  pl.BlockSpec((1, d), lambda i: (0, 0))],
        out_specs=pl.BlockSpec((tm, d), lambda i: (i, 0)),
        out_shape=jax.ShapeDtypeStruct((s, d), F32),
        compiler_params=_params(1, 48),
        name="final",
    )(x1, out_t, mod, gain.reshape(1, d))


def kernel(x, c, positions, norm_mix_gain, w_ada, b_ada, w_in, pool_w, pool_scale, ret_norm_gain,
           w_branch_pool, w_branch_ret, w_out, norm_ffn_gain, peer_w_query, peer_sub_keys, peer_u, peer_v,
           final_norm_gain):
    batch, s, d = x.shape
    assert batch == 1 and d == D_MODEL and s % 1024 == 0
    depth = w_in.shape[0]
    xs = x.reshape(s, d)
    cos, sin = _rope_tables(positions.reshape(s))
    for l in range(depth):
        mod = _ada(c, w_ada[l], b_ada[l])
        proj = _in_proj(xs, norm_mix_gain[l], mod, w_in[l].astype(BF16))
        pool_out = _pool(proj, pool_w[l].astype(BF16), pool_scale[l])
        ret_out = _retention(proj, cos, sin, ret_norm_gain[l])
        merged = _branch(pool_out, ret_out, proj, w_branch_pool[l].astype(BF16), w_branch_ret[l].astype(BF16))
        x1, fnt = _out_proj(merged, w_out[l].astype(BF16), xs, mod, norm_ffn_gain[l])
        keys = peer_sub_keys[l].reshape(2 * PEER_HEADS, PEER_NKEYS, PEER_HALF).astype(BF16)
        scores = _peer_scores(fnt, peer_w_query[l].T.astype(BF16), keys)
        s1r, ar, b, tau = _peer_topk(scores)
        out_t = _peer_dense(fnt, peer_u[l].astype(BF16), peer_v[l].T.astype(BF16), scores, s1r, ar, b, tau)
        xs = _final(x1, out_t, mod, final_norm_gain, apply_norm=(l == depth - 1))
    return xs.reshape(batch, s, d)
```

```python
import functools

import jax
import jax.numpy as jnp
from jax import lax
from jax.experimental import pallas as pl
from jax.experimental.pallas import tpu as pltpu

F32 = jnp.float32
BF16 = jnp.bfloat16

D_MODEL = 2048
EPS = 1e-6
POOL_GROUPS = 4
POOL_GDIM = 512
RET_HEADS = 8
RET_DK = 256
RET_DV = 512
ROPE_BASE = 10000.0
PEER_HEADS = 8
PEER_NKEYS = 128
PEER_HALF = 128
PEER_TOPK = 16
N_EXPERTS = PEER_NKEYS * PEER_NKEYS

COL_POOL, COL_Q, COL_K, COL_V, COL_G, COL_APOOL, COL_ARET = 0, 2048, 4096, 6144, 10240, 14336, 16384
IN_COLS = 18432

LANES = 128
MIB = 1024 * 1024

RET_CHUNK = 256


def _params(n_axes, vmem_mib, flags=None):
    return pltpu.CompilerParams(dimension_semantics=("arbitrary",) * n_axes,
                                vmem_limit_bytes=vmem_mib * MIB, flags=flags)


def _dot(a, b):
    return jnp.dot(a, b, preferred_element_type=F32)


def _sigmoid(x):
    return 1.0 / (1.0 + jnp.exp(-x))


def _ada_kernel(c_ref, w_ref, b_ref, o_ref):
    c = c_ref[...]
    cond = c * _sigmoid(c)
    o_ref[...] = jnp.dot(cond, w_ref[...], preferred_element_type=F32,
                         precision=lax.Precision.HIGHEST) + b_ref[...]


def _ada(c, w, b):
    d, n = w.shape
    tn = 1024
    c8 = jnp.broadcast_to(c, (8, d))
    out = pl.pallas_call(
        _ada_kernel,
        grid=(n // tn,),
        in_specs=[pl.BlockSpec((8, d), lambda j: (0, 0)),
                  pl.BlockSpec((d, tn), lambda j: (0, j)),
                  pl.BlockSpec((1, tn), lambda j: (0, j))],
        out_specs=pl.BlockSpec((8, tn), lambda j: (0, j)),
        out_shape=jax.ShapeDtypeStruct((8, n), F32),
        compiler_params=_params(1, 40),
        name="ada",
    )(c8, w, b.reshape(1, n))
    return out[0].reshape(6, d)


def _in_kernel(x_ref, gain_ref, mod_ref, w_ref, o_ref, hn_ref):
    tm = x_ref.shape[0]
    rc = 256

    @pl.when(pl.program_id(1) == 0)
    def _():
        scale = gain_ref[...] * (1.0 + mod_ref[1:2, :])
        shift = mod_ref[0:1, :]

        def body(r, carry):
            rows = pl.ds(pl.multiple_of(r * rc, rc), rc)
            x = x_ref[rows, :]
            ms = jnp.mean(x * x, axis=-1, keepdims=True)
            hn_ref[rows, :] = (x * lax.rsqrt(ms + EPS) * scale + shift).astype(BF16)
            return carry

        lax.fori_loop(0, tm // rc, body, 0)

    o_ref[...] = _dot(hn_ref[...], w_ref[...]).astype(BF16)


def _col_tiles(w, tn):
    k, n = w.shape
    return w.astype(BF16).reshape(k, n // tn, tn).transpose(1, 0, 2)


def _in_proj(x2d, gain, mod, w):
    s, d = x2d.shape
    n = w.shape[1]
    tm, tn = 1024, 1024
    return pl.pallas_call(
        _in_kernel,
        grid=(s // tm, n // tn),
        in_specs=[pl.BlockSpec((tm, d), lambda i, j: (i, 0)),
                  pl.BlockSpec((1, d), lambda i, j: (0, 0)),
                  pl.BlockSpec((6, d), lambda i, j: (0, 0)),
                  pl.BlockSpec((None, d, tn), lambda i, j: (j, 0, 0))],
        out_specs=pl.BlockSpec((tm, tn), lambda i, j: (i, j)),
        out_shape=jax.ShapeDtypeStruct((s, n), BF16),
        scratch_shapes=[pltpu.VMEM((tm, d), BF16)],
        compiler_params=_params(2, 48),
        name="in_proj",
    )(x2d, gain.reshape(1, d), mod, _col_tiles(w, tn))


def _rope_kernel(pos_ref, inv_ref, cos_ref, sin_ref):
    ang = pos_ref[...].astype(F32) * inv_ref[...]
    cos_ref[...] = jnp.cos(ang)
    sin_ref[...] = jnp.sin(ang)


def _rope_tables(positions):
    s = positions.shape[0]
    half = RET_DK // 2
    inv_freq = (ROPE_BASE ** (-jnp.arange(0, RET_DK, 2, dtype=F32) / RET_DK)).reshape(1, half)
    tm = 1024
    return pl.pallas_call(
        _rope_kernel,
        grid=(s // tm,),
        in_specs=[pl.BlockSpec((tm, 1), lambda i: (i, 0)),
                  pl.BlockSpec((1, half), lambda i: (0, 0))],
        out_specs=[pl.BlockSpec((tm, half), lambda i: (i, 0)),
                   pl.BlockSpec((tm, half), lambda i: (i, 0))],
        out_shape=[jax.ShapeDtypeStruct((s, half), F32), jax.ShapeDtypeStruct((s, half), F32)],
        compiler_params=_params(1, 32),
        name="rope",
    )(positions.reshape(s, 1), inv_freq)


POOL_HALO = 16


def _pool_kernel(u_ref, halo_ref, pw_ref, ps_ref, o_ref):
    i = pl.program_id(0)
    g = pl.program_id(1)
    tm = u_ref.shape[0]
    w = jnp.left_shift(2, g)
    r = lax.broadcasted_iota(jnp.int32, (tm, tm), 0)
    c = lax.broadcasted_iota(jnp.int32, (tm, tm), 1)
    band = jnp.where(c <= r, jnp.where(c > r - w, 1.0, 0.0), 0.0).astype(BF16)
    rh = lax.broadcasted_iota(jnp.int32, (tm, POOL_HALO), 0)
    ch = lax.broadcasted_iota(jnp.int32, (tm, POOL_HALO), 1)
    halo_on = jnp.where(i > 0, 1.0, 0.0)
    bandh = (jnp.where(ch > rh + POOL_HALO - w, 1.0, 0.0) * halo_on).astype(BF16)
    u = u_ref[...]
    wsum = _dot(band, u) + _dot(bandh, halo_ref[...])
    t = i * tm + lax.broadcasted_iota(jnp.int32, (tm, 1), 0)
    cnt = jnp.minimum(t + 1, w).astype(F32)
    pooled = wsum / cnt - u.astype(F32)
    o_ref[...] = (_dot(pooled.astype(BF16), pw_ref[...]) * ps_ref[...]).astype(BF16)


def _pool(proj, pool_w_bf16, pool_scale):
    s = proj.shape[0]
    tm = 512
    hb = tm // POOL_HALO
    return pl.pallas_call(
        _pool_kernel,
        grid=(s // tm, POOL_GROUPS),
        in_specs=[pl.BlockSpec((tm, POOL_GDIM), lambda i, g: (i, g)),
                  pl.BlockSpec((POOL_HALO, POOL_GDIM), lambda i, g: (jnp.maximum(i * hb - 1, 0), g)),
                  pl.BlockSpec((None, POOL_GDIM, POOL_GDIM), lambda i, g: (g, 0, 0)),
                  pl.BlockSpec((1, POOL_GDIM), lambda i, g: (0, g))],
        out_specs=pl.BlockSpec((tm, POOL_GDIM), lambda i, g: (i, g)),
        out_shape=jax.ShapeDtypeStruct((s, POOL_GROUPS * POOL_GDIM), BF16),
        compiler_params=_params(2, 32),
        name="pool",
    )(proj, proj, pool_w_bf16, pool_scale.reshape(1, -1))


def _ret_kernel(cdec_ref, q_ref, k_ref, v_ref, g_ref, cos_ref, sin_ref, dec_ref, qd_ref, kd_ref,
                gain_ref, o_ref, state_ref):
    h = pl.program_id(0)

    @pl.when(pl.program_id(1) == 0)
    def _():
        state_ref[...] = jnp.zeros_like(state_ref)

    cos = cos_ref[...]
    sin = sin_ref[...]
    half = RET_DK // 2

    def rot(t_ref):
        t = t_ref[...].astype(F32)
        t1 = t[:, :half]
        t2 = t[:, half:]
        return t1 * cos - t2 * sin, t2 * cos + t1 * sin

    q1, q2 = rot(q_ref)
    k1, k2 = rot(k_ref)
    qd = qd_ref[...]
    kd = kd_ref[...]
    qb = jnp.concatenate([q1, q2], axis=1).astype(BF16)
    kb = jnp.concatenate([k1, k2], axis=1).astype(BF16)
    qdb = jnp.concatenate([q1 * qd, q2 * qd], axis=1).astype(BF16)
    kdb = jnp.concatenate([k1 * kd, k2 * kd], axis=1).astype(BF16)
    v = v_ref[...]
    scores = lax.dot_general(qb, kb, (((1,), (1,)), ((), ())), preferred_element_type=F32) * dec_ref[...]
    inner = _dot(scores.astype(BF16), v)
    state = state_ref[...]
    cross = _dot(qdb, state.astype(BF16))
    state_ref[...] = state * cdec_ref[h] + lax.dot_general(
        kdb, v, (((0,), (0,)), ((), ())), preferred_element_type=F32)
    y = inner + cross
    mu = jnp.mean(y, axis=-1, keepdims=True)
    yc = y - mu
    var = jnp.mean(yc * yc, axis=-1, keepdims=True)
    yn = yc * lax.rsqrt(var + EPS) * gain_ref[...]
    gate = g_ref[...].astype(F32)
    o_ref[...] = (gate * _sigmoid(gate) * yn).astype(BF16)


def _ret_tables():
    c = RET_CHUNK
    log_g = jnp.log(1.0 - 2.0 ** (-5.0 - jnp.arange(RET_HEADS, dtype=F32)))
    idx = jnp.arange(c, dtype=F32)
    diff = idx[:, None] - idx[None, :]
    kscale = RET_DK ** -0.5
    dec = jnp.where(diff[None] >= 0, jnp.exp(jnp.maximum(diff, 0.0)[None] * log_g[:, None, None]), 0.0) * kscale
    qd = jnp.exp((idx + 1.0)[None, :] * log_g[:, None])
    kd = jnp.exp((c - 1.0 - idx)[None, :] * log_g[:, None]) * kscale
    half = RET_DK // 2
    qd = jnp.broadcast_to(qd[:, :, None], (RET_HEADS, c, half))
    kd = jnp.broadcast_to(kd[:, :, None], (RET_HEADS, c, half))
    cdec = jnp.exp(c * log_g)
    return cdec, dec, qd, kd


def _retention(proj, cos, sin, ret_gain):
    s = proj.shape[0]
    c = RET_CHUNK
    half = RET_DK // 2
    cdec, dec, qd, kd = _ret_tables()
    qb0, kb0 = COL_Q // RET_DK, COL_K // RET_DK
    vb0, gb0 = COL_V // RET_DV, COL_G // RET_DV
    return pl.pallas_call(
        _ret_kernel,
        grid=(RET_HEADS, s // c),
        in_specs=[pl.BlockSpec(memory_space=pltpu.SMEM),
                  pl.BlockSpec((c, RET_DK), lambda h, i: (i, qb0 + h)),
                  pl.BlockSpec((c, RET_DK), lambda h, i: (i, kb0 + h)),
                  pl.BlockSpec((c, RET_DV), lambda h, i: (i, vb0 + h)),
                  pl.BlockSpec((c, RET_DV), lambda h, i: (i, gb0 + h)),
                  pl.BlockSpec((c, half), lambda h, i: (i, 0)),
                  pl.BlockSpec((c, half), lambda h, i: (i, 0)),
                  pl.BlockSpec((None, c, c), lambda h, i: (h, 0, 0)),
                  pl.BlockSpec((None, c, half), lambda h, i: (h, 0, 0)),
                  pl.BlockSpec((None, c, half), lambda h, i: (h, 0, 0)),
                  pl.BlockSpec((1, RET_DV), lambda h, i: (0, h))],
        out_specs=pl.BlockSpec((c, RET_DV), lambda h, i: (i, h)),
        out_shape=jax.ShapeDtypeStruct((s, RET_HEADS * RET_DV), BF16),
        scratch_shapes=[pltpu.VMEM((RET_DK, RET_DV), F32)],
        compiler_params=_params(2, 32),
        name="retention",
    )(cdec, proj, proj, proj, proj, cos, sin, dec, qd, kd, ret_gain.reshape(1, -1))


def _branch_kernel(p_ref, r_ref, ap_ref, ar_ref, wp_ref, wr_ref, o_ref):
    bp = _dot(p_ref[...], wp_ref[...])
    br = _dot(r_ref[...], wr_ref[...])
    ap = ap_ref[...].astype(F32)
    ar = ar_ref[...].astype(F32)
    o_ref[...] = (_sigmoid(ap) * bp + _sigmoid(ar) * br).astype(BF16)


def _branch(pool_out, ret_out, proj, wp, wr):
    s = pool_out.shape[0]
    d = wp.shape[1]
    tm, tn = 512, 512
    ap0, ar0 = COL_APOOL // tn, COL_ARET // tn
    return pl.pallas_call(
        _branch_kernel,
        grid=(s // tm, d // tn),
        in_specs=[pl.BlockSpec((tm, pool_out.shape[1]), lambda i, j: (i, 0)),
                  pl.BlockSpec((tm, ret_out.shape[1]), lambda i, j: (i, 0)),
                  pl.BlockSpec((tm, tn), lambda i, j: (i, ap0 + j)),
                  pl.BlockSpec((tm, tn), lambda i, j: (i, ar0 + j)),
                  pl.BlockSpec((None, wp.shape[0], tn), lambda i, j: (j, 0, 0)),
                  pl.BlockSpec((None, wr.shape[0], tn), lambda i, j: (j, 0, 0))],
        out_specs=pl.BlockSpec((tm, tn), lambda i, j: (i, j)),
        out_shape=jax.ShapeDtypeStruct((s, d), BF16),
        compiler_params=_params(2, 40),
        name="branch",
    )(pool_out, ret_out, proj, proj, _col_tiles(wp, tn), _col_tiles(wr, tn))


def _out_kernel(m_ref, w_ref, x_ref, mod_ref, gain_ref, x1_ref, fnt_ref):
    x1 = x_ref[...] + mod_ref[2:3, :] * _dot(m_ref[...], w_ref[...])
    x1_ref[...] = x1
    ms = jnp.mean(x1 * x1, axis=-1, keepdims=True)
    fn = x1 * lax.rsqrt(ms + EPS) * (gain_ref[...] * (1.0 + mod_ref[4:5, :])) + mod_ref[3:4, :]
    fnt_ref[...] = fn.T.astype(BF16)


def _out_proj(merged, w_bf16, x2d, mod, gain):
    s, d = x2d.shape
    tm = 256
    return pl.pallas_call(
        _out_kernel,
        grid=(s // tm,),
        in_specs=[pl.BlockSpec((tm, d), lambda i: (i, 0)),
                  pl.BlockSpec((d, d), lambda i: (0, 0)),
                  pl.BlockSpec((tm, d), lambda i: (i, 0)),
                  pl.BlockSpec((6, d), lambda i: (0, 0)),
                  pl.BlockSpec((1, d), lambda i: (0, 0))],
        out_specs=[pl.BlockSpec((tm, d), lambda i: (i, 0)),
                   pl.BlockSpec((d, tm), lambda i: (0, i))],
        out_shape=[jax.ShapeDtypeStruct((s, d), F32), jax.ShapeDtypeStruct((d, s), BF16)],
        compiler_params=_params(1, 48),
        name="out_proj",
    )(merged, w_bf16, x2d, mod, gain.reshape(1, d))


def _pscore_kernel(fnt_ref, wqt_ref, keys_ref, s_ref):
    t = fnt_ref.shape[1]
    qt = _dot(wqt_ref[...], fnt_ref[...]).astype(BF16)
    for hp in range(2 * PEER_HEADS):
        sc = _dot(keys_ref[hp], qt[hp * PEER_HALF:(hp + 1) * PEER_HALF, :])
        for lc in range(t // LANES):
            s_ref[hp, lc] = sc[:, lc * LANES:(lc + 1) * LANES]


def _peer_scores(fnt, wqt_bf16, keys_bf16):
    d, s = fnt.shape
    t = 512
    nq = wqt_bf16.shape[0]
    return pl.pallas_call(
        _pscore_kernel,
        grid=(s // t,),
        in_specs=[pl.BlockSpec((d, t), lambda i: (0, i)),
                  pl.BlockSpec((nq, d), lambda i: (0, 0)),
                  pl.BlockSpec((2 * PEER_HEADS, PEER_NKEYS, PEER_HALF), lambda i: (0, 0, 0))],
        out_specs=pl.BlockSpec((2 * PEER_HEADS, t // LANES, PEER_NKEYS, LANES), lambda i: (0, i, 0, 0)),
        out_shape=jax.ShapeDtypeStruct((2 * PEER_HEADS, s // LANES, PEER_NKEYS, LANES), F32),
        compiler_params=_params(1, 48),
        name="peer_scores",
    )(fnt, wqt_bf16, keys_bf16)


SUBLANES = 8
LOG2E = 1.4426950408889634


def _sort_network(n):
    size = 1
    while size < n:
        size *= 2

    def merge(lo, hi, r):
        step = r * 2
        if step < hi - lo:
            yield from merge(lo, hi, step)
            yield from merge(lo + r, hi, step)
            for i in range(lo + r, hi - r, step):
                yield (i, i + r)
        else:
            yield (lo, lo + r)

    def sort(lo, hi):
        if hi - lo >= 1:
            mid = lo + (hi - lo) // 2
            yield from sort(lo, mid)
            yield from sort(mid + 1, hi)
            yield from merge(lo, hi, 1)

    return [(i, j) for i, j in sort(0, size - 1) if j < n]


def _top_values(slabs, k):
    cols = list(slabs)
    for i, j in _sort_network(len(cols)):
        hi = jnp.maximum(cols[i], cols[j])
        cols[j] = jnp.minimum(cols[i], cols[j])
        cols[i] = hi
    vals = []
    for r in range(k):
        m = jnp.max(cols[0], axis=0, keepdims=True)
        vals.append(m)
        depth = min(len(cols), k - r)
        if r == k - 1:
            break
        hit = cols[0] == m
        for q in range(depth - 1):
            cols[q] = jnp.where(hit, cols[q + 1], cols[q])
        if depth == len(cols):
            cols[depth - 1] = jnp.where(hit, -jnp.inf, cols[depth - 1])
    return vals


def _rows_to_slabs(rows, n_slabs, row_id):
    slabs = []
    for g in range(n_slabs):
        slab = jnp.full(row_id.shape, -jnp.inf, F32)
        for q in range(SUBLANES):
            r = g * SUBLANES + q
            if r < len(rows):
                slab = jnp.where(row_id == q, rows[r], slab)
        slabs.append(slab)
    return slabs


PEER_ETILE = 512
PEER_IB = PEER_ETILE // PEER_NKEYS


PEER_NTOP = PEER_TOPK + 1


def _ptopk_kernel(s_ref, alpha_ref, beta_ref, tau_ref):
    n_lc = s_ref.shape[1]
    n_slabs = PEER_NKEYS // SUBLANES
    row_id = lax.broadcasted_iota(jnp.int32, (SUBLANES, LANES), 0)

    def body(lc, carry):
        sl = pl.ds(pl.multiple_of(lc * LANES, LANES), LANES)
        s1 = s_ref[0, lc]
        s2 = s_ref[1, lc]
        v1 = _top_values([s1[g * SUBLANES:(g + 1) * SUBLANES, :] for g in range(n_slabs)], PEER_NTOP)
        v2 = _top_values([s2[g * SUBLANES:(g + 1) * SUBLANES, :] for g in range(n_slabs)], PEER_NTOP)
        n_vs = -(-PEER_NTOP // SUBLANES)
        v1s = _rows_to_slabs(v1, n_vs, row_id)
        v2s = _rows_to_slabs(v2, n_vs, row_id)
        cand = [v1[0] + slab for slab in v2s]
        for r1 in range(1, SUBLANES):
            n_ok = PEER_NTOP // (r1 + 1)
            pair = v1[r1] + v2s[0]
            cand.append(pair if n_ok >= SUBLANES else jnp.where(row_id < n_ok, pair, -jnp.inf))
        for slab in v1s[1:]:
            cand.append(slab + v2[0])
        best = _top_values(cand, PEER_NTOP)
        z = jnp.ones_like(best[0])
        for r in range(1, PEER_TOPK):
            z = z + jnp.exp(best[r] - best[0])
        log_norm = best[0] + jnp.log(z)
        alpha = (s1 - log_norm) * LOG2E
        for g in range(PEER_NKEYS // PEER_IB):
            alpha_ref[g, :, sl] = alpha[g * PEER_IB:(g + 1) * PEER_IB, :]
        beta_ref[lc] = s2 * LOG2E
        tau_ref[:, sl] = (0.5 * (best[PEER_TOPK - 1] + best[PEER_TOPK]) - log_norm) * LOG2E
        return carry

    lax.fori_loop(0, n_lc, body, 0)


def _peer_topk(scores):
    hp, n_chunks, nk, _ = scores.shape
    s = n_chunks * LANES
    t = 512
    ng = nk // PEER_IB
    return pl.pallas_call(
        _ptopk_kernel,
        grid=(s // t, PEER_HEADS),
        in_specs=[pl.BlockSpec((2, t // LANES, nk, LANES), lambda i, h: (h, i, 0, 0))],
        out_specs=[pl.BlockSpec((None, ng, PEER_IB, t), lambda i, h: (h, 0, 0, i)),
                   pl.BlockSpec((None, t // LANES, nk, LANES), lambda i, h: (h, i, 0, 0)),
                   pl.BlockSpec((None, 1, t), lambda i, h: (h, 0, i))],
        out_shape=[jax.ShapeDtypeStruct((PEER_HEADS, ng, PEER_IB, s), F32),
                   jax.ShapeDtypeStruct((PEER_HEADS, n_chunks, nk, LANES), F32),
                   jax.ShapeDtypeStruct((PEER_HEADS, 1, s), F32)],
        compiler_params=_params(2, 32),
        name="peer_topk",
    )(scores)


MXU_COLS = 256
Z_ROWS = 128
O_ROWS = 256


def _gelu_tanh(x):
    return 0.5 * x * (1.0 + jnp.tanh(0.7978845608028654 * (x + 0.044715 * (x * x * x))))


def _pdense_step(fnt_ref, u_ref, vt_ref, alpha_ref, beta_ref, tau_ref, o_ref,
                 z_prev_ref, z_next_ref, a_prev_ref, a_next_ref):
    t = fnt_ref.shape[1]
    n_lc = t // LANES
    d_model = vt_ref.shape[0]
    n_exp = u_ref.shape[0]
    mxu_chunks = []
    for c0 in range(0, t, MXU_COLS):
        cols = slice(c0, c0 + MXU_COLS)
        for r0 in range(0, n_exp, Z_ROWS):
            mxu_chunks.append(("z", cols, slice(r0, r0 + Z_ROWS)))
        for r0 in range(0, d_model, O_ROWS):
            mxu_chunks.append(("o", cols, slice(r0, r0 + O_ROWS)))
    n_blocks = PEER_IB * n_lc
    place = {}
    for k, chunk in enumerate(mxu_chunks):
        place.setdefault(k * n_blocks // len(mxu_chunks), []).append(chunk)
    for il in range(PEER_IB):
        rows = slice(il * PEER_NKEYS, (il + 1) * PEER_NKEYS)
        for lc in range(n_lc):
            for kind, cols, mrows in place.get(il * n_lc + lc, ()):
                slabs = range(cols.start // LANES, cols.stop // LANES)
                if kind == "z":
                    zc = _dot(u_ref[mrows, :], fnt_ref[:, cols])
                    for k, slab in enumerate(slabs):
                        z_next_ref[slab, mrows, :] = zc[:, k * LANES:(k + 1) * LANES]
                else:
                    ac = jnp.concatenate([a_prev_ref[slab] for slab in slabs], axis=1)
                    o_ref[mrows, cols] += _dot(vt_ref[mrows, :], ac)
            sl = slice(lc * LANES, (lc + 1) * LANES)
            acc = jnp.zeros((PEER_NKEYS, LANES), F32)
            for h in range(PEER_HEADS):
                lg = alpha_ref[h, il:il + 1, sl] + beta_ref[h, lc]
                acc = acc + jnp.where(lg >= tau_ref[h, :, sl], jnp.exp2(lg), 0.0)
            a_next_ref[lc, rows, :] = (acc * _gelu_tanh(z_prev_ref[lc, rows, :])).astype(BF16)


def _pdense_kernel(fnt_ref, u_ref, vt_ref, alpha_ref, beta_ref, tau_ref, o_ref,
                   z0_ref, z1_ref, a0_ref, a1_ref):
    e = pl.program_id(1)
    ins = (fnt_ref, u_ref, vt_ref, alpha_ref, beta_ref, tau_ref, o_ref)

    @pl.when(e == 0)
    def _():
        z1_ref[...] = jnp.zeros_like(z1_ref)
        a1_ref[...] = jnp.zeros_like(a1_ref)
        o_ref[...] = jnp.zeros_like(o_ref)

    @pl.when(e % 2 == 0)
    def _():
        _pdense_step(*ins, z1_ref, z0_ref, a1_ref, a0_ref)

    @pl.when(e % 2 == 1)
    def _():
        _pdense_step(*ins, z0_ref, z1_ref, a0_ref, a1_ref)


def _peer_dense(fnt, u_bf16, vt_bf16, alpha, beta, tau):
    d, s = fnt.shape
    ne = u_bf16.shape[0]
    t = 512
    et = PEER_ETILE
    n_et = ne // et
    nk = beta.shape[2]
    n_lc = t // LANES
    last = n_et - 1

    def tile(e, lag):
        return jnp.clip(e - lag, 0, last)

    return pl.pallas_call(
        _pdense_kernel,
        grid=(s // t, n_et + 2),
        in_specs=[pl.BlockSpec((d, t), lambda i, e: (0, i)),
                  pl.BlockSpec((et, d), lambda i, e: (tile(e, 0), 0)),
                  pl.BlockSpec((None, d, et), lambda i, e: (tile(e, 2), 0, 0)),
                  pl.BlockSpec((PEER_HEADS, None, PEER_IB, t), lambda i, e: (0, tile(e, 1), 0, i)),
                  pl.BlockSpec((PEER_HEADS, n_lc, nk, LANES), lambda i, e: (0, i, 0, 0)),
                  pl.BlockSpec((PEER_HEADS, 1, t), lambda i, e: (0, 0, i))],
        out_specs=pl.BlockSpec((d, t), lambda i, e: (0, i)),
        out_shape=jax.ShapeDtypeStruct((d, s), F32),
        scratch_shapes=[pltpu.VMEM((n_lc, et, LANES), F32), pltpu.VMEM((n_lc, et, LANES), F32),
                        pltpu.VMEM((n_lc, et, LANES), BF16), pltpu.VMEM((n_lc, et, LANES), BF16)],
        compiler_params=_params(2, 48),
        name="peer_dense",
    )(fnt, u_bf16, vt_bf16, alpha, beta, tau)


def _final_kernel(x1_ref, ot_ref, mod_ref, gain_ref, y_ref, *, apply_norm):
    x2 = x1_ref[...] + mod_ref[5:6, :] * ot_ref[...].T
    if apply_norm:
        ms = jnp.mean(x2 * x2, axis=-1, keepdims=True)
        x2 = x2 * lax.rsqrt(ms + EPS) * gain_ref[...]
    y_ref[...] = x2


def _final(x1, out_t, mod, gain, apply_norm):
    s, d = x1.shape
    tm = 512
    return pl.pallas_call(
        functools.partial(_final_kernel, apply_norm=apply_norm),
        grid=(s // tm,),
        in_specs=[pl.BlockSpec((tm, d), lambda i: (i, 0)),
                  pl.BlockSpec((d, tm), lambda i: (0, i)),
                  pl.BlockSpec((6, d), lambda i: (0, 0)),
                  pl.BlockSpec((1, d), lambda i: (0, 0))],
        out_specs=pl.BlockSpec((tm, d), lambda i: (i, 0)),
        out_shape=jax.ShapeDtypeStruct((s, d), F32),
        compiler_params=_params(1, 48),
        name="final",
    )(x1, out_t, mod, gain.reshape(1, d))


def kernel(x, c, positions, norm_mix_gain, w_ada, b_ada, w_in, pool_w, pool_scale, ret_norm_gain,
           w_branch_pool, w_branch_ret, w_out, norm_ffn_gain, peer_w_query, peer_sub_keys, peer_u, peer_v,
           final_norm_gain):
    batch, s, d = x.shape
    assert batch == 1 and d == D_MODEL and s % 1024 == 0
    depth = w_in.shape[0]
    xs = x.reshape(s, d)
    cos, sin = _rope_tables(positions.reshape(s))
    for l in range(depth):
        mod = _ada(c, w_ada[l], b_ada[l])
        proj = _in_proj(xs, norm_mix_gain[l], mod, w_in[l])
        pool_out = _pool(proj, pool_w[l].astype(BF16), pool_scale[l])
        ret_out = _retention(proj, cos, sin, ret_norm_gain[l])
        merged = _branch(pool_out, ret_out, proj, w_branch_pool[l], w_branch_ret[l])
        x1, fnt = _out_proj(merged, w_out[l].astype(BF16), xs, mod, norm_ffn_gain[l])
        keys = peer_sub_keys[l].reshape(2 * PEER_HEADS, PEER_NKEYS, PEER_HALF).astype(BF16)
        scores = _peer_scores(fnt, peer_w_query[l].T.astype(BF16), keys)
        alpha, beta, tau = _peer_topk(scores)
        vt_tiles = peer_v[l].astype(BF16).reshape(N_EXPERTS // PEER_ETILE, PEER_ETILE, d).transpose(0, 2, 1)
        out_t = _peer_dense(fnt, peer_u[l].astype(BF16), vt_tiles, alpha, beta, tau)
        xs = _final(x1, out_t, mod, final_norm_gain, apply_norm=(l == depth - 1))
    return xs.reshape(batch, s, d)
```

```python
import functools

import jax
import jax.numpy as jnp
from jax import lax
from jax.experimental import pallas as pl
from jax.experimental.pallas import tpu as pltpu

F32 = jnp.float32
BF16 = jnp.bfloat16

D_MODEL = 2048
EPS = 1e-6
POOL_GROUPS = 4
POOL_GDIM = 512
RET_HEADS = 8
RET_DK = 256
RET_DV = 512
ROPE_BASE = 10000.0
PEER_HEADS = 8
PEER_NKEYS = 128
PEER_HALF = 128
PEER_TOPK = 16
N_EXPERTS = PEER_NKEYS * PEER_NKEYS

COL_POOL, COL_Q, COL_K, COL_V, COL_G, COL_APOOL, COL_ARET = 0, 2048, 4096, 6144, 10240, 14336, 16384
IN_COLS = 18432

LANES = 128
MIB = 1024 * 1024

RET_CHUNK = 256


def _params(n_axes, vmem_mib, flags=None):
    return pltpu.CompilerParams(dimension_semantics=("arbitrary",) * n_axes,
                                vmem_limit_bytes=vmem_mib * MIB, flags=flags)


def _dot(a, b):
    return jnp.dot(a, b, preferred_element_type=F32)


def _sigmoid(x):
    return 1.0 / (1.0 + jnp.exp(-x))


def _ada_kernel(c_ref, w_ref, b_ref, o_ref):
    c = c_ref[...]
    cond = c * _sigmoid(c)
    o_ref[...] = jnp.dot(cond, w_ref[...], preferred_element_type=F32,
                         precision=lax.Precision.HIGHEST) + b_ref[...]


def _ada(c, w, b):
    d, n = w.shape
    tn = 1024
    c8 = jnp.broadcast_to(c, (8, d))
    out = pl.pallas_call(
        _ada_kernel,
        grid=(n // tn,),
        in_specs=[pl.BlockSpec((8, d), lambda j: (0, 0)),
                  pl.BlockSpec((d, tn), lambda j: (0, j)),
                  pl.BlockSpec((1, tn), lambda j: (0, j))],
        out_specs=pl.BlockSpec((8, tn), lambda j: (0, j)),
        out_shape=jax.ShapeDtypeStruct((8, n), F32),
        compiler_params=_params(1, 40),
        name="ada",
    )(c8, w, b.reshape(1, n))
    return out[0].reshape(6, d)


def _in_kernel(x_ref, gain_ref, mod_ref, w_ref, o_ref, hn_ref):
    tm = x_ref.shape[0]
    rc = 256

    @pl.when(pl.program_id(1) == 0)
    def _():
        scale = gain_ref[...] * (1.0 + mod_ref[1:2, :])
        shift = mod_ref[0:1, :]

        def body(r, carry):
            rows = pl.ds(pl.multiple_of(r * rc, rc), rc)
            x = x_ref[rows, :]
            ms = jnp.mean(x * x, axis=-1, keepdims=True)
            hn_ref[rows, :] = (x * lax.rsqrt(ms + EPS) * scale + shift).astype(BF16)
            return carry

        lax.fori_loop(0, tm // rc, body, 0)

    o_ref[...] = _dot(hn_ref[...], w_ref[...]).astype(BF16)


def _in_proj(x2d, gain, mod, w):
    s, d = x2d.shape
    n = w.shape[1]
    tm, tn = 1024, 1024
    return pl.pallas_call(
        _in_kernel,
        grid=(s // tm, n // tn),
        in_specs=[pl.BlockSpec((tm, d), lambda i, j: (i, 0)),
                  pl.BlockSpec((1, d), lambda i, j: (0, 0)),
                  pl.BlockSpec((6, d), lambda i, j: (0, 0)),
                  pl.BlockSpec((d, tn), lambda i, j: (0, j))],
        out_specs=pl.BlockSpec((tm, tn), lambda i, j: (i, j)),
        out_shape=jax.ShapeDtypeStruct((s, n), BF16),
        scratch_shapes=[pltpu.VMEM((tm, d), BF16)],
        compiler_params=_params(2, 48),
        name="in_proj",
    )(x2d, gain.reshape(1, d), mod, w.astype(BF16))


def _rope_kernel(pos_ref, inv_ref, cos_ref, sin_ref):
    ang = pos_ref[...].astype(F32) * inv_ref[...]
    cos_ref[...] = jnp.cos(ang)
    sin_ref[...] = jnp.sin(ang)


def _rope_tables(positions):
    s = positions.shape[0]
    half = RET_DK // 2
    inv_freq = (ROPE_BASE ** (-jnp.arange(0, RET_DK, 2, dtype=F32) / RET_DK)).reshape(1, half)
    tm = 1024
    return pl.pallas_call(
        _rope_kernel,
        grid=(s // tm,),
        in_specs=[pl.BlockSpec((tm, 1), lambda i: (i, 0)),
                  pl.BlockSpec((1, half), lambda i: (0, 0))],
        out_specs=[pl.BlockSpec((tm, half), lambda i: (i, 0)),
                   pl.BlockSpec((tm, half), lambda i: (i, 0))],
        out_shape=[jax.ShapeDtypeStruct((s, half), F32), jax.ShapeDtypeStruct((s, half), F32)],
        compiler_params=_params(1, 32),
        name="rope",
    )(positions.reshape(s, 1), inv_freq)


POOL_HALO = 16


def _pool_kernel(u_ref, halo_ref, pw_ref, ps_ref, o_ref):
    i = pl.program_id(0)
    g = pl.program_id(1)
    tm = u_ref.shape[0]
    w = jnp.left_shift(2, g)
    r = lax.broadcasted_iota(jnp.int32, (tm, tm), 0)
    c = lax.broadcasted_iota(jnp.int32, (tm, tm), 1)
    band = jnp.where(c <= r, jnp.where(c > r - w, 1.0, 0.0), 0.0).astype(BF16)
    rh = lax.broadcasted_iota(jnp.int32, (tm, POOL_HALO), 0)
    ch = lax.broadcasted_iota(jnp.int32, (tm, POOL_HALO), 1)
    halo_on = jnp.where(i > 0, 1.0, 0.0)
    bandh = (jnp.where(ch > rh + POOL_HALO - w, 1.0, 0.0) * halo_on).astype(BF16)
    u = u_ref[...]
    wsum = _dot(band, u) + _dot(bandh, halo_ref[...])
    t = i * tm + lax.broadcasted_iota(jnp.int32, (tm, 1), 0)
    cnt = jnp.minimum(t + 1, w).astype(F32)
    pooled = wsum / cnt - u.astype(F32)
    o_ref[...] = (_dot(pooled.astype(BF16), pw_ref[...]) * ps_ref[...]).astype(BF16)


def _pool(proj, pool_w_bf16, pool_scale):
    s = proj.shape[0]
    tm = 512
    hb = tm // POOL_HALO
    return pl.pallas_call(
        _pool_kernel,
        grid=(s // tm, POOL_GROUPS),
        in_specs=[pl.BlockSpec((tm, POOL_GDIM), lambda i, g: (i, g)),
                  pl.BlockSpec((POOL_HALO, POOL_GDIM), lambda i, g: (jnp.maximum(i * hb - 1, 0), g)),
                  pl.BlockSpec((None, POOL_GDIM, POOL_GDIM), lambda i, g: (g, 0, 0)),
                  pl.BlockSpec((1, POOL_GDIM), lambda i, g: (0, g))],
        out_specs=pl.BlockSpec((tm, POOL_GDIM), lambda i, g: (i, g)),
        out_shape=jax.ShapeDtypeStruct((s, POOL_GROUPS * POOL_GDIM), BF16),
        compiler_params=_params(2, 32),
        name="pool",
    )(proj, proj, pool_w_bf16, pool_scale.reshape(1, -1))


RET_HPS = 2


def _ret_kernel(cdec_ref, q_ref, k_ref, v_ref, g_ref, cos_ref, sin_ref, dec_ref, qd_ref, kd_ref,
                gain_ref, o_ref, state_ref):
    hp = pl.program_id(0)

    @pl.when(pl.program_id(1) == 0)
    def _():
        state_ref[...] = jnp.zeros_like(state_ref)

    cos = cos_ref[...]
    sin = sin_ref[...]
    half = RET_DK // 2

    def rot(t_ref, col0):
        t1 = t_ref[:, col0:col0 + half].astype(F32)
        t2 = t_ref[:, col0 + half:col0 + RET_DK].astype(F32)
        return t1 * cos - t2 * sin, t2 * cos + t1 * sin

    for j in range(RET_HPS):
        q1, q2 = rot(q_ref, j * RET_DK)
        k1, k2 = rot(k_ref, j * RET_DK)
        qd = qd_ref[j]
        kd = kd_ref[j]
        qb = jnp.concatenate([q1, q2], axis=1).astype(BF16)
        kb = jnp.concatenate([k1, k2], axis=1).astype(BF16)
        qdb = jnp.concatenate([q1 * qd, q2 * qd], axis=1).astype(BF16)
        kdb = jnp.concatenate([k1 * kd, k2 * kd], axis=1).astype(BF16)
        vcols = slice(j * RET_DV, (j + 1) * RET_DV)
        v = v_ref[:, vcols]
        scores = lax.dot_general(qb, kb, (((1,), (1,)), ((), ())), preferred_element_type=F32) * dec_ref[j]
        inner = _dot(scores.astype(BF16), v)
        state = state_ref[j]
        cross = _dot(qdb, state.astype(BF16))
        state_ref[j] = state * cdec_ref[hp * RET_HPS + j] + lax.dot_general(
            kdb, v, (((0,), (0,)), ((), ())), preferred_element_type=F32)
        y = inner + cross
        mu = jnp.mean(y, axis=-1, keepdims=True)
        yc = y - mu
        var = jnp.mean(yc * yc, axis=-1, keepdims=True)
        yn = yc * lax.rsqrt(var + EPS) * gain_ref[:, vcols]
        gate = g_ref[:, vcols].astype(F32)
        o_ref[:, vcols] = (gate * _sigmoid(gate) * yn).astype(BF16)


def _ret_tables():
    c = RET_CHUNK
    log_g = jnp.log(1.0 - 2.0 ** (-5.0 - jnp.arange(RET_HEADS, dtype=F32)))
    idx = jnp.arange(c, dtype=F32)
    diff = idx[:, None] - idx[None, :]
    kscale = RET_DK ** -0.5
    dec = jnp.where(diff[None] >= 0, jnp.exp(jnp.maximum(diff, 0.0)[None] * log_g[:, None, None]), 0.0) * kscale
    qd = jnp.exp((idx + 1.0)[None, :] * log_g[:, None])
    kd = jnp.exp((c - 1.0 - idx)[None, :] * log_g[:, None]) * kscale
    half = RET_DK // 2
    qd = jnp.broadcast_to(qd[:, :, None], (RET_HEADS, c, half))
    kd = jnp.broadcast_to(kd[:, :, None], (RET_HEADS, c, half))
    cdec = jnp.exp(c * log_g)
    return cdec, dec, qd, kd


def _retention(proj, cos, sin, ret_gain):
    s = proj.shape[0]
    c = RET_CHUNK
    half = RET_DK // 2
    cdec, dec, qd, kd = _ret_tables()
    wqk, wv = RET_HPS * RET_DK, RET_HPS * RET_DV
    qb0, kb0 = COL_Q // wqk, COL_K // wqk
    vb0, gb0 = COL_V // wv, COL_G // wv
    return pl.pallas_call(
        _ret_kernel,
        grid=(RET_HEADS // RET_HPS, s // c),
        in_specs=[pl.BlockSpec(memory_space=pltpu.SMEM),
                  pl.BlockSpec((c, wqk), lambda h, i: (i, qb0 + h)),
                  pl.BlockSpec((c, wqk), lambda h, i: (i, kb0 + h)),
                  pl.BlockSpec((c, wv), lambda h, i: (i, vb0 + h)),
                  pl.BlockSpec((c, wv), lambda h, i: (i, gb0 + h)),
                  pl.BlockSpec((c, half), lambda h, i: (i, 0)),
                  pl.BlockSpec((c, half), lambda h, i: (i, 0)),
                  pl.BlockSpec((RET_HPS, c, c), lambda h, i: (h, 0, 0)),
                  pl.BlockSpec((RET_HPS, c, half), lambda h, i: (h, 0, 0)),
                  pl.BlockSpec((RET_HPS, c, half), lambda h, i: (h, 0, 0)),
                  pl.BlockSpec((1, wv), lambda h, i: (0, h))],
        out_specs=pl.BlockSpec((c, wv), lambda h, i: (i, h)),
        out_shape=jax.ShapeDtypeStruct((s, RET_HEADS * RET_DV), BF16),
        scratch_shapes=[pltpu.VMEM((RET_HPS, RET_DK, RET_DV), F32)],
        compiler_params=_params(2, 32),
        name="retention",
    )(cdec, proj, proj, proj, proj, cos, sin, dec, qd, kd, ret_gain.reshape(1, -1))


def _branch_kernel(p_ref, r_ref, ap_ref, ar_ref, wp_ref, wr_ref, o_ref):
    bp = _dot(p_ref[...], wp_ref[...])
    br = _dot(r_ref[...], wr_ref[...])
    ap = ap_ref[...].astype(F32)
    ar = ar_ref[...].astype(F32)
    o_ref[...] = (_sigmoid(ap) * bp + _sigmoid(ar) * br).astype(BF16)


def _branch(pool_out, ret_out, proj, wp, wr):
    s = pool_out.shape[0]
    d = wp.shape[1]
    tm, tn = 512, 512
    ap0, ar0 = COL_APOOL // tn, COL_ARET // tn
    return pl.pallas_call(
        _branch_kernel,
        grid=(s // tm, d // tn),
        in_specs=[pl.BlockSpec((tm, pool_out.shape[1]), lambda i, j: (i, 0)),
                  pl.BlockSpec((tm, ret_out.shape[1]), lambda i, j: (i, 0)),
                  pl.BlockSpec((tm, tn), lambda i, j: (i, ap0 + j)),
                  pl.BlockSpec((tm, tn), lambda i, j: (i, ar0 + j)),
                  pl.BlockSpec((wp.shape[0], tn), lambda i, j: (0, j)),
                  pl.BlockSpec((wr.shape[0], tn), lambda i, j: (0, j))],
        out_specs=pl.BlockSpec((tm, tn), lambda i, j: (i, j)),
        out_shape=jax.ShapeDtypeStruct((s, d), BF16),
        compiler_params=_params(2, 40),
        name="branch",
    )(pool_out, ret_out, proj, proj, wp.astype(BF16), wr.astype(BF16))


def _out_kernel(m_ref, w_ref, x_ref, mod_ref, gain_ref, x1_ref, fnt_ref):
    x1 = x_ref[...] + mod_ref[2:3, :] * _dot(m_ref[...], w_ref[...])
    x1_ref[...] = x1
    ms = jnp.mean(x1 * x1, axis=-1, keepdims=True)
    fn = x1 * lax.rsqrt(ms + EPS) * (gain_ref[...] * (1.0 + mod_ref[4:5, :])) + mod_ref[3:4, :]
    fnt_ref[...] = fn.T.astype(BF16)


def _out_proj(merged, w_bf16, x2d, mod, gain):
    s, d = x2d.shape
    tm = 256
    return pl.pallas_call(
        _out_kernel,
        grid=(s // tm,),
        in_specs=[pl.BlockSpec((tm, d), lambda i: (i, 0)),
                  pl.BlockSpec((d, d), lambda i: (0, 0)),
                  pl.BlockSpec((tm, d), lambda i: (i, 0)),
                  pl.BlockSpec((6, d), lambda i: (0, 0)),
                  pl.BlockSpec((1, d), lambda i: (0, 0))],
        out_specs=[pl.BlockSpec((tm, d), lambda i: (i, 0)),
                   pl.BlockSpec((d, tm), lambda i: (0, i))],
        out_shape=[jax.ShapeDtypeStruct((s, d), F32), jax.ShapeDtypeStruct((d, s), BF16)],
        compiler_params=_params(1, 48),
        name="out_proj",
    )(merged, w_bf16, x2d, mod, gain.reshape(1, d))


def _pscore_kernel(fnt_ref, wqt_ref, keys_ref, s_ref):
    t = fnt_ref.shape[1]
    qt = _dot(wqt_ref[...], fnt_ref[...]).astype(BF16)
    for hp in range(2 * PEER_HEADS):
        sc = _dot(keys_ref[hp], qt[hp * PEER_HALF:(hp + 1) * PEER_HALF, :])
        for lc in range(t // LANES):
            s_ref[hp, lc] = sc[:, lc * LANES:(lc + 1) * LANES]


def _peer_scores(fnt, wqt_bf16, keys_bf16):
    d, s = fnt.shape
    t = 512
    nq = wqt_bf16.shape[0]
    return pl.pallas_call(
        _pscore_kernel,
        grid=(s // t,),
        in_specs=[pl.BlockSpec((d, t), lambda i: (0, i)),
                  pl.BlockSpec((nq, d), lambda i: (0, 0)),
                  pl.BlockSpec((2 * PEER_HEADS, PEER_NKEYS, PEER_HALF), lambda i: (0, 0, 0))],
        out_specs=pl.BlockSpec((2 * PEER_HEADS, t // LANES, PEER_NKEYS, LANES), lambda i: (0, i, 0, 0)),
        out_shape=jax.ShapeDtypeStruct((2 * PEER_HEADS, s // LANES, PEER_NKEYS, LANES), F32),
        compiler_params=_params(1, 48),
        name="peer_scores",
    )(fnt, wqt_bf16, keys_bf16)


SUBLANES = 8
LOG2E = 1.4426950408889634


def _sort_network(n):
    size = 1
    while size < n:
        size *= 2

    def merge(lo, hi, r):
        step = r * 2
        if step < hi - lo:
            yield from merge(lo, hi, step)
            yield from merge(lo + r, hi, step)
            for i in range(lo + r, hi - r, step):
                yield (i, i + r)
        else:
            yield (lo, lo + r)

    def sort(lo, hi):
        if hi - lo >= 1:
            mid = lo + (hi - lo) // 2
            yield from sort(lo, mid)
            yield from sort(mid + 1, hi)
            yield from merge(lo, hi, 1)

    return [(i, j) for i, j in sort(0, size - 1) if j < n]


def _top_values(slabs, k):
    cols = list(slabs)
    for i, j in _sort_network(len(cols)):
        hi = jnp.maximum(cols[i], cols[j])
        cols[j] = jnp.minimum(cols[i], cols[j])
        cols[i] = hi
    vals = []
    for r in range(k):
        m = jnp.max(cols[0], axis=0, keepdims=True)
        vals.append(m)
        depth = min(len(cols), k - r)
        if r == k - 1:
            break
        hit = cols[0] == m
        for q in range(depth - 1):
            cols[q] = jnp.where(hit, cols[q + 1], cols[q])
        if depth == len(cols):
            cols[depth - 1] = jnp.where(hit, -jnp.inf, cols[depth - 1])
    return vals


def _rows_to_slabs(rows, n_slabs, row_id):
    slabs = []
    for g in range(n_slabs):
        slab = jnp.full(row_id.shape, -jnp.inf, F32)
        for q in range(SUBLANES):
            r = g * SUBLANES + q
            if r < len(rows):
                slab = jnp.where(row_id == q, rows[r], slab)
        slabs.append(slab)
    return slabs


PEER_ETILE = 512
PEER_IB = PEER_ETILE // PEER_NKEYS


PEER_NTOP = PEER_TOPK + 1


def _ptopk_kernel(s_ref, alpha_ref, beta_ref, tau_ref):
    n_lc = s_ref.shape[1]
    n_slabs = PEER_NKEYS // SUBLANES
    row_id = lax.broadcasted_iota(jnp.int32, (SUBLANES, LANES), 0)

    def body(lc, carry):
        sl = pl.ds(pl.multiple_of(lc * LANES, LANES), LANES)
        s1 = s_ref[0, lc]
        s2 = s_ref[1, lc]
        v1 = _top_values([s1[g * SUBLANES:(g + 1) * SUBLANES, :] for g in range(n_slabs)], PEER_NTOP)
        v2 = _top_values([s2[g * SUBLANES:(g + 1) * SUBLANES, :] for g in range(n_slabs)], PEER_NTOP)
        n_vs = -(-PEER_NTOP // SUBLANES)
        v1s = _rows_to_slabs(v1, n_vs, row_id)
        v2s = _rows_to_slabs(v2, n_vs, row_id)
        cand = [v1[0] + slab for slab in v2s]
        for r1 in range(1, SUBLANES):
            n_ok = PEER_NTOP // (r1 + 1)
            pair = v1[r1] + v2s[0]
            cand.append(pair if n_ok >= SUBLANES else jnp.where(row_id < n_ok, pair, -jnp.inf))
        for slab in v1s[1:]:
            cand.append(slab + v2[0])
        best = _top_values(cand, PEER_NTOP)
        z = jnp.ones_like(best[0])
        for r in range(1, PEER_TOPK):
            z = z + jnp.exp(best[r] - best[0])
        log_norm = best[0] + jnp.log(z)
        alpha = (s1 - log_norm) * LOG2E
        for g in range(PEER_NKEYS // PEER_IB):
            alpha_ref[g, :, sl] = alpha[g * PEER_IB:(g + 1) * PEER_IB, :]
        beta_ref[lc] = s2 * LOG2E
        tau_ref[:, sl] = (0.5 * (best[PEER_TOPK - 1] + best[PEER_TOPK]) - log_norm) * LOG2E
        return carry

    lax.fori_loop(0, n_lc, body, 0)


def _peer_topk(scores):
    hp, n_chunks, nk, _ = scores.shape
    s = n_chunks * LANES
    t = 512
    ng = nk // PEER_IB
    return pl.pallas_call(
        _ptopk_kernel,
        grid=(s // t, PEER_HEADS),
        in_specs=[pl.BlockSpec((2, t // LANES, nk, LANES), lambda i, h: (h, i, 0, 0))],
        out_specs=[pl.BlockSpec((None, ng, PEER_IB, t), lambda i, h: (h, 0, 0, i)),
                   pl.BlockSpec((None, t // LANES, nk, LANES), lambda i, h: (h, i, 0, 0)),
                   pl.BlockSpec((None, 1, t), lambda i, h: (h, 0, i))],
        out_shape=[jax.ShapeDtypeStruct((PEER_HEADS, ng, PEER_IB, s), F32),
                   jax.ShapeDtypeStruct((PEER_HEADS, n_chunks, nk, LANES), F32),
                   jax.ShapeDtypeStruct((PEER_HEADS, 1, s), F32)],
        compiler_params=_params(2, 32),
        name="peer_topk",
    )(scores)


MXU_COLS = 256
Z_ROWS = 128
O_ROWS = 256
GATE_ROWS = 32


def _gelu_tanh(x):
    return 0.5 * x * (1.0 + jnp.tanh(0.7978845608028654 * (x + 0.044715 * (x * x * x))))


def _pdense_step(fnt_ref, u_ref, vt_ref, alpha_ref, beta_ref, tau_ref, o_ref,
                 z_prev_ref, z_next_ref, a_prev_ref, a_next_ref):
    t = fnt_ref.shape[1]
    n_lc = t // LANES
    d_model = vt_ref.shape[0]
    n_exp = u_ref.shape[0]
    mxu_chunks = []
    for c0 in range(0, t, MXU_COLS):
        cols = slice(c0, c0 + MXU_COLS)
        for r0 in range(0, n_exp, Z_ROWS):
            mxu_chunks.append(("z", cols, slice(r0, r0 + Z_ROWS)))
        for r0 in range(0, d_model, O_ROWS):
            mxu_chunks.append(("o", cols, slice(r0, r0 + O_ROWS)))
    n_jb = PEER_NKEYS // GATE_ROWS
    n_blocks = n_lc * n_jb
    place = {}
    for k, chunk in enumerate(mxu_chunks):
        place.setdefault(k * n_blocks // len(mxu_chunks), []).append(chunk)
    for lc in range(n_lc):
        sl = slice(lc * LANES, (lc + 1) * LANES)
        for jb in range(n_jb):
            for kind, cols, mrows in place.get(lc * n_jb + jb, ()):
                slabs = range(cols.start // LANES, cols.stop // LANES)
                if kind == "z":
                    zc = _dot(u_ref[mrows, :], fnt_ref[:, cols])
                    for k, slab in enumerate(slabs):
                        z_next_ref[slab, mrows, :] = zc[:, k * LANES:(k + 1) * LANES]
                else:
                    ac = jnp.concatenate([a_prev_ref[slab] for slab in slabs], axis=1)
                    o_ref[mrows, cols] += _dot(vt_ref[mrows, :], ac)
            jrows = slice(jb * GATE_ROWS, (jb + 1) * GATE_ROWS)
            accs = [jnp.zeros((GATE_ROWS, LANES), F32) for _ in range(PEER_IB)]
            for h in range(PEER_HEADS):
                beta = beta_ref[h, lc, jrows, :]
                tau = tau_ref[h, :, sl]
                for il in range(PEER_IB):
                    lg = alpha_ref[h, il:il + 1, sl] + beta
                    accs[il] = accs[il] + jnp.where(lg >= tau, jnp.exp2(lg), 0.0)
            for il in range(PEER_IB):
                rows = slice(il * PEER_NKEYS + jb * GATE_ROWS, il * PEER_NKEYS + (jb + 1) * GATE_ROWS)
                a_next_ref[lc, rows, :] = (accs[il] * _gelu_tanh(z_prev_ref[lc, rows, :])).astype(BF16)


def _pdense_kernel(fnt_ref, u_ref, vt_ref, alpha_ref, beta_ref, tau_ref, o_ref,
                   z0_ref, z1_ref, a0_ref, a1_ref, *, n_et):
    s = pl.program_id(0)
    ins = (fnt_ref, u_ref, vt_ref, alpha_ref, beta_ref, tau_ref, o_ref)

    @pl.when(s == 0)
    def _():
        z1_ref[...] = jnp.zeros_like(z1_ref)
        a1_ref[...] = jnp.zeros_like(a1_ref)

    @pl.when((s == 0) | ((s + n_et - 2) % n_et == 0))
    def _():
        o_ref[...] = jnp.zeros_like(o_ref)

    @pl.when(s % 2 == 0)
    def _():
        _pdense_step(*ins, z1_ref, z0_ref, a1_ref, a0_ref)

    @pl.when(s % 2 == 1)
    def _():
        _pdense_step(*ins, z0_ref, z1_ref, a0_ref, a1_ref)


def _peer_dense(fnt, u_bf16, vt_bf16, alpha, beta, tau):
    d, s = fnt.shape
    ne = u_bf16.shape[0]
    t = 512
    et = PEER_ETILE
    n_et = ne // et
    assert n_et % 2 == 0
    nk = beta.shape[2]
    n_lc = t // LANES
    n_pairs = (s // t) * n_et

    def pair(step, lag):
        p = jnp.clip(step - lag, 0, n_pairs - 1)
        return p // n_et, p % n_et

    return pl.pallas_call(
        functools.partial(_pdense_kernel, n_et=n_et),
        grid=(n_pairs + 2,),
        in_specs=[pl.BlockSpec((d, t), lambda p: (0, pair(p, 0)[0])),
                  pl.BlockSpec((et, d), lambda p: (pair(p, 0)[1], 0)),
                  pl.BlockSpec((None, d, et), lambda p: (pair(p, 2)[1], 0, 0)),
                  pl.BlockSpec((PEER_HEADS, None, PEER_IB, t), lambda p: (0, pair(p, 1)[1], 0, pair(p, 1)[0])),
                  pl.BlockSpec((PEER_HEADS, n_lc, nk, LANES), lambda p: (0, pair(p, 1)[0], 0, 0)),
                  pl.BlockSpec((PEER_HEADS, 1, t), lambda p: (0, 0, pair(p, 1)[0]))],
        out_specs=pl.BlockSpec((d, t), lambda p: (0, pair(p, 2)[0])),
        out_shape=jax.ShapeDtypeStruct((d, s), F32),
        scratch_shapes=[pltpu.VMEM((n_lc, et, LANES), F32), pltpu.VMEM((n_lc, et, LANES), F32),
                        pltpu.VMEM((n_lc, et, LANES), BF16), pltpu.VMEM((n_lc, et, LANES), BF16)],
        compiler_params=_params(1, 48),
        name="peer_dense",
    )(fnt, u_bf16, vt_bf16, alpha, beta, tau)


def _final_kernel(x1_ref, ot_ref, mod_ref, gain_ref, y_ref, *, apply_norm):
    x2 = x1_ref[...] + mod_ref[5:6, :] * ot_ref[...].T
    if apply_norm:
        ms = jnp.mean(x2 * x2, axis=-1, keepdims=True)
        x2 = x2 * lax.rsqrt(ms + EPS) * gain_ref[...]
    y_ref[...] = x2


def _final(x1, out_t, mod, gain, apply_norm):
    s, d = x1.shape
    tm = 512
    return pl.pallas_call(
        functools.partial(_final_kernel, apply_norm=apply_norm),
        grid=(s // tm,),
        in_specs=[pl.BlockSpec((tm, d), lambda i: (i, 0)),
                  pl.BlockSpec((d, tm), lambda i: (0, i)),
                  pl.BlockSpec((6, d), lambda i: (0, 0)),
                  pl.BlockSpec((1, d), lambda i: (0, 0))],
        out_specs=pl.BlockSpec((tm, d), lambda i: (i, 0)),
        out_shape=jax.ShapeDtypeStruct((s, d), F32),
        compiler_params=_params(1, 48),
        name="final",
    )(x1, out_t, mod, gain.reshape(1, d))


def kernel(x, c, positions, norm_mix_gain, w_ada, b_ada, w_in, pool_w, pool_scale, ret_norm_gain,
           w_branch_pool, w_branch_ret, w_out, norm_ffn_gain, peer_w_query, peer_sub_keys, peer_u, peer_v,
           final_norm_gain):
    batch, s, d = x.shape
    assert batch == 1 and d == D_MODEL and s % 1024 == 0
    depth = w_in.shape[0]
    xs = x.reshape(s, d)
    cos, sin = _rope_tables(positions.reshape(s))
    for l in range(depth):
        mod = _ada(c, w_ada[l], b_ada[l])
        proj = _in_proj(xs, norm_mix_gain[l], mod, w_in[l])
        pool_out = _pool(proj, pool_w[l].astype(BF16), pool_scale[l])
        ret_out = _retention(proj, cos, sin, ret_norm_gain[l])
        merged = _branch(pool_out, ret_out, proj, w_branch_pool[l], w_branch_ret[l])
        x1, fnt = _out_proj(merged, w_out[l].astype(BF16), xs, mod, norm_ffn_gain[l])
        keys = peer_sub_keys[l].reshape(2 * PEER_HEADS, PEER_NKEYS, PEER_HALF).astype(BF16)
        scores = _peer_scores(fnt, peer_w_query[l].T.astype(BF16), keys)
        alpha, beta, tau = _peer_topk(scores)
        vt_tiles = peer_v[l].astype(BF16).reshape(N_EXPERTS // PEER_ETILE, PEER_ETILE, d).transpose(0, 2, 1)
        out_t = _peer_dense(fnt, peer_u[l].astype(BF16), vt_tiles, alpha, beta, tau)
        xs = _final(x1, out_t, mod, final_norm_gain, apply_norm=(l == depth - 1))
    return xs.reshape(batch, s, d)
```

```python
import functools

import jax
import jax.numpy as jnp
from jax import lax
from jax.experimental import pallas as pl
from jax.experimental.pallas import tpu as pltpu

F32 = jnp.float32
BF16 = jnp.bfloat16

D_MODEL = 2048
EPS = 1e-6
POOL_GROUPS = 4
POOL_GDIM = 512
RET_HEADS = 8
RET_DK = 256
RET_DV = 512
ROPE_BASE = 10000.0
PEER_HEADS = 8
PEER_NKEYS = 128
PEER_HALF = 128
PEER_TOPK = 16
N_EXPERTS = PEER_NKEYS * PEER_NKEYS

COL_POOL, COL_Q, COL_K, COL_V, COL_G, COL_APOOL, COL_ARET = 0, 2048, 4096, 6144, 10240, 14336, 16384
IN_COLS = 18432

LANES = 128
MIB = 1024 * 1024

RET_CHUNK = 256


def _params(n_axes, vmem_mib, flags=None):
    return pltpu.CompilerParams(dimension_semantics=("arbitrary",) * n_axes,
                                vmem_limit_bytes=vmem_mib * MIB, flags=flags)


def _dot(a, b):
    return jnp.dot(a, b, preferred_element_type=F32)


def _sigmoid(x):
    return 1.0 / (1.0 + jnp.exp(-x))


def _ada_kernel(c_ref, w_ref, b_ref, o_ref):
    c = c_ref[...]
    cond = c * _sigmoid(c)
    o_ref[...] = jnp.dot(cond, w_ref[...], preferred_element_type=F32,
                         precision=lax.Precision.HIGHEST) + b_ref[...]


def _ada(c, w, b):
    d, n = w.shape
    tn = 1024
    c8 = jnp.broadcast_to(c, (8, d))
    out = pl.pallas_call(
        _ada_kernel,
        grid=(n // tn,),
        in_specs=[pl.BlockSpec((8, d), lambda j: (0, 0)),
                  pl.BlockSpec((d, tn), lambda j: (0, j)),
                  pl.BlockSpec((1, tn), lambda j: (0, j))],
        out_specs=pl.BlockSpec((8, tn), lambda j: (0, j)),
        out_shape=jax.ShapeDtypeStruct((8, n), F32),
        compiler_params=_params(1, 40),
        name="ada",
    )(c8, w, b.reshape(1, n))
    return out[0].reshape(6, d)


def _in_kernel(x_ref, gain_ref, mod_ref, w_ref, o_ref, hn_ref):
    tm = x_ref.shape[0]
    rc = 256

    @pl.when(pl.program_id(1) == 0)
    def _():
        scale = gain_ref[...] * (1.0 + mod_ref[1:2, :])
        shift = mod_ref[0:1, :]

        def body(r, carry):
            rows = pl.ds(pl.multiple_of(r * rc, rc), rc)
            x = x_ref[rows, :]
            ms = jnp.mean(x * x, axis=-1, keepdims=True)
            hn_ref[rows, :] = (x * lax.rsqrt(ms + EPS) * scale + shift).astype(BF16)
            return carry

        lax.fori_loop(0, tm // rc, body, 0)

    o_ref[...] = _dot(hn_ref[...], w_ref[...]).astype(BF16)


def _in_proj(x2d, gain, mod, w):
    s, d = x2d.shape
    n = w.shape[1]
    tm, tn = 1024, 1024
    return pl.pallas_call(
        _in_kernel,
        grid=(s // tm, n // tn),
        in_specs=[pl.BlockSpec((tm, d), lambda i, j: (i, 0)),
                  pl.BlockSpec((1, d), lambda i, j: (0, 0)),
                  pl.BlockSpec((6, d), lambda i, j: (0, 0)),
                  pl.BlockSpec((d, tn), lambda i, j: (0, j))],
        out_specs=pl.BlockSpec((tm, tn), lambda i, j: (i, j)),
        out_shape=jax.ShapeDtypeStruct((s, n), BF16),
        scratch_shapes=[pltpu.VMEM((tm, d), BF16)],
        compiler_params=_params(2, 48),
        name="in_proj",
    )(x2d, gain.reshape(1, d), mod, w.astype(BF16))


def _rope_kernel(pos_ref, inv_ref, cos_ref, sin_ref):
    ang = pos_ref[...].astype(F32) * inv_ref[...]
    cos_ref[...] = jnp.cos(ang)
    sin_ref[...] = jnp.sin(ang)


def _rope_tables(positions):
    s = positions.shape[0]
    half = RET_DK // 2
    inv_freq = (ROPE_BASE ** (-jnp.arange(0, RET_DK, 2, dtype=F32) / RET_DK)).reshape(1, half)
    tm = 1024
    return pl.pallas_call(
        _rope_kernel,
        grid=(s // tm,),
        in_specs=[pl.BlockSpec((tm, 1), lambda i: (i, 0)),
                  pl.BlockSpec((1, half), lambda i: (0, 0))],
        out_specs=[pl.BlockSpec((tm, half), lambda i: (i, 0)),
                   pl.BlockSpec((tm, half), lambda i: (i, 0))],
        out_shape=[jax.ShapeDtypeStruct((s, half), F32), jax.ShapeDtypeStruct((s, half), F32)],
        compiler_params=_params(1, 32),
        name="rope",
    )(positions.reshape(s, 1), inv_freq)


POOL_HALO = 16


def _pool_kernel(u_ref, halo_ref, pw_ref, ps_ref, o_ref):
    i = pl.program_id(0)
    g = pl.program_id(1)
    tm = u_ref.shape[0]
    w = jnp.left_shift(2, g)
    r = lax.broadcasted_iota(jnp.int32, (tm, tm), 0)
    c = lax.broadcasted_iota(jnp.int32, (tm, tm), 1)
    band = jnp.where(c <= r, jnp.where(c > r - w, 1.0, 0.0), 0.0).astype(BF16)
    rh = lax.broadcasted_iota(jnp.int32, (tm, POOL_HALO), 0)
    ch = lax.broadcasted_iota(jnp.int32, (tm, POOL_HALO), 1)
    halo_on = jnp.where(i > 0, 1.0, 0.0)
    bandh = (jnp.where(ch > rh + POOL_HALO - w, 1.0, 0.0) * halo_on).astype(BF16)
    u = u_ref[...]
    wsum = _dot(band, u) + _dot(bandh, halo_ref[...])
    t = i * tm + lax.broadcasted_iota(jnp.int32, (tm, 1), 0)
    cnt = jnp.minimum(t + 1, w).astype(F32)
    pooled = wsum / cnt - u.astype(F32)
    o_ref[...] = (_dot(pooled.astype(BF16), pw_ref[...]) * ps_ref[...]).astype(BF16)


def _pool(proj, pool_w_bf16, pool_scale):
    s = proj.shape[0]
    tm = 512
    hb = tm // POOL_HALO
    return pl.pallas_call(
        _pool_kernel,
        grid=(s // tm, POOL_GROUPS),
        in_specs=[pl.BlockSpec((tm, POOL_GDIM), lambda i, g: (i, g)),
                  pl.BlockSpec((POOL_HALO, POOL_GDIM), lambda i, g: (jnp.maximum(i * hb - 1, 0), g)),
                  pl.BlockSpec((None, POOL_GDIM, POOL_GDIM), lambda i, g: (g, 0, 0)),
                  pl.BlockSpec((1, POOL_GDIM), lambda i, g: (0, g))],
        out_specs=pl.BlockSpec((tm, POOL_GDIM), lambda i, g: (i, g)),
        out_shape=jax.ShapeDtypeStruct((s, POOL_GROUPS * POOL_GDIM), BF16),
        compiler_params=_params(2, 32),
        name="pool",
    )(proj, proj, pool_w_bf16, pool_scale.reshape(1, -1))


RET_HPS = 2


def _ret_kernel(cdec_ref, q_ref, k_ref, v_ref, g_ref, cos_ref, sin_ref, dec_ref, qd_ref, kd_ref,
                gain_ref, o_ref, state_ref):
    hp = pl.program_id(0)

    @pl.when(pl.program_id(1) == 0)
    def _():
        state_ref[...] = jnp.zeros_like(state_ref)

    cos = cos_ref[...]
    sin = sin_ref[...]
    half = RET_DK // 2

    def rot(t_ref, col0):
        t1 = t_ref[:, col0:col0 + half].astype(F32)
        t2 = t_ref[:, col0 + half:col0 + RET_DK].astype(F32)
        return t1 * cos - t2 * sin, t2 * cos + t1 * sin

    for j in range(RET_HPS):
        q1, q2 = rot(q_ref, j * RET_DK)
        k1, k2 = rot(k_ref, j * RET_DK)
        qd = qd_ref[j]
        kd = kd_ref[j]
        qb = jnp.concatenate([q1, q2], axis=1).astype(BF16)
        kb = jnp.concatenate([k1, k2], axis=1).astype(BF16)
        qdb = jnp.concatenate([q1 * qd, q2 * qd], axis=1).astype(BF16)
        kdb = jnp.concatenate([k1 * kd, k2 * kd], axis=1).astype(BF16)
        vcols = slice(j * RET_DV, (j + 1) * RET_DV)
        v = v_ref[:, vcols]
        scores = lax.dot_general(qb, kb, (((1,), (1,)), ((), ())), preferred_element_type=F32) * dec_ref[j]
        inner = _dot(scores.astype(BF16), v)
        state = state_ref[j]
        cross = _dot(qdb, state.astype(BF16))
        state_ref[j] = state * cdec_ref[hp * RET_HPS + j] + lax.dot_general(
            kdb, v, (((0,), (0,)), ((), ())), preferred_element_type=F32)
        y = inner + cross
        mu = jnp.mean(y, axis=-1, keepdims=True)
        yc = y - mu
        var = jnp.mean(yc * yc, axis=-1, keepdims=True)
        yn = yc * lax.rsqrt(var + EPS) * gain_ref[:, vcols]
        gate = g_ref[:, vcols].astype(F32)
        o_ref[:, vcols] = (gate * _sigmoid(gate) * yn).astype(BF16)


def _ret_tables():
    c = RET_CHUNK
    log_g = jnp.log(1.0 - 2.0 ** (-5.0 - jnp.arange(RET_HEADS, dtype=F32)))
    idx = jnp.arange(c, dtype=F32)
    diff = idx[:, None] - idx[None, :]
    kscale = RET_DK ** -0.5
    dec = jnp.where(diff[None] >= 0, jnp.exp(jnp.maximum(diff, 0.0)[None] * log_g[:, None, None]), 0.0) * kscale
    qd = jnp.exp((idx + 1.0)[None, :] * log_g[:, None])
    kd = jnp.exp((c - 1.0 - idx)[None, :] * log_g[:, None]) * kscale
    half = RET_DK // 2
    qd = jnp.broadcast_to(qd[:, :, None], (RET_HEADS, c, half))
    kd = jnp.broadcast_to(kd[:, :, None], (RET_HEADS, c, half))
    cdec = jnp.exp(c * log_g)
    return cdec, dec, qd, kd


def _retention(proj, cos, sin, ret_gain):
    s = proj.shape[0]
    c = RET_CHUNK
    half = RET_DK // 2
    cdec, dec, qd, kd = _ret_tables()
    wqk, wv = RET_HPS * RET_DK, RET_HPS * RET_DV
    qb0, kb0 = COL_Q // wqk, COL_K // wqk
    vb0, gb0 = COL_V // wv, COL_G // wv
    return pl.pallas_call(
        _ret_kernel,
        grid=(RET_HEADS // RET_HPS, s // c),
        in_specs=[pl.BlockSpec(memory_space=pltpu.SMEM),
                  pl.BlockSpec((c, wqk), lambda h, i: (i, qb0 + h)),
                  pl.BlockSpec((c, wqk), lambda h, i: (i, kb0 + h)),
                  pl.BlockSpec((c, wv), lambda h, i: (i, vb0 + h)),
                  pl.BlockSpec((c, wv), lambda h, i: (i, gb0 + h)),
                  pl.BlockSpec((c, half), lambda h, i: (i, 0)),
                  pl.BlockSpec((c, half), lambda h, i: (i, 0)),
                  pl.BlockSpec((RET_HPS, c, c), lambda h, i: (h, 0, 0)),
                  pl.BlockSpec((RET_HPS, c, half), lambda h, i: (h, 0, 0)),
                  pl.BlockSpec((RET_HPS, c, half), lambda h, i: (h, 0, 0)),
                  pl.BlockSpec((1, wv), lambda h, i: (0, h))],
        out_specs=pl.BlockSpec((c, wv), lambda h, i: (i, h)),
        out_shape=jax.ShapeDtypeStruct((s, RET_HEADS * RET_DV), BF16),
        scratch_shapes=[pltpu.VMEM((RET_HPS, RET_DK, RET_DV), F32)],
        compiler_params=_params(2, 32),
        name="retention",
    )(cdec, proj, proj, proj, proj, cos, sin, dec, qd, kd, ret_gain.reshape(1, -1))


def _branch_kernel(p_ref, r_ref, ap_ref, ar_ref, wp_ref, wr_ref, o_ref):
    bp = _dot(p_ref[...], wp_ref[...])
    br = _dot(r_ref[...], wr_ref[...])
    ap = ap_ref[...].astype(F32)
    ar = ar_ref[...].astype(F32)
    o_ref[...] = (_sigmoid(ap) * bp + _sigmoid(ar) * br).astype(BF16)


def _branch(pool_out, ret_out, proj, wp, wr):
    s = pool_out.shape[0]
    d = wp.shape[1]
    tm, tn = 512, 512
    ap0, ar0 = COL_APOOL // tn, COL_ARET // tn
    return pl.pallas_call(
        _branch_kernel,
        grid=(s // tm, d // tn),
        in_specs=[pl.BlockSpec((tm, pool_out.shape[1]), lambda i, j: (i, 0)),
                  pl.BlockSpec((tm, ret_out.shape[1]), lambda i, j: (i, 0)),
                  pl.BlockSpec((tm, tn), lambda i, j: (i, ap0 + j)),
                  pl.BlockSpec((tm, tn), lambda i, j: (i, ar0 + j)),
                  pl.BlockSpec((wp.shape[0], tn), lambda i, j: (0, j)),
                  pl.BlockSpec((wr.shape[0], tn), lambda i, j: (0, j))],
        out_specs=pl.BlockSpec((tm, tn), lambda i, j: (i, j)),
        out_shape=jax.ShapeDtypeStruct((s, d), BF16),
        compiler_params=_params(2, 40),
        name="branch",
    )(pool_out, ret_out, proj, proj, wp.astype(BF16), wr.astype(BF16))


def _out_kernel(m_ref, w_ref, x_ref, mod_ref, gain_ref, x1_ref, fnt_ref):
    x1 = x_ref[...] + mod_ref[2:3, :] * _dot(m_ref[...], w_ref[...])
    x1_ref[...] = x1
    ms = jnp.mean(x1 * x1, axis=-1, keepdims=True)
    fn = x1 * lax.rsqrt(ms + EPS) * (gain_ref[...] * (1.0 + mod_ref[4:5, :])) + mod_ref[3:4, :]
    fnt_ref[...] = fn.T.astype(BF16)


def _out_proj(merged, w_bf16, x2d, mod, gain):
    s, d = x2d.shape
    tm = 256
    return pl.pallas_call(
        _out_kernel,
        grid=(s // tm,),
        in_specs=[pl.BlockSpec((tm, d), lambda i: (i, 0)),
                  pl.BlockSpec((d, d), lambda i: (0, 0)),
                  pl.BlockSpec((tm, d), lambda i: (i, 0)),
                  pl.BlockSpec((6, d), lambda i: (0, 0)),
                  pl.BlockSpec((1, d), lambda i: (0, 0))],
        out_specs=[pl.BlockSpec((tm, d), lambda i: (i, 0)),
                   pl.BlockSpec((d, tm), lambda i: (0, i))],
        out_shape=[jax.ShapeDtypeStruct((s, d), F32), jax.ShapeDtypeStruct((d, s), BF16)],
        compiler_params=_params(1, 48),
        name="out_proj",
    )(merged, w_bf16, x2d, mod, gain.reshape(1, d))


def _pscore_kernel(fnt_ref, wqt_ref, keys_ref, s_ref):
    t = fnt_ref.shape[1]
    qt = _dot(wqt_ref[...], fnt_ref[...]).astype(BF16)
    for hp in range(2 * PEER_HEADS):
        sc = _dot(keys_ref[hp], qt[hp * PEER_HALF:(hp + 1) * PEER_HALF, :])
        for lc in range(t // LANES):
            s_ref[hp, lc] = sc[:, lc * LANES:(lc + 1) * LANES]


def _peer_scores(fnt, wqt_bf16, keys_bf16):
    d, s = fnt.shape
    t = 512
    nq = wqt_bf16.shape[0]
    return pl.pallas_call(
        _pscore_kernel,
        grid=(s // t,),
        in_specs=[pl.BlockSpec((d, t), lambda i: (0, i)),
                  pl.BlockSpec((nq, d), lambda i: (0, 0)),
                  pl.BlockSpec((2 * PEER_HEADS, PEER_NKEYS, PEER_HALF), lambda i: (0, 0, 0))],
        out_specs=pl.BlockSpec((2 * PEER_HEADS, t // LANES, PEER_NKEYS, LANES), lambda i: (0, i, 0, 0)),
        out_shape=jax.ShapeDtypeStruct((2 * PEER_HEADS, s // LANES, PEER_NKEYS, LANES), F32),
        compiler_params=_params(1, 48),
        name="peer_scores",
    )(fnt, wqt_bf16, keys_bf16)


SUBLANES = 8
LOG2E = 1.4426950408889634


def _sort_network(n):
    size = 1
    while size < n:
        size *= 2

    def merge(lo, hi, r):
        step = r * 2
        if step < hi - lo:
            yield from merge(lo, hi, step)
            yield from merge(lo + r, hi, step)
            for i in range(lo + r, hi - r, step):
                yield (i, i + r)
        else:
            yield (lo, lo + r)

    def sort(lo, hi):
        if hi - lo >= 1:
            mid = lo + (hi - lo) // 2
            yield from sort(lo, mid)
            yield from sort(mid + 1, hi)
            yield from merge(lo, hi, 1)

    return [(i, j) for i, j in sort(0, size - 1) if j < n]


def _top_values(slabs, k):
    cols = list(slabs)
    for i, j in _sort_network(len(cols)):
        hi = jnp.maximum(cols[i], cols[j])
        cols[j] = jnp.minimum(cols[i], cols[j])
        cols[i] = hi
    vals = []
    for r in range(k):
        m = jnp.max(cols[0], axis=0, keepdims=True)
        vals.append(m)
        depth = min(len(cols), k - r)
        if r == k - 1:
            break
        hit = cols[0] == m
        for q in range(depth - 1):
            cols[q] = jnp.where(hit, cols[q + 1], cols[q])
        if depth == len(cols):
            cols[depth - 1] = jnp.where(hit, -jnp.inf, cols[depth - 1])
    return vals


def _rows_to_slabs(rows, n_slabs, row_id):
    slabs = []
    for g in range(n_slabs):
        slab = jnp.full(row_id.shape, -jnp.inf, F32)
        for q in range(SUBLANES):
            r = g * SUBLANES + q
            if r < len(rows):
                slab = jnp.where(row_id == q, rows[r], slab)
        slabs.append(slab)
    return slabs


PEER_ETILE = 512
PEER_IB = PEER_ETILE // PEER_NKEYS


PEER_NTOP = PEER_TOPK + 1


def _ptopk_kernel(s_ref, alpha_ref, beta_ref, tau_ref):
    n_lc = s_ref.shape[1]
    n_slabs = PEER_NKEYS // SUBLANES
    row_id = lax.broadcasted_iota(jnp.int32, (SUBLANES, LANES), 0)

    def body(lc, carry):
        sl = pl.ds(pl.multiple_of(lc * LANES, LANES), LANES)
        s1 = s_ref[0, lc]
        s2 = s_ref[1, lc]
        v1 = _top_values([s1[g * SUBLANES:(g + 1) * SUBLANES, :] for g in range(n_slabs)], PEER_NTOP)
        v2 = _top_values([s2[g * SUBLANES:(g + 1) * SUBLANES, :] for g in range(n_slabs)], PEER_NTOP)
        n_vs = -(-PEER_NTOP // SUBLANES)
        v1s = _rows_to_slabs(v1, n_vs, row_id)
        v2s = _rows_to_slabs(v2, n_vs, row_id)
        cand = [v1[0] + slab for slab in v2s]
        for r1 in range(1, SUBLANES):
            n_ok = PEER_NTOP // (r1 + 1)
            pair = v1[r1] + v2s[0]
            cand.append(pair if n_ok >= SUBLANES else jnp.where(row_id < n_ok, pair, -jnp.inf))
        for slab in v1s[1:]:
            cand.append(slab + v2[0])
        best = _top_values(cand, PEER_NTOP)
        z = jnp.ones_like(best[0])
        for r in range(1, PEER_TOPK):
            z = z + jnp.exp(best[r] - best[0])
        log_norm = best[0] + jnp.log(z)
        alpha = (s1 - log_norm) * LOG2E
        for g in range(PEER_NKEYS // PEER_IB):
            alpha_ref[g, :, sl] = alpha[g * PEER_IB:(g + 1) * PEER_IB, :]
        beta_ref[lc] = s2 * LOG2E
        tau_ref[:, sl] = (0.5 * (best[PEER_TOPK - 1] + best[PEER_TOPK]) - log_norm) * LOG2E
        return carry

    lax.fori_loop(0, n_lc, body, 0)


def _peer_topk(scores):
    hp, n_chunks, nk, _ = scores.shape
    s = n_chunks * LANES
    t = 512
    ng = nk // PEER_IB
    return pl.pallas_call(
        _ptopk_kernel,
        grid=(s // t, PEER_HEADS),
        in_specs=[pl.BlockSpec((2, t // LANES, nk, LANES), lambda i, h: (h, i, 0, 0))],
        out_specs=[pl.BlockSpec((None, ng, PEER_IB, t), lambda i, h: (h, 0, 0, i)),
                   pl.BlockSpec((None, t // LANES, nk, LANES), lambda i, h: (h, i, 0, 0)),
                   pl.BlockSpec((None, 1, t), lambda i, h: (h, 0, i))],
        out_shape=[jax.ShapeDtypeStruct((PEER_HEADS, ng, PEER_IB, s), F32),
                   jax.ShapeDtypeStruct((PEER_HEADS, n_chunks, nk, LANES), F32),
                   jax.ShapeDtypeStruct((PEER_HEADS, 1, s), F32)],
        compiler_params=_params(2, 32),
        name="peer_topk",
    )(scores)


MXU_COLS = 256
Z_ROWS = 128
O_ROWS = 128
GATE_ROWS = 16


def _gelu_tanh(x):
    return 0.5 * x * (1.0 + jnp.tanh(0.7978845608028654 * (x + 0.044715 * (x * x * x))))


def _pdense_step(fnt_ref, u_ref, vt_ref, alpha_ref, beta_ref, tau_ref, o_ref,
                 z_prev_ref, z_next_ref, a_prev_ref, a_next_ref):
    t = fnt_ref.shape[1]
    n_lc = t // LANES
    d_model = vt_ref.shape[0]
    n_exp = u_ref.shape[0]
    mxu_chunks = []
    for c0 in range(0, t, MXU_COLS):
        cols = slice(c0, c0 + MXU_COLS)
        zs = [("z", cols, slice(r0, r0 + Z_ROWS)) for r0 in range(0, n_exp, Z_ROWS)]
        os_ = [("o", cols, slice(r0, r0 + O_ROWS)) for r0 in range(0, d_model, O_ROWS)]
        per_z = len(os_) // len(zs)
        for k, zp in enumerate(zs):
            mxu_chunks.append(zp)
            mxu_chunks.extend(os_[k * per_z:(k + 1) * per_z])
    n_jb = PEER_NKEYS // GATE_ROWS
    n_blocks = n_lc * n_jb
    place = {}
    for k, chunk in enumerate(mxu_chunks):
        place.setdefault(k * n_blocks // len(mxu_chunks), []).append(chunk)
    for lc in range(n_lc):
        sl = slice(lc * LANES, (lc + 1) * LANES)
        for jb in range(n_jb):
            for kind, cols, mrows in place.get(lc * n_jb + jb, ()):
                slabs = range(cols.start // LANES, cols.stop // LANES)
                if kind == "z":
                    zc = _dot(u_ref[mrows, :], fnt_ref[:, cols])
                    for k, slab in enumerate(slabs):
                        z_next_ref[slab, mrows, :] = zc[:, k * LANES:(k + 1) * LANES]
                else:
                    ac = jnp.concatenate([a_prev_ref[slab] for slab in slabs], axis=1)
                    o_ref[mrows, cols] += _dot(vt_ref[mrows, :], ac)
            jrows = slice(jb * GATE_ROWS, (jb + 1) * GATE_ROWS)
            accs = [jnp.zeros((GATE_ROWS, LANES), F32) for _ in range(PEER_IB)]
            for h in range(PEER_HEADS):
                beta = beta_ref[h, lc, jrows, :]
                tau = tau_ref[h, :, sl]
                for il in range(PEER_IB):
                    lg = alpha_ref[h, il:il + 1, sl] + beta
                    accs[il] = accs[il] + jnp.where(lg >= tau, jnp.exp2(lg), 0.0)
            for il in range(PEER_IB):
                rows = slice(il * PEER_NKEYS + jb * GATE_ROWS, il * PEER_NKEYS + (jb + 1) * GATE_ROWS)
                a_next_ref[lc, rows, :] = (accs[il] * _gelu_tanh(z_prev_ref[lc, rows, :])).astype(BF16)


def _pdense_kernel(fnt_ref, u_ref, vt_ref, alpha_ref, beta_ref, tau_ref, o_ref,
                   z0_ref, z1_ref, a0_ref, a1_ref, *, n_et):
    s = pl.program_id(0)
    ins = (fnt_ref, u_ref, vt_ref, alpha_ref, beta_ref, tau_ref, o_ref)

    @pl.when(s == 0)
    def _():
        z1_ref[...] = jnp.zeros_like(z1_ref)
        a1_ref[...] = jnp.zeros_like(a1_ref)

    @pl.when((s == 0) | ((s + n_et - 2) % n_et == 0))
    def _():
        o_ref[...] = jnp.zeros_like(o_ref)

    @pl.when(s % 2 == 0)
    def _():
        _pdense_step(*ins, z1_ref, z0_ref, a1_ref, a0_ref)

    @pl.when(s % 2 == 1)
    def _():
        _pdense_step(*ins, z0_ref, z1_ref, a0_ref, a1_ref)


def _peer_dense(fnt, u_bf16, vt_bf16, alpha, beta, tau):
    d, s = fnt.shape
    ne = u_bf16.shape[0]
    t = 512
    et = PEER_ETILE
    n_et = ne // et
    assert n_et % 2 == 0
    nk = beta.shape[2]
    n_lc = t // LANES
    n_pairs = (s // t) * n_et

    def pair(step, lag):
        p = jnp.clip(step - lag, 0, n_pairs - 1)
        return p // n_et, p % n_et

    return pl.pallas_call(
        functools.partial(_pdense_kernel, n_et=n_et),
        grid=(n_pairs + 2,),
        in_specs=[pl.BlockSpec((d, t), lambda p: (0, pair(p, 0)[0])),
                  pl.BlockSpec((et, d), lambda p: (pair(p, 0)[1], 0)),
                  pl.BlockSpec((None, d, et), lambda p: (pair(p, 2)[1], 0, 0)),
                  pl.BlockSpec((PEER_HEADS, None, PEER_IB, t), lambda p: (0, pair(p, 1)[1], 0, pair(p, 1)[0])),
                  pl.BlockSpec((PEER_HEADS, n_lc, nk, LANES), lambda p: (0, pair(p, 1)[0], 0, 0)),
                  pl.BlockSpec((PEER_HEADS, 1, t), lambda p: (0, 0, pair(p, 1)[0]))],
        out_specs=pl.BlockSpec((d, t), lambda p: (0, pair(p, 2)[0])),
        out_shape=jax.ShapeDtypeStruct((d, s), F32),
        scratch_shapes=[pltpu.VMEM((n_lc, et, LANES), F32), pltpu.VMEM((n_lc, et, LANES), F32),
                        pltpu.VMEM((n_lc, et, LANES), BF16), pltpu.VMEM((n_lc, et, LANES), BF16)],
        compiler_params=_params(1, 48),
        name="peer_dense",
    )(fnt, u_bf16, vt_bf16, alpha, beta, tau)


def _final_kernel(x1_ref, ot_ref, mod_ref, gain_ref, y_ref, *, apply_norm):
    x2 = x1_ref[...] + mod_ref[5:6, :] * ot_ref[...].T
    if apply_norm:
        ms = jnp.mean(x2 * x2, axis=-1, keepdims=True)
        x2 = x2 * lax.rsqrt(ms + EPS) * gain_ref[...]
    y_ref[...] = x2


def _final(x1, out_t, mod, gain, apply_norm):
    s, d = x1.shape
    tm = 512
    return pl.pallas_call(
        functools.partial(_final_kernel, apply_norm=apply_norm),
        grid=(s // tm,),
        in_specs=[pl.BlockSpec((tm, d), lambda i: (i, 0)),
                  pl.BlockSpec((d, tm), lambda i: (0, i)),
                  pl.BlockSpec((6, d), lambda i: (0, 0)),
                  pl.BlockSpec((1, d), lambda i: (0, 0))],
        out_specs=pl.BlockSpec((tm, d), lambda i: (i, 0)),
        out_shape=jax.ShapeDtypeStruct((s, d), F32),
        compiler_params=_params(1, 48),
        name="final",
    )(x1, out_t, mod, gain.reshape(1, d))


def kernel(x, c, positions, norm_mix_gain, w_ada, b_ada, w_in, pool_w, pool_scale, ret_norm_gain,
           w_branch_pool, w_branch_ret, w_out, norm_ffn_gain, peer_w_query, peer_sub_keys, peer_u, peer_v,
           final_norm_gain):
    batch, s, d = x.shape
    assert batch == 1 and d == D_MODEL and s % 1024 == 0
    depth = w_in.shape[0]
    xs = x.reshape(s, d)
    cos, sin = _rope_tables(positions.reshape(s))
    for l in range(depth):
        mod = _ada(c, w_ada[l], b_ada[l])
        proj = _in_proj(xs, norm_mix_gain[l], mod, w_in[l])
        pool_out = _pool(proj, pool_w[l].astype(BF16), pool_scale[l])
        ret_out = _retention(proj, cos, sin, ret_norm_gain[l])
        merged = _branch(pool_out, ret_out, proj, w_branch_pool[l], w_branch_ret[l])
        x1, fnt = _out_proj(merged, w_out[l].astype(BF16), xs, mod, norm_ffn_gain[l])
        keys = peer_sub_keys[l].reshape(2 * PEER_HEADS, PEER_NKEYS, PEER_HALF).astype(BF16)
        scores = _peer_scores(fnt, peer_w_query[l].T.astype(BF16), keys)
        alpha, beta, tau = _peer_topk(scores)
        vt_tiles = peer_v[l].astype(BF16).reshape(N_EXPERTS // PEER_ETILE, PEER_ETILE, d).transpose(0, 2, 1)
        out_t = _peer_dense(fnt, peer_u[l].astype(BF16), vt_tiles, alpha, beta, tau)
        xs = _final(x1, out_t, mod, final_norm_gain, apply_norm=(l == depth - 1))
    return xs.reshape(batch, s, d)
```

```python
import functools

import jax
import jax.numpy as jnp
from jax import lax
from jax.experimental import pallas as pl
from jax.experimental.pallas import tpu as pltpu

F32 = jnp.float32
BF16 = jnp.bfloat16

D_MODEL = 2048
EPS = 1e-6
POOL_GROUPS = 4
POOL_GDIM = 512
RET_HEADS = 8
RET_DK = 256
RET_DV = 512
ROPE_BASE = 10000.0
PEER_HEADS = 8
PEER_NKEYS = 128
PEER_HALF = 128
PEER_TOPK = 16
N_EXPERTS = PEER_NKEYS * PEER_NKEYS

COL_POOL, COL_Q, COL_K, COL_V, COL_G, COL_APOOL, COL_ARET = 0, 2048, 4096, 6144, 10240, 14336, 16384
IN_COLS = 18432

LANES = 128
MIB = 1024 * 1024

RET_CHUNK = 256


def _params(n_axes, vmem_mib, flags=None):
    return pltpu.CompilerParams(dimension_semantics=("arbitrary",) * n_axes,
                                vmem_limit_bytes=vmem_mib * MIB, flags=flags)


def _dot(a, b):
    return jnp.dot(a, b, preferred_element_type=F32)


def _sigmoid(x):
    return 1.0 / (1.0 + jnp.exp(-x))


def _ada_kernel(c_ref, w_ref, b_ref, o_ref):
    c = c_ref[...]
    cond = c * _sigmoid(c)
    o_ref[...] = jnp.sum(cond * w_ref[...], axis=0, keepdims=True) + b_ref[...]


def _ada(c, w, b):
    d, n = w.shape
    tn = 1024
    out = pl.pallas_call(
        _ada_kernel,
        grid=(n // tn,),
        in_specs=[pl.BlockSpec((d, 1), lambda j: (0, 0)),
                  pl.BlockSpec((d, tn), lambda j: (0, j)),
                  pl.BlockSpec((1, tn), lambda j: (0, j))],
        out_specs=pl.BlockSpec((1, tn), lambda j: (0, j)),
        out_shape=jax.ShapeDtypeStruct((1, n), F32),
        compiler_params=_params(1, 40),
        name="ada",
    )(c.reshape(d, 1), w, b.reshape(1, n))
    return out.reshape(6, d)


def _in_kernel(x_ref, gain_ref, mod_ref, w_ref, o_ref, hn_ref):
    tm = x_ref.shape[0]
    rc = 256

    @pl.when(pl.program_id(1) == 0)
    def _():
        scale = gain_ref[...] * (1.0 + mod_ref[1:2, :])
        shift = mod_ref[0:1, :]

        def body(r, carry):
            rows = pl.ds(pl.multiple_of(r * rc, rc), rc)
            x = x_ref[rows, :]
            ms = jnp.mean(x * x, axis=-1, keepdims=True)
            hn_ref[rows, :] = (x * lax.rsqrt(ms + EPS) * scale + shift).astype(BF16)
            return carry

        lax.fori_loop(0, tm // rc, body, 0)

    o_ref[...] = _dot(hn_ref[...], w_ref[...].astype(BF16)).astype(BF16)


def _in_proj(x2d, gain, mod, w):
    s, d = x2d.shape
    n = w.shape[1]
    tm, tn = 1024, 1024
    return pl.pallas_call(
        _in_kernel,
        grid=(s // tm, n // tn),
        in_specs=[pl.BlockSpec((tm, d), lambda i, j: (i, 0)),
                  pl.BlockSpec((1, d), lambda i, j: (0, 0)),
                  pl.BlockSpec((6, d), lambda i, j: (0, 0)),
                  pl.BlockSpec((d, tn), lambda i, j: (0, j))],
        out_specs=pl.BlockSpec((tm, tn), lambda i, j: (i, j)),
        out_shape=jax.ShapeDtypeStruct((s, n), BF16),
        scratch_shapes=[pltpu.VMEM((tm, d), BF16)],
        compiler_params=_params(2, 54),
        name="in_proj",
    )(x2d, gain.reshape(1, d), mod, w)


def _rope_kernel(pos_ref, inv_ref, cos_ref, sin_ref):
    ang = pos_ref[...].astype(F32) * inv_ref[...]
    cos_ref[...] = jnp.cos(ang)
    sin_ref[...] = jnp.sin(ang)


def _rope_tables(positions):
    s = positions.shape[0]
    half = RET_DK // 2
    inv_freq = (ROPE_BASE ** (-jnp.arange(0, RET_DK, 2, dtype=F32) / RET_DK)).reshape(1, half)
    tm = 1024
    return pl.pallas_call(
        _rope_kernel,
        grid=(s // tm,),
        in_specs=[pl.BlockSpec((tm, 1), lambda i: (i, 0)),
                  pl.BlockSpec((1, half), lambda i: (0, 0))],
        out_specs=[pl.BlockSpec((tm, half), lambda i: (i, 0)),
                   pl.BlockSpec((tm, half), lambda i: (i, 0))],
        out_shape=[jax.ShapeDtypeStruct((s, half), F32), jax.ShapeDtypeStruct((s, half), F32)],
        compiler_params=_params(1, 32),
        name="rope",
    )(positions.reshape(s, 1), inv_freq)


POOL_HALO = 16


def _pool_kernel(u_ref, halo_ref, pw_ref, ps_ref, o_ref):
    i = pl.program_id(0)
    g = pl.program_id(1)
    tm = u_ref.shape[0]
    w = jnp.left_shift(2, g)
    r = lax.broadcasted_iota(jnp.int32, (tm, tm), 0)
    c = lax.broadcasted_iota(jnp.int32, (tm, tm), 1)
    band = jnp.where(c <= r, jnp.where(c > r - w, 1.0, 0.0), 0.0).astype(BF16)
    rh = lax.broadcasted_iota(jnp.int32, (tm, POOL_HALO), 0)
    ch = lax.broadcasted_iota(jnp.int32, (tm, POOL_HALO), 1)
    halo_on = jnp.where(i > 0, 1.0, 0.0)
    bandh = (jnp.where(ch > rh + POOL_HALO - w, 1.0, 0.0) * halo_on).astype(BF16)
    u = u_ref[...]
    wsum = _dot(band, u) + _dot(bandh, halo_ref[...])
    t = i * tm + lax.broadcasted_iota(jnp.int32, (tm, 1), 0)
    cnt = jnp.minimum(t + 1, w).astype(F32)
    pooled = wsum / cnt - u.astype(F32)
    o_ref[...] = (_dot(pooled.astype(BF16), pw_ref[...]) * ps_ref[...]).astype(BF16)


def _pool(proj, pool_w_bf16, pool_scale):
    s = proj.shape[0]
    tm = 512
    hb = tm // POOL_HALO
    return pl.pallas_call(
        _pool_kernel,
        grid=(s // tm, POOL_GROUPS),
        in_specs=[pl.BlockSpec((tm, POOL_GDIM), lambda i, g: (i, g)),
                  pl.BlockSpec((POOL_HALO, POOL_GDIM), lambda i, g: (jnp.maximum(i * hb - 1, 0), g)),
                  pl.BlockSpec((None, POOL_GDIM, POOL_GDIM), lambda i, g: (g, 0, 0)),
                  pl.BlockSpec((1, POOL_GDIM), lambda i, g: (0, g))],
        out_specs=pl.BlockSpec((tm, POOL_GDIM), lambda i, g: (i, g)),
        out_shape=jax.ShapeDtypeStruct((s, POOL_GROUPS * POOL_GDIM), BF16),
        compiler_params=_params(2, 32),
        name="pool",
    )(proj, proj, pool_w_bf16, pool_scale.reshape(1, -1))


RET_HPS = 2


def _ret_kernel(cdec_ref, q_ref, k_ref, v_ref, g_ref, cos_ref, sin_ref, dec_ref, qd_ref, kd_ref,
                gain_ref, o_ref, state_ref):
    hp = pl.program_id(0)

    @pl.when(pl.program_id(1) == 0)
    def _():
        state_ref[...] = jnp.zeros_like(state_ref)

    cos = cos_ref[...]
    sin = sin_ref[...]
    half = RET_DK // 2

    def rot(t_ref, col0):
        t1 = t_ref[:, col0:col0 + half].astype(F32)
        t2 = t_ref[:, col0 + half:col0 + RET_DK].astype(F32)
        return t1 * cos - t2 * sin, t2 * cos + t1 * sin

    for j in range(RET_HPS):
        q1, q2 = rot(q_ref, j * RET_DK)
        k1, k2 = rot(k_ref, j * RET_DK)
        qd = qd_ref[j]
        kd = kd_ref[j]
        qb = jnp.concatenate([q1, q2], axis=1).astype(BF16)
        kb = jnp.concatenate([k1, k2], axis=1).astype(BF16)
        qdb = jnp.concatenate([q1 * qd, q2 * qd], axis=1).astype(BF16)
        kdb = jnp.concatenate([k1 * kd, k2 * kd], axis=1).astype(BF16)
        vcols = slice(j * RET_DV, (j + 1) * RET_DV)
        v = v_ref[:, vcols]
        scores = lax.dot_general(qb, kb, (((1,), (1,)), ((), ())), preferred_element_type=F32) * dec_ref[j]
        inner = _dot(scores.astype(BF16), v)
        state = state_ref[j]
        cross = _dot(qdb, state.astype(BF16))
        state_ref[j] = state * cdec_ref[hp * RET_HPS + j] + lax.dot_general(
            kdb, v, (((0,), (0,)), ((), ())), preferred_element_type=F32)
        y = inner + cross
        mu = jnp.mean(y, axis=-1, keepdims=True)
        yc = y - mu
        var = jnp.mean(yc * yc, axis=-1, keepdims=True)
        yn = yc * lax.rsqrt(var + EPS) * gain_ref[:, vcols]
        gate = g_ref[:, vcols].astype(F32)
        o_ref[:, vcols] = (gate * _sigmoid(gate) * yn).astype(BF16)


def _ret_tables():
    c = RET_CHUNK
    log_g = jnp.log(1.0 - 2.0 ** (-5.0 - jnp.arange(RET_HEADS, dtype=F32)))
    idx = jnp.arange(c, dtype=F32)
    diff = idx[:, None] - idx[None, :]
    kscale = RET_DK ** -0.5
    dec = jnp.where(diff[None] >= 0, jnp.exp(jnp.maximum(diff, 0.0)[None] * log_g[:, None, None]), 0.0) * kscale
    qd = jnp.exp((idx + 1.0)[None, :] * log_g[:, None])
    kd = jnp.exp((c - 1.0 - idx)[None, :] * log_g[:, None]) * kscale
    half = RET_DK // 2
    qd = jnp.broadcast_to(qd[:, :, None], (RET_HEADS, c, half))
    kd = jnp.broadcast_to(kd[:, :, None], (RET_HEADS, c, half))
    cdec = jnp.exp(c * log_g)
    return cdec, dec, qd, kd


def _retention(proj, cos, sin, ret_gain):
    s = proj.shape[0]
    c = RET_CHUNK
    half = RET_DK // 2
    cdec, dec, qd, kd = _ret_tables()
    wqk, wv = RET_HPS * RET_DK, RET_HPS * RET_DV
    qb0, kb0 = COL_Q // wqk, COL_K // wqk
    vb0, gb0 = COL_V // wv, COL_G // wv
    return pl.pallas_call(
        _ret_kernel,
        grid=(RET_HEADS // RET_HPS, s // c),
        in_specs=[pl.BlockSpec(memory_space=pltpu.SMEM),
                  pl.BlockSpec((c, wqk), lambda h, i: (i, qb0 + h)),
                  pl.BlockSpec((c, wqk), lambda h, i: (i, kb0 + h)),
                  pl.BlockSpec((c, wv), lambda h, i: (i, vb0 + h)),
                  pl.BlockSpec((c, wv), lambda h, i: (i, gb0 + h)),
                  pl.BlockSpec((c, half), lambda h, i: (i, 0)),
                  pl.BlockSpec((c, half), lambda h, i: (i, 0)),
                  pl.BlockSpec((RET_HPS, c, c), lambda h, i: (h, 0, 0)),
                  pl.BlockSpec((RET_HPS, c, half), lambda h, i: (h, 0, 0)),
                  pl.BlockSpec((RET_HPS, c, half), lambda h, i: (h, 0, 0)),
                  pl.BlockSpec((1, wv), lambda h, i: (0, h))],
        out_specs=pl.BlockSpec((c, wv), lambda h, i: (i, h)),
        out_shape=jax.ShapeDtypeStruct((s, RET_HEADS * RET_DV), BF16),
        scratch_shapes=[pltpu.VMEM((RET_HPS, RET_DK, RET_DV), F32)],
        compiler_params=_params(2, 32),
        name="retention",
    )(cdec, proj, proj, proj, proj, cos, sin, dec, qd, kd, ret_gain.reshape(1, -1))


def _branch_kernel(p_ref, r_ref, ap_ref, ar_ref, wp_ref, wr_ref, o_ref):
    bp = _dot(p_ref[...], wp_ref[...])
    br = _dot(r_ref[...], wr_ref[...])
    ap = ap_ref[...].astype(F32)
    ar = ar_ref[...].astype(F32)
    o_ref[...] = (_sigmoid(ap) * bp + _sigmoid(ar) * br).astype(BF16)


def _branch(pool_out, ret_out, proj, wp, wr):
    s = pool_out.shape[0]
    d = wp.shape[1]
    tm, tn = 512, 512
    ap0, ar0 = COL_APOOL // tn, COL_ARET // tn
    return pl.pallas_call(
        _branch_kernel,
        grid=(s // tm, d // tn),
        in_specs=[pl.BlockSpec((tm, pool_out.shape[1]), lambda i, j: (i, 0)),
                  pl.BlockSpec((tm, ret_out.shape[1]), lambda i, j: (i, 0)),
                  pl.BlockSpec((tm, tn), lambda i, j: (i, ap0 + j)),
                  pl.BlockSpec((tm, tn), lambda i, j: (i, ar0 + j)),
                  pl.BlockSpec((wp.shape[0], tn), lambda i, j: (0, j)),
                  pl.BlockSpec((wr.shape[0], tn), lambda i, j: (0, j))],
        out_specs=pl.BlockSpec((tm, tn), lambda i, j: (i, j)),
        out_shape=jax.ShapeDtypeStruct((s, d), BF16),
        compiler_params=_params(2, 40),
        name="branch",
    )(pool_out, ret_out, proj, proj, wp.astype(BF16), wr.astype(BF16))


def _out_kernel(m_ref, w_ref, x_ref, mod_ref, gain_ref, x1_ref, fnt_ref):
    x1 = x_ref[...] + mod_ref[2:3, :] * _dot(m_ref[...], w_ref[...])
    x1_ref[...] = x1
    ms = jnp.mean(x1 * x1, axis=-1, keepdims=True)
    fn = x1 * lax.rsqrt(ms + EPS) * (gain_ref[...] * (1.0 + mod_ref[4:5, :])) + mod_ref[3:4, :]
    fnt_ref[...] = fn.T.astype(BF16)


def _out_proj(merged, w_bf16, x2d, mod, gain):
    s, d = x2d.shape
    tm = 256
    return pl.pallas_call(
        _out_kernel,
        grid=(s // tm,),
        in_specs=[pl.BlockSpec((tm, d), lambda i: (i, 0)),
                  pl.BlockSpec((d, d), lambda i: (0, 0)),
                  pl.BlockSpec((tm, d), lambda i: (i, 0)),
                  pl.BlockSpec((6, d), lambda i: (0, 0)),
                  pl.BlockSpec((1, d), lambda i: (0, 0))],
        out_specs=[pl.BlockSpec((tm, d), lambda i: (i, 0)),
                   pl.BlockSpec((d, tm), lambda i: (0, i))],
        out_shape=[jax.ShapeDtypeStruct((s, d), F32), jax.ShapeDtypeStruct((d, s), BF16)],
        compiler_params=_params(1, 48),
        name="out_proj",
    )(merged, w_bf16, x2d, mod, gain.reshape(1, d))


def _pscore_kernel(fnt_ref, wqt_ref, keys_ref, s_ref):
    t = fnt_ref.shape[1]
    qt = _dot(wqt_ref[...], fnt_ref[...]).astype(BF16)
    for hp in range(2 * PEER_HEADS):
        sc = _dot(keys_ref[hp], qt[hp * PEER_HALF:(hp + 1) * PEER_HALF, :])
        for lc in range(t // LANES):
            s_ref[hp, lc] = sc[:, lc * LANES:(lc + 1) * LANES]


def _peer_scores(fnt, wqt_bf16, keys_bf16):
    d, s = fnt.shape
    t = 512
    nq = wqt_bf16.shape[0]
    return pl.pallas_call(
        _pscore_kernel,
        grid=(s // t,),
        in_specs=[pl.BlockSpec((d, t), lambda i: (0, i)),
                  pl.BlockSpec((nq, d), lambda i: (0, 0)),
                  pl.BlockSpec((2 * PEER_HEADS, PEER_NKEYS, PEER_HALF), lambda i: (0, 0, 0))],
        out_specs=pl.BlockSpec((2 * PEER_HEADS, t // LANES, PEER_NKEYS, LANES), lambda i: (0, i, 0, 0)),
        out_shape=jax.ShapeDtypeStruct((2 * PEER_HEADS, s // LANES, PEER_NKEYS, LANES), F32),
        compiler_params=_params(1, 48),
        name="peer_scores",
    )(fnt, wqt_bf16, keys_bf16)


SUBLANES = 8
LOG2E = 1.4426950408889634


def _sort_network(n):
    size = 1
    while size < n:
        size *= 2

    def merge(lo, hi, r):
        step = r * 2
        if step < hi - lo:
            yield from merge(lo, hi, step)
            yield from merge(lo + r, hi, step)
            for i in range(lo + r, hi - r, step):
                yield (i, i + r)
        else:
            yield (lo, lo + r)

    def sort(lo, hi):
        if hi - lo >= 1:
            mid = lo + (hi - lo) // 2
            yield from sort(lo, mid)
            yield from sort(mid + 1, hi)
            yield from merge(lo, hi, 1)

    return [(i, j) for i, j in sort(0, size - 1) if j < n]


def _top_values(slabs, k):
    cols = list(slabs)
    for i, j in _sort_network(len(cols)):
        hi = jnp.maximum(cols[i], cols[j])
        cols[j] = jnp.minimum(cols[i], cols[j])
        cols[i] = hi
    vals = []
    for r in range(k):
        m = jnp.max(cols[0], axis=0, keepdims=True)
        vals.append(m)
        depth = min(len(cols), k - r)
        if r == k - 1:
            break
        hit = cols[0] == m
        for q in range(depth - 1):
            cols[q] = jnp.where(hit, cols[q + 1], cols[q])
        if depth == len(cols):
            cols[depth - 1] = jnp.where(hit, -jnp.inf, cols[depth - 1])
    return vals


def _rows_to_slabs(rows, n_slabs, row_id):
    slabs = []
    for g in range(n_slabs):
        slab = jnp.full(row_id.shape, -jnp.inf, F32)
        for q in range(SUBLANES):
            r = g * SUBLANES + q
            if r < len(rows):
                slab = jnp.where(row_id == q, rows[r], slab)
        slabs.append(slab)
    return slabs


PEER_ETILE = 512
PEER_IB = PEER_ETILE // PEER_NKEYS


PEER_NTOP = PEER_TOPK + 1


def _ptopk_kernel(s_ref, alpha_ref, beta_ref, tau_ref):
    n_lc = s_ref.shape[1]
    n_slabs = PEER_NKEYS // SUBLANES
    row_id = lax.broadcasted_iota(jnp.int32, (SUBLANES, LANES), 0)

    def body(lc, carry):
        sl = pl.ds(pl.multiple_of(lc * LANES, LANES), LANES)
        s1 = s_ref[0, lc]
        s2 = s_ref[1, lc]
        v1 = _top_values([s1[g * SUBLANES:(g + 1) * SUBLANES, :] for g in range(n_slabs)], PEER_NTOP)
        v2 = _top_values([s2[g * SUBLANES:(g + 1) * SUBLANES, :] for g in range(n_slabs)], PEER_NTOP)
        n_vs = -(-PEER_NTOP // SUBLANES)
        v1s = _rows_to_slabs(v1, n_vs, row_id)
        v2s = _rows_to_slabs(v2, n_vs, row_id)
        cand = [v1[0] + slab for slab in v2s]
        for r1 in range(1, SUBLANES):
            n_ok = PEER_NTOP // (r1 + 1)
            pair = v1[r1] + v2s[0]
            cand.append(pair if n_ok >= SUBLANES else jnp.where(row_id < n_ok, pair, -jnp.inf))
        for slab in v1s[1:]:
            cand.append(slab + v2[0])
        best = _top_values(cand, PEER_NTOP)
        z = jnp.ones_like(best[0])
        for r in range(1, PEER_TOPK):
            z = z + jnp.exp(best[r] - best[0])
        log_norm = best[0] + jnp.log(z)
        alpha = (s1 - log_norm) * LOG2E
        for g in range(PEER_NKEYS // PEER_IB):
            alpha_ref[g, :, sl] = alpha[g * PEER_IB:(g + 1) * PEER_IB, :]
        beta_ref[lc] = s2 * LOG2E
        tau_ref[:, sl] = (0.5 * (best[PEER_TOPK - 1] + best[PEER_TOPK]) - log_norm) * LOG2E
        return carry

    lax.fori_loop(0, n_lc, body, 0, unroll=4)


def _peer_topk(scores):
    hp, n_chunks, nk, _ = scores.shape
    s = n_chunks * LANES
    t = 512
    ng = nk // PEER_IB
    return pl.pallas_call(
        _ptopk_kernel,
        grid=(s // t, PEER_HEADS),
        in_specs=[pl.BlockSpec((2, t // LANES, nk, LANES), lambda i, h: (h, i, 0, 0))],
        out_specs=[pl.BlockSpec((None, ng, PEER_IB, t), lambda i, h: (h, 0, 0, i)),
                   pl.BlockSpec((None, t // LANES, nk, LANES), lambda i, h: (h, i, 0, 0)),
                   pl.BlockSpec((None, 1, t), lambda i, h: (h, 0, i))],
        out_shape=[jax.ShapeDtypeStruct((PEER_HEADS, ng, PEER_IB, s), F32),
                   jax.ShapeDtypeStruct((PEER_HEADS, n_chunks, nk, LANES), F32),
                   jax.ShapeDtypeStruct((PEER_HEADS, 1, s), F32)],
        compiler_params=_params(2, 32),
        name="peer_topk",
    )(scores)


MXU_COLS = 256
Z_ROWS = 128
O_ROWS = 128
GATE_ROWS = 16


def _gelu_tanh(x):
    return 0.5 * x * (1.0 + jnp.tanh(0.7978845608028654 * (x + 0.044715 * (x * x * x))))


def _pdense_step(fnt_ref, u_ref, vt_ref, alpha_ref, beta_ref, tau_ref, o_ref,
                 z_prev_ref, z_next_ref, a_prev_ref, a_next_ref):
    t = fnt_ref.shape[1]
    n_lc = t // LANES
    d_model = vt_ref.shape[0]
    n_exp = u_ref.shape[0]
    mxu_chunks = []
    for c0 in range(0, t, MXU_COLS):
        cols = slice(c0, c0 + MXU_COLS)
        zs = [("z", cols, slice(r0, r0 + Z_ROWS)) for r0 in range(0, n_exp, Z_ROWS)]
        os_ = [("o", cols, slice(r0, r0 + O_ROWS)) for r0 in range(0, d_model, O_ROWS)]
        per_z = len(os_) // len(zs)
        for k, zp in enumerate(zs):
            mxu_chunks.append(zp)
            mxu_chunks.extend(os_[k * per_z:(k + 1) * per_z])
    n_jb = PEER_NKEYS // GATE_ROWS
    n_blocks = n_lc * n_jb
    place = {}
    for k, chunk in enumerate(mxu_chunks):
        place.setdefault(k * n_blocks // len(mxu_chunks), []).append(chunk)
    for lc in range(n_lc):
        sl = slice(lc * LANES, (lc + 1) * LANES)
        for jb in range(n_jb):
            for kind, cols, mrows in place.get(lc * n_jb + jb, ()):
                slabs = range(cols.start // LANES, cols.stop // LANES)
                if kind == "z":
                    zc = _dot(u_ref[mrows, :], fnt_ref[:, cols])
                    for k, slab in enumerate(slabs):
                        z_next_ref[slab, mrows, :] = zc[:, k * LANES:(k + 1) * LANES]
                else:
                    ac = jnp.concatenate([a_prev_ref[slab] for slab in slabs], axis=1)
                    o_ref[mrows, cols] += _dot(vt_ref[mrows, :], ac)
            jrows = slice(jb * GATE_ROWS, (jb + 1) * GATE_ROWS)
            accs = [jnp.zeros((GATE_ROWS, LANES), F32) for _ in range(PEER_IB)]
            for h in range(PEER_HEADS):
                beta = beta_ref[h, lc, jrows, :]
                tau = tau_ref[h, :, sl]
                for il in range(PEER_IB):
                    lg = alpha_ref[h, il:il + 1, sl] + beta
                    accs[il] = accs[il] + jnp.where(lg >= tau, jnp.exp2(lg), 0.0)
            for il in range(PEER_IB):
                rows = slice(il * PEER_NKEYS + jb * GATE_ROWS, il * PEER_NKEYS + (jb + 1) * GATE_ROWS)
                a_next_ref[lc, rows, :] = (accs[il] * _gelu_tanh(z_prev_ref[lc, rows, :])).astype(BF16)


def _pdense_kernel(fnt_ref, u_ref, vt_ref, alpha_ref, beta_ref, tau_ref, x1_ref, mod_ref, gain_ref, y_ref,
                   z0_ref, z1_ref, a0_ref, a1_ref, acc_ref, *, n_et, apply_norm):
    s = pl.program_id(0)
    ins = (fnt_ref, u_ref, vt_ref, alpha_ref, beta_ref, tau_ref, acc_ref)
    e_out = (s + n_et - 2) % n_et

    @pl.when(s == 0)
    def _():
        z1_ref[...] = jnp.zeros_like(z1_ref)
        a1_ref[...] = jnp.zeros_like(a1_ref)

    @pl.when((s == 0) | (e_out == 0))
    def _():
        acc_ref[...] = jnp.zeros_like(acc_ref)

    @pl.when(s % 2 == 0)
    def _():
        _pdense_step(*ins, z1_ref, z0_ref, a1_ref, a0_ref)

    @pl.when(s % 2 == 1)
    def _():
        _pdense_step(*ins, z0_ref, z1_ref, a0_ref, a1_ref)

    @pl.when((s >= 2) & (e_out == n_et - 1))
    def _():
        x2 = x1_ref[...] + mod_ref[5:6, :] * acc_ref[...].T
        if apply_norm:
            ms = jnp.mean(x2 * x2, axis=-1, keepdims=True)
            x2 = x2 * lax.rsqrt(ms + EPS) * gain_ref[...]
        y_ref[...] = x2


def _peer_dense(fnt, u_bf16, vt_bf16, alpha, beta, tau, x1, mod, gain, apply_norm):
    d, s = fnt.shape
    ne = u_bf16.shape[0]
    t = 512
    et = PEER_ETILE
    n_et = ne // et
    assert n_et % 2 == 0
    nk = beta.shape[2]
    n_lc = t // LANES
    n_pairs = (s // t) * n_et

    def pair(step, lag):
        p = jnp.clip(step - lag, 0, n_pairs - 1)
        return p // n_et, p % n_et

    return pl.pallas_call(
        functools.partial(_pdense_kernel, n_et=n_et, apply_norm=apply_norm),
        grid=(n_pairs + 2,),
        in_specs=[pl.BlockSpec((d, t), lambda p: (0, pair(p, 0)[0])),
                  pl.BlockSpec((et, d), lambda p: (pair(p, 0)[1], 0)),
                  pl.BlockSpec((None, d, et), lambda p: (pair(p, 2)[1], 0, 0)),
                  pl.BlockSpec((PEER_HEADS, None, PEER_IB, t), lambda p: (0, pair(p, 1)[1], 0, pair(p, 1)[0])),
                  pl.BlockSpec((PEER_HEADS, n_lc, nk, LANES), lambda p: (0, pair(p, 1)[0], 0, 0)),
                  pl.BlockSpec((PEER_HEADS, 1, t), lambda p: (0, 0, pair(p, 1)[0])),
                  pl.BlockSpec((t, d), lambda p: (pair(p, 2)[0], 0)),
                  pl.BlockSpec((6, d), lambda p: (0, 0)),
                  pl.BlockSpec((1, d), lambda p: (0, 0))],
        out_specs=pl.BlockSpec((t, d), lambda p: (pair(p, 2)[0], 0)),
        out_shape=jax.ShapeDtypeStruct((s, d), F32),
        scratch_shapes=[pltpu.VMEM((n_lc, et, LANES), F32), pltpu.VMEM((n_lc, et, LANES), F32),
                        pltpu.VMEM((n_lc, et, LANES), BF16), pltpu.VMEM((n_lc, et, LANES), BF16),
                        pltpu.VMEM((d, t), F32)],
        compiler_params=_params(1, 56),
        name="peer_dense",
    )(fnt, u_bf16, vt_bf16, alpha, beta, tau, x1, mod, gain.reshape(1, d))


def kernel(x, c, positions, norm_mix_gain, w_ada, b_ada, w_in, pool_w, pool_scale, ret_norm_gain,
           w_branch_pool, w_branch_ret, w_out, norm_ffn_gain, peer_w_query, peer_sub_keys, peer_u, peer_v,
           final_norm_gain):
    batch, s, d = x.shape
    assert batch == 1 and d == D_MODEL and s % 1024 == 0
    depth = w_in.shape[0]
    xs = x.reshape(s, d)
    cos, sin = _rope_tables(positions.reshape(s))
    for l in range(depth):
        mod = _ada(c, w_ada[l], b_ada[l])
        proj = _in_proj(xs, norm_mix_gain[l], mod, w_in[l])
        pool_out = _pool(proj, pool_w[l].astype(BF16), pool_scale[l])
        ret_out = _retention(proj, cos, sin, ret_norm_gain[l])
        merged = _branch(pool_out, ret_out, proj, w_branch_pool[l], w_branch_ret[l])
        x1, fnt = _out_proj(merged, w_out[l].astype(BF16), xs, mod, norm_ffn_gain[l])
        keys = peer_sub_keys[l].reshape(2 * PEER_HEADS, PEER_NKEYS, PEER_HALF).astype(BF16)
        scores = _peer_scores(fnt, peer_w_query[l].T.astype(BF16), keys)
        alpha, beta, tau = _peer_topk(scores)
        vt_tiles = peer_v[l].astype(BF16).reshape(N_EXPERTS // PEER_ETILE, PEER_ETILE, d).transpose(0, 2, 1)
        xs = _peer_dense(fnt, peer_u[l].astype(BF16), vt_tiles, alpha, beta, tau, x1, mod, final_norm_gain,
                         apply_norm=(l == depth - 1))
    return xs.reshape(batch, s, d)
```

```python
import functools

import jax
import jax.numpy as jnp
from jax import lax
from jax.experimental import pallas as pl
from jax.experimental.pallas import tpu as pltpu

F32 = jnp.float32
BF16 = jnp.bfloat16

D_MODEL = 2048
EPS = 1e-6
POOL_GROUPS = 4
POOL_GDIM = 512
RET_HEADS = 8
RET_DK = 256
RET_DV = 512
ROPE_BASE = 10000.0
PEER_HEADS = 8
PEER_NKEYS = 128
PEER_HALF = 128
PEER_TOPK = 16
N_EXPERTS = PEER_NKEYS * PEER_NKEYS

COL_POOL, COL_Q, COL_K, COL_V, COL_G, COL_APOOL, COL_ARET = 0, 2048, 4096, 6144, 10240, 14336, 16384
IN_COLS = 18432

LANES = 128
MIB = 1024 * 1024

RET_CHUNK = 256


def _params(n_axes, vmem_mib, flags=None):
    return pltpu.CompilerParams(dimension_semantics=("arbitrary",) * n_axes,
                                vmem_limit_bytes=vmem_mib * MIB, flags=flags)


def _dot(a, b):
    return jnp.dot(a, b, preferred_element_type=F32)


def _sigmoid(x):
    return 1.0 / (1.0 + jnp.exp(-x))


def _ada_kernel(c_ref, w_ref, b_ref, o_ref):
    @pl.when(pl.program_id(0) == 0)
    def _():
        o_ref[...] = b_ref[...]

    c = c_ref[...]
    cond = c * _sigmoid(c)
    o_ref[...] += jnp.sum(cond * w_ref[...], axis=0, keepdims=True)


def _ada(c, w, b):
    d, n = w.shape
    tk = 256
    out = pl.pallas_call(
        _ada_kernel,
        grid=(d // tk,),
        in_specs=[pl.BlockSpec((tk, 1), lambda k: (k, 0)),
                  pl.BlockSpec((tk, n), lambda k: (k, 0)),
                  pl.BlockSpec((1, n), lambda k: (0, 0))],
        out_specs=pl.BlockSpec((1, n), lambda k: (0, 0)),
        out_shape=jax.ShapeDtypeStruct((1, n), F32),
        compiler_params=_params(1, 40),
        name="ada",
    )(c.reshape(d, 1), w, b.reshape(1, n))
    return out.reshape(6, d)


def _in_kernel(x_ref, gain_ref, mod_ref, w_ref, o_ref, hn_ref):
    tm = x_ref.shape[0]
    rc = 256

    @pl.when(pl.program_id(1) == 0)
    def _():
        scale = gain_ref[...] * (1.0 + mod_ref[1:2, :])
        shift = mod_ref[0:1, :]

        def body(r, carry):
            rows = pl.ds(pl.multiple_of(r * rc, rc), rc)
            x = x_ref[rows, :]
            ms = jnp.mean(x * x, axis=-1, keepdims=True)
            hn_ref[rows, :] = (x * lax.rsqrt(ms + EPS) * scale + shift).astype(BF16)
            return carry

        lax.fori_loop(0, tm // rc, body, 0)

    o_ref[...] = _dot(hn_ref[...], w_ref[...].astype(BF16)).astype(BF16)


def _in_proj(x2d, gain, mod, w):
    s, d = x2d.shape
    n = w.shape[1]
    tm, tn = 1024, 1024
    return pl.pallas_call(
        _in_kernel,
        grid=(s // tm, n // tn),
        in_specs=[pl.BlockSpec((tm, d), lambda i, j: (i, 0)),
                  pl.BlockSpec((1, d), lambda i, j: (0, 0)),
                  pl.BlockSpec((6, d), lambda i, j: (0, 0)),
                  pl.BlockSpec((d, tn), lambda i, j: (0, j))],
        out_specs=pl.BlockSpec((tm, tn), lambda i, j: (i, j)),
        out_shape=jax.ShapeDtypeStruct((s, n), BF16),
        scratch_shapes=[pltpu.VMEM((tm, d), BF16)],
        compiler_params=_params(2, 54),
        name="in_proj",
    )(x2d, gain.reshape(1, d), mod, w)


def _rope_kernel(pos_ref, inv_ref, cos_ref, sin_ref):
    ang = pos_ref[...].astype(F32) * inv_ref[...]
    cos_ref[...] = jnp.cos(ang)
    sin_ref[...] = jnp.sin(ang)


def _rope_tables(positions):
    s = positions.shape[0]
    half = RET_DK // 2
    inv_freq = (ROPE_BASE ** (-jnp.arange(0, RET_DK, 2, dtype=F32) / RET_DK)).reshape(1, half)
    tm = 1024
    return pl.pallas_call(
        _rope_kernel,
        grid=(s // tm,),
        in_specs=[pl.BlockSpec((tm, 1), lambda i: (i, 0)),
                  pl.BlockSpec((1, half), lambda i: (0, 0))],
        out_specs=[pl.BlockSpec((tm, half), lambda i: (i, 0)),
                   pl.BlockSpec((tm, half), lambda i: (i, 0))],
        out_shape=[jax.ShapeDtypeStruct((s, half), F32), jax.ShapeDtypeStruct((s, half), F32)],
        compiler_params=_params(1, 32),
        name="rope",
    )(positions.reshape(s, 1), inv_freq)


POOL_HALO = 16


def _pool_kernel(u_ref, halo_ref, pw_ref, ps_ref, o_ref):
    i = pl.program_id(0)
    g = pl.program_id(1)
    tm = u_ref.shape[0]
    w = jnp.left_shift(2, g)
    r = lax.broadcasted_iota(jnp.int32, (tm, tm), 0)
    c = lax.broadcasted_iota(jnp.int32, (tm, tm), 1)
    band = jnp.where(c <= r, jnp.where(c > r - w, 1.0, 0.0), 0.0).astype(BF16)
    rh = lax.broadcasted_iota(jnp.int32, (tm, POOL_HALO), 0)
    ch = lax.broadcasted_iota(jnp.int32, (tm, POOL_HALO), 1)
    halo_on = jnp.where(i > 0, 1.0, 0.0)
    bandh = (jnp.where(ch > rh + POOL_HALO - w, 1.0, 0.0) * halo_on).astype(BF16)
    u = u_ref[...]
    wsum = _dot(band, u) + _dot(bandh, halo_ref[...])
    t = i * tm + lax.broadcasted_iota(jnp.int32, (tm, 1), 0)
    cnt = jnp.minimum(t + 1, w).astype(F32)
    pooled = wsum / cnt - u.astype(F32)
    o_ref[...] = (_dot(pooled.astype(BF16), pw_ref[...]) * ps_ref[...]).astype(BF16)


def _pool(proj, pool_w_bf16, pool_scale):
    s = proj.shape[0]
    tm = 512
    hb = tm // POOL_HALO
    return pl.pallas_call(
        _pool_kernel,
        grid=(s // tm, POOL_GROUPS),
        in_specs=[pl.BlockSpec((tm, POOL_GDIM), lambda i, g: (i, g)),
                  pl.BlockSpec((POOL_HALO, POOL_GDIM), lambda i, g: (jnp.maximum(i * hb - 1, 0), g)),
                  pl.BlockSpec((None, POOL_GDIM, POOL_GDIM), lambda i, g: (g, 0, 0)),
                  pl.BlockSpec((1, POOL_GDIM), lambda i, g: (0, g))],
        out_specs=pl.BlockSpec((tm, POOL_GDIM), lambda i, g: (i, g)),
        out_shape=jax.ShapeDtypeStruct((s, POOL_GROUPS * POOL_GDIM), BF16),
        compiler_params=_params(2, 32),
        name="pool",
    )(proj, proj, pool_w_bf16, pool_scale.reshape(1, -1))


RET_HPS = 2


def _ret_kernel(cdec_ref, q_ref, k_ref, v_ref, g_ref, cos_ref, sin_ref, dec_ref, qd_ref, kd_ref,
                gain_ref, o_ref, state_ref):
    hp = pl.program_id(0)

    @pl.when(pl.program_id(1) == 0)
    def _():
        state_ref[...] = jnp.zeros_like(state_ref)

    cos = cos_ref[...]
    sin = sin_ref[...]
    half = RET_DK // 2

    def rot(t_ref, col0):
        t1 = t_ref[:, col0:col0 + half].astype(F32)
        t2 = t_ref[:, col0 + half:col0 + RET_DK].astype(F32)
        return t1 * cos - t2 * sin, t2 * cos + t1 * sin

    for j in range(RET_HPS):
        q1, q2 = rot(q_ref, j * RET_DK)
        k1, k2 = rot(k_ref, j * RET_DK)
        qd = qd_ref[j]
        kd = kd_ref[j]
        qb = jnp.concatenate([q1, q2], axis=1).astype(BF16)
        kb = jnp.concatenate([k1, k2], axis=1).astype(BF16)
        qdb = jnp.concatenate([q1 * qd, q2 * qd], axis=1).astype(BF16)
        kdb = jnp.concatenate([k1 * kd, k2 * kd], axis=1).astype(BF16)
        vcols = slice(j * RET_DV, (j + 1) * RET_DV)
        v = v_ref[:, vcols]
        scores = lax.dot_general(qb, kb, (((1,), (1,)), ((), ())), preferred_element_type=F32) * dec_ref[j]
        inner = _dot(scores.astype(BF16), v)
        state = state_ref[j]
        cross = _dot(qdb, state.astype(BF16))
        state_ref[j] = state * cdec_ref[hp * RET_HPS + j] + lax.dot_general(
            kdb, v, (((0,), (0,)), ((), ())), preferred_element_type=F32)
        y = inner + cross
        mu = jnp.mean(y, axis=-1, keepdims=True)
        yc = y - mu
        var = jnp.mean(yc * yc, axis=-1, keepdims=True)
        yn = yc * lax.rsqrt(var + EPS) * gain_ref[:, vcols]
        gate = g_ref[:, vcols].astype(F32)
        o_ref[:, vcols] = (gate * _sigmoid(gate) * yn).astype(BF16)


def _ret_tables():
    c = RET_CHUNK
    log_g = jnp.log(1.0 - 2.0 ** (-5.0 - jnp.arange(RET_HEADS, dtype=F32)))
    idx = jnp.arange(c, dtype=F32)
    diff = idx[:, None] - idx[None, :]
    kscale = RET_DK ** -0.5
    dec = jnp.where(diff[None] >= 0, jnp.exp(jnp.maximum(diff, 0.0)[None] * log_g[:, None, None]), 0.0) * kscale
    qd = jnp.exp((idx + 1.0)[None, :] * log_g[:, None])
    kd = jnp.exp((c - 1.0 - idx)[None, :] * log_g[:, None]) * kscale
    half = RET_DK // 2
    qd = jnp.broadcast_to(qd[:, :, None], (RET_HEADS, c, half))
    kd = jnp.broadcast_to(kd[:, :, None], (RET_HEADS, c, half))
    cdec = jnp.exp(c * log_g)
    return cdec, dec, qd, kd


def _retention(proj, cos, sin, ret_gain):
    s = proj.shape[0]
    c = RET_CHUNK
    half = RET_DK // 2
    cdec, dec, qd, kd = _ret_tables()
    wqk, wv = RET_HPS * RET_DK, RET_HPS * RET_DV
    qb0, kb0 = COL_Q // wqk, COL_K // wqk
    vb0, gb0 = COL_V // wv, COL_G // wv
    return pl.pallas_call(
        _ret_kernel,
        grid=(RET_HEADS // RET_HPS, s // c),
        in_specs=[pl.BlockSpec(memory_space=pltpu.SMEM),
                  pl.BlockSpec((c, wqk), lambda h, i: (i, qb0 + h)),
                  pl.BlockSpec((c, wqk), lambda h, i: (i, kb0 + h)),
                  pl.BlockSpec((c, wv), lambda h, i: (i, vb0 + h)),
                  pl.BlockSpec((c, wv), lambda h, i: (i, gb0 + h)),
                  pl.BlockSpec((c, half), lambda h, i: (i, 0)),
                  pl.BlockSpec((c, half), lambda h, i: (i, 0)),
                  pl.BlockSpec((RET_HPS, c, c), lambda h, i: (h, 0, 0)),
                  pl.BlockSpec((RET_HPS, c, half), lambda h, i: (h, 0, 0)),
                  pl.BlockSpec((RET_HPS, c, half), lambda h, i: (h, 0, 0)),
                  pl.BlockSpec((1, wv), lambda h, i: (0, h))],
        out_specs=pl.BlockSpec((c, wv), lambda h, i: (i, h)),
        out_shape=jax.ShapeDtypeStruct((s, RET_HEADS * RET_DV), BF16),
        scratch_shapes=[pltpu.VMEM((RET_HPS, RET_DK, RET_DV), F32)],
        compiler_params=_params(2, 32),
        name="retention",
    )(cdec, proj, proj, proj, proj, cos, sin, dec, qd, kd, ret_gain.reshape(1, -1))


def _branch_kernel(p_ref, r_ref, ap_ref, ar_ref, wp_ref, wr_ref, o_ref):
    bp = _dot(p_ref[...], wp_ref[...])
    br = _dot(r_ref[...], wr_ref[...])
    ap = ap_ref[...].astype(F32)
    ar = ar_ref[...].astype(F32)
    o_ref[...] = (_sigmoid(ap) * bp + _sigmoid(ar) * br).astype(BF16)


def _branch(pool_out, ret_out, proj, wp, wr):
    s = pool_out.shape[0]
    d = wp.shape[1]
    tm, tn = 512, 512
    ap0, ar0 = COL_APOOL // tn, COL_ARET // tn
    return pl.pallas_call(
        _branch_kernel,
        grid=(s // tm, d // tn),
        in_specs=[pl.BlockSpec((tm, pool_out.shape[1]), lambda i, j: (i, 0)),
                  pl.BlockSpec((tm, ret_out.shape[1]), lambda i, j: (i, 0)),
                  pl.BlockSpec((tm, tn), lambda i, j: (i, ap0 + j)),
                  pl.BlockSpec((tm, tn), lambda i, j: (i, ar0 + j)),
                  pl.BlockSpec((wp.shape[0], tn), lambda i, j: (0, j)),
                  pl.BlockSpec((wr.shape[0], tn), lambda i, j: (0, j))],
        out_specs=pl.BlockSpec((tm, tn), lambda i, j: (i, j)),
        out_shape=jax.ShapeDtypeStruct((s, d), BF16),
        compiler_params=_params(2, 40),
        name="branch",
    )(pool_out, ret_out, proj, proj, wp.astype(BF16), wr.astype(BF16))


def _out_kernel(m_ref, w_ref, x_ref, mod_ref, gain_ref, x1_ref, fnt_ref):
    x1 = x_ref[...] + mod_ref[2:3, :] * _dot(m_ref[...], w_ref[...])
    x1_ref[...] = x1
    ms = jnp.mean(x1 * x1, axis=-1, keepdims=True)
    fn = x1 * lax.rsqrt(ms + EPS) * (gain_ref[...] * (1.0 + mod_ref[4:5, :])) + mod_ref[3:4, :]
    fnt_ref[...] = fn.T.astype(BF16)


def _out_proj(merged, w_bf16, x2d, mod, gain):
    s, d = x2d.shape
    tm = 256
    return pl.pallas_call(
        _out_kernel,
        grid=(s // tm,),
        in_specs=[pl.BlockSpec((tm, d), lambda i: (i, 0)),
                  pl.BlockSpec((d, d), lambda i: (0, 0)),
                  pl.BlockSpec((tm, d), lambda i: (i, 0)),
                  pl.BlockSpec((6, d), lambda i: (0, 0)),
                  pl.BlockSpec((1, d), lambda i: (0, 0))],
        out_specs=[pl.BlockSpec((tm, d), lambda i: (i, 0)),
                   pl.BlockSpec((d, tm), lambda i: (0, i))],
        out_shape=[jax.ShapeDtypeStruct((s, d), F32), jax.ShapeDtypeStruct((d, s), BF16)],
        compiler_params=_params(1, 48),
        name="out_proj",
    )(merged, w_bf16, x2d, mod, gain.reshape(1, d))


def _pscore_kernel(fnt_ref, wqt_ref, keys_ref, s_ref):
    t = fnt_ref.shape[1]
    qt = _dot(wqt_ref[...], fnt_ref[...]).astype(BF16)
    for hp in range(2 * PEER_HEADS):
        sc = _dot(keys_ref[hp], qt[hp * PEER_HALF:(hp + 1) * PEER_HALF, :])
        for lc in range(t // LANES):
            s_ref[hp, lc] = sc[:, lc * LANES:(lc + 1) * LANES]


def _peer_scores(fnt, wqt_bf16, keys_bf16):
    d, s = fnt.shape
    t = 512
    nq = wqt_bf16.shape[0]
    return pl.pallas_call(
        _pscore_kernel,
        grid=(s // t,),
        in_specs=[pl.BlockSpec((d, t), lambda i: (0, i)),
                  pl.BlockSpec((nq, d), lambda i: (0, 0)),
                  pl.BlockSpec((2 * PEER_HEADS, PEER_NKEYS, PEER_HALF), lambda i: (0, 0, 0))],
        out_specs=pl.BlockSpec((2 * PEER_HEADS, t // LANES, PEER_NKEYS, LANES), lambda i: (0, i, 0, 0)),
        out_shape=jax.ShapeDtypeStruct((2 * PEER_HEADS, s // LANES, PEER_NKEYS, LANES), F32),
        compiler_params=_params(1, 48),
        name="peer_scores",
    )(fnt, wqt_bf16, keys_bf16)


SUBLANES = 8
LOG2E = 1.4426950408889634


def _sort_network(n):
    size = 1
    while size < n:
        size *= 2

    def merge(lo, hi, r):
        step = r * 2
        if step < hi - lo:
            yield from merge(lo, hi, step)
            yield from merge(lo + r, hi, step)
            for i in range(lo + r, hi - r, step):
                yield (i, i + r)
        else:
            yield (lo, lo + r)

    def sort(lo, hi):
        if hi - lo >= 1:
            mid = lo + (hi - lo) // 2
            yield from sort(lo, mid)
            yield from sort(mid + 1, hi)
            yield from merge(lo, hi, 1)

    return [(i, j) for i, j in sort(0, size - 1) if j < n]


def _top_values(slabs, k):
    cols = list(slabs)
    for i, j in _sort_network(len(cols)):
        hi = jnp.maximum(cols[i], cols[j])
        cols[j] = jnp.minimum(cols[i], cols[j])
        cols[i] = hi
    vals = []
    for r in range(k):
        m = jnp.max(cols[0], axis=0, keepdims=True)
        vals.append(m)
        depth = min(len(cols), k - r)
        if r == k - 1:
            break
        hit = cols[0] == m
        for q in range(depth - 1):
            cols[q] = jnp.where(hit, cols[q + 1], cols[q])
        if depth == len(cols):
            cols[depth - 1] = jnp.where(hit, -jnp.inf, cols[depth - 1])
    return vals


def _rows_to_slabs(rows, n_slabs, row_id):
    slabs = []
    for g in range(n_slabs):
        slab = jnp.full(row_id.shape, -jnp.inf, F32)
        for q in range(SUBLANES):
            r = g * SUBLANES + q
            if r < len(rows):
                slab = jnp.where(row_id == q, rows[r], slab)
        slabs.append(slab)
    return slabs


PEER_ETILE = 512
PEER_IB = PEER_ETILE // PEER_NKEYS


PEER_NTOP = PEER_TOPK + 1


def _ptopk_kernel(s_ref, alpha_ref, beta_ref, tau_ref):
    n_lc = s_ref.shape[1]
    n_slabs = PEER_NKEYS // SUBLANES
    row_id = lax.broadcasted_iota(jnp.int32, (SUBLANES, LANES), 0)

    def body(lc, carry):
        sl = pl.ds(pl.multiple_of(lc * LANES, LANES), LANES)
        s1 = s_ref[0, lc]
        s2 = s_ref[1, lc]
        v1 = _top_values([s1[g * SUBLANES:(g + 1) * SUBLANES, :] for g in range(n_slabs)], PEER_NTOP)
        v2 = _top_values([s2[g * SUBLANES:(g + 1) * SUBLANES, :] for g in range(n_slabs)], PEER_NTOP)
        n_vs = -(-PEER_NTOP // SUBLANES)
        v1s = _rows_to_slabs(v1, n_vs, row_id)
        v2s = _rows_to_slabs(v2, n_vs, row_id)
        cand = [v1[0] + slab for slab in v2s]
        for r1 in range(1, SUBLANES):
            n_ok = PEER_NTOP // (r1 + 1)
            pair = v1[r1] + v2s[0]
            cand.append(pair if n_ok >= SUBLANES else jnp.where(row_id < n_ok, pair, -jnp.inf))
        for slab in v1s[1:]:
            cand.append(slab + v2[0])
        best = _top_values(cand, PEER_NTOP)
        z = jnp.ones_like(best[0])
        for r in range(1, PEER_TOPK):
            z = z + jnp.exp(best[r] - best[0])
        log_norm = best[0] + jnp.log(z)
        alpha = (s1 - log_norm) * LOG2E
        for g in range(PEER_NKEYS // PEER_IB):
            alpha_ref[g, :, sl] = alpha[g * PEER_IB:(g + 1) * PEER_IB, :]
        beta_ref[lc] = s2 * LOG2E
        tau_ref[:, sl] = (0.5 * (best[PEER_TOPK - 1] + best[PEER_TOPK]) - log_norm) * LOG2E
        return carry

    lax.fori_loop(0, n_lc, body, 0, unroll=4)


def _peer_topk(scores):
    hp, n_chunks, nk, _ = scores.shape
    s = n_chunks * LANES
    t = 512
    ng = nk // PEER_IB
    return pl.pallas_call(
        _ptopk_kernel,
        grid=(s // t, PEER_HEADS),
        in_specs=[pl.BlockSpec((2, t // LANES, nk, LANES), lambda i, h: (h, i, 0, 0))],
        out_specs=[pl.BlockSpec((None, ng, PEER_IB, t), lambda i, h: (h, 0, 0, i)),
                   pl.BlockSpec((None, t // LANES, nk, LANES), lambda i, h: (h, i, 0, 0)),
                   pl.BlockSpec((None, 1, t), lambda i, h: (h, 0, i))],
        out_shape=[jax.ShapeDtypeStruct((PEER_HEADS, ng, PEER_IB, s), F32),
                   jax.ShapeDtypeStruct((PEER_HEADS, n_chunks, nk, LANES), F32),
                   jax.ShapeDtypeStruct((PEER_HEADS, 1, s), F32)],
        compiler_params=_params(2, 32),
        name="peer_topk",
    )(scores)


MXU_COLS = 256
Z_ROWS = 128
O_ROWS = 256
GATE_ROWS = 16


def _gelu_tanh(x):
    return 0.5 * x * (1.0 + jnp.tanh(0.7978845608028654 * (x + 0.044715 * (x * x * x))))


def _pdense_step(fnt_ref, u_ref, vt_ref, alpha_ref, beta_ref, tau_ref, o_ref,
                 z_prev_ref, z_next_ref, a_prev_ref, a_next_ref):
    t = fnt_ref.shape[1]
    n_lc = t // LANES
    d_model = vt_ref.shape[0]
    n_exp = u_ref.shape[0]
    mxu_chunks = []
    for c0 in range(0, t, MXU_COLS):
        cols = slice(c0, c0 + MXU_COLS)
        zs = [("z", cols, slice(r0, r0 + Z_ROWS)) for r0 in range(0, n_exp, Z_ROWS)]
        os_ = [("o", cols, slice(r0, r0 + O_ROWS)) for r0 in range(0, d_model, O_ROWS)]
        per_z = len(os_) // len(zs)
        for k, zp in enumerate(zs):
            mxu_chunks.append(zp)
            mxu_chunks.extend(os_[k * per_z:(k + 1) * per_z])
    n_jb = PEER_NKEYS // GATE_ROWS
    n_blocks = n_lc * n_jb
    place = {}
    for k, chunk in enumerate(mxu_chunks):
        place.setdefault(k * n_blocks // len(mxu_chunks), []).append(chunk)
    for lc in range(n_lc):
        sl = slice(lc * LANES, (lc + 1) * LANES)
        for jb in range(n_jb):
            for kind, cols, mrows in place.get(lc * n_jb + jb, ()):
                slabs = range(cols.start // LANES, cols.stop // LANES)
                if kind == "z":
                    zc = _dot(u_ref[mrows, :], fnt_ref[:, cols])
                    for k, slab in enumerate(slabs):
                        z_next_ref[slab, mrows, :] = zc[:, k * LANES:(k + 1) * LANES]
                else:
                    ac = jnp.concatenate([a_prev_ref[slab] for slab in slabs], axis=1)
                    o_ref[mrows, cols] += _dot(vt_ref[mrows, :], ac)
            jrows = slice(jb * GATE_ROWS, (jb + 1) * GATE_ROWS)
            accs = [jnp.zeros((GATE_ROWS, LANES), F32) for _ in range(PEER_IB)]
            for h in range(PEER_HEADS):
                beta = beta_ref[h, lc, jrows, :]
                tau = tau_ref[h, :, sl]
                for il in range(PEER_IB):
                    lg = alpha_ref[h, il:il + 1, sl] + beta
                    accs[il] = accs[il] + jnp.where(lg >= tau, jnp.exp2(lg), 0.0)
            for il in range(PEER_IB):
                rows = slice(il * PEER_NKEYS + jb * GATE_ROWS, il * PEER_NKEYS + (jb + 1) * GATE_ROWS)
                a_next_ref[lc, rows, :] = (accs[il] * _gelu_tanh(z_prev_ref[lc, rows, :])).astype(BF16)


def _pdense_kernel(fnt_ref, u_ref, vt_ref, alpha_ref, beta_ref, tau_ref, x1_ref, mod_ref, gain_ref, y_ref,
                   z0_ref, z1_ref, a0_ref, a1_ref, acc_ref, *, n_et, apply_norm):
    s = pl.program_id(0)
    ins = (fnt_ref, u_ref, vt_ref, alpha_ref, beta_ref, tau_ref, acc_ref)
    e_out = (s + n_et - 2) % n_et

    @pl.when(s == 0)
    def _():
        z1_ref[...] = jnp.zeros_like(z1_ref)
        a1_ref[...] = jnp.zeros_like(a1_ref)

    @pl.when((s == 0) | (e_out == 0))
    def _():
        acc_ref[...] = jnp.zeros_like(acc_ref)

    @pl.when(s % 2 == 0)
    def _():
        _pdense_step(*ins, z1_ref, z0_ref, a1_ref, a0_ref)

    @pl.when(s % 2 == 1)
    def _():
        _pdense_step(*ins, z0_ref, z1_ref, a0_ref, a1_ref)

    @pl.when((s >= 2) & (e_out == n_et - 1))
    def _():
        x2 = x1_ref[...] + mod_ref[5:6, :] * acc_ref[...].T
        if apply_norm:
            ms = jnp.mean(x2 * x2, axis=-1, keepdims=True)
            x2 = x2 * lax.rsqrt(ms + EPS) * gain_ref[...]
        y_ref[...] = x2


def _peer_dense(fnt, u_bf16, vt_bf16, alpha, beta, tau, x1, mod, gain, apply_norm):
    d, s = fnt.shape
    ne = u_bf16.shape[0]
    t = 512
    et = PEER_ETILE
    n_et = ne // et
    assert n_et % 2 == 0
    nk = beta.shape[2]
    n_lc = t // LANES
    n_pairs = (s // t) * n_et

    def pair(step, lag):
        p = jnp.clip(step - lag, 0, n_pairs - 1)
        return p // n_et, p % n_et

    return pl.pallas_call(
        functools.partial(_pdense_kernel, n_et=n_et, apply_norm=apply_norm),
        grid=(n_pairs + 2,),
        in_specs=[pl.BlockSpec((d, t), lambda p: (0, pair(p, 0)[0])),
                  pl.BlockSpec((et, d), lambda p: (pair(p, 0)[1], 0)),
                  pl.BlockSpec((None, d, et), lambda p: (pair(p, 2)[1], 0, 0)),
                  pl.BlockSpec((PEER_HEADS, None, PEER_IB, t), lambda p: (0, pair(p, 1)[1], 0, pair(p, 1)[0])),
                  pl.BlockSpec((PEER_HEADS, n_lc, nk, LANES), lambda p: (0, pair(p, 1)[0], 0, 0)),
                  pl.BlockSpec((PEER_HEADS, 1, t), lambda p: (0, 0, pair(p, 1)[0])),
                  pl.BlockSpec((t, d), lambda p: (pair(p, 2)[0], 0)),
                  pl.BlockSpec((6, d), lambda p: (0, 0)),
                  pl.BlockSpec((1, d), lambda p: (0, 0))],
        out_specs=pl.BlockSpec((t, d), lambda p: (pair(p, 2)[0], 0)),
        out_shape=jax.ShapeDtypeStruct((s, d), F32),
        scratch_shapes=[pltpu.VMEM((n_lc, et, LANES), F32), pltpu.VMEM((n_lc, et, LANES), F32),
                        pltpu.VMEM((n_lc, et, LANES), BF16), pltpu.VMEM((n_lc, et, LANES), BF16),
                        pltpu.VMEM((d, t), F32)],
        compiler_params=_params(1, 56),
        name="peer_dense",
    )(fnt, u_bf16, vt_bf16, alpha, beta, tau, x1, mod, gain.reshape(1, d))


def kernel(x, c, positions, norm_mix_gain, w_ada, b_ada, w_in, pool_w, pool_scale, ret_norm_gain,
           w_branch_pool, w_branch_ret, w_out, norm_ffn_gain, peer_w_query, peer_sub_keys, peer_u, peer_v,
           final_norm_gain):
    batch, s, d = x.shape
    assert batch == 1 and d == D_MODEL and s % 1024 == 0
    depth = w_in.shape[0]
    xs = x.reshape(s, d)
    cos, sin = _rope_tables(positions.reshape(s))
    for l in range(depth):
        mod = _ada(c, w_ada[l], b_ada[l])
        proj = _in_proj(xs, norm_mix_gain[l], mod, w_in[l])
        pool_out = _pool(proj, pool_w[l].astype(BF16), pool_scale[l])
        ret_out = _retention(proj, cos, sin, ret_norm_gain[l])
        merged = _branch(pool_out, ret_out, proj, w_branch_pool[l], w_branch_ret[l])
        x1, fnt = _out_proj(merged, w_out[l].astype(BF16), xs, mod, norm_ffn_gain[l])
        keys = peer_sub_keys[l].reshape(2 * PEER_HEADS, PEER_NKEYS, PEER_HALF).astype(BF16)
        scores = _peer_scores(fnt, peer_w_query[l].T.astype(BF16), keys)
        alpha, beta, tau = _peer_topk(scores)
        vt_tiles = peer_v[l].astype(BF16).reshape(N_EXPERTS // PEER_ETILE, PEER_ETILE, d).transpose(0, 2, 1)
        xs = _peer_dense(fnt, peer_u[l].astype(BF16), vt_tiles, alpha, beta, tau, x1, mod, final_norm_gain,
                         apply_norm=(l == depth - 1))
    return xs.reshape(batch, s, d)
```

```python
import functools

import jax
import jax.numpy as jnp
from jax import lax
from jax.experimental import pallas as pl
from jax.experimental.pallas import tpu as pltpu

F32 = jnp.float32
BF16 = jnp.bfloat16

D_MODEL = 2048
EPS = 1e-6
POOL_GROUPS = 4
POOL_GDIM = 512
RET_HEADS = 8
RET_DK = 256
RET_DV = 512
ROPE_BASE = 10000.0
PEER_HEADS = 8
PEER_NKEYS = 128
PEER_HALF = 128
PEER_TOPK = 16
N_EXPERTS = PEER_NKEYS * PEER_NKEYS

COL_POOL, COL_Q, COL_K, COL_V, COL_G, COL_APOOL, COL_ARET = 0, 2048, 4096, 6144, 10240, 14336, 16384
IN_COLS = 18432

LANES = 128
MIB = 1024 * 1024

RET_CHUNK = 256


def _params(n_axes, vmem_mib, flags=None):
    return pltpu.CompilerParams(dimension_semantics=("arbitrary",) * n_axes,
                                vmem_limit_bytes=vmem_mib * MIB, flags=flags)


def _dot(a, b):
    return jnp.dot(a, b, preferred_element_type=F32)


def _sigmoid(x):
    return 1.0 / (1.0 + jnp.exp(-x))


def _ada_kernel(c_ref, w_ref, b_ref, o_ref):
    @pl.when(pl.program_id(0) == 0)
    def _():
        o_ref[...] = b_ref[...]

    c = c_ref[...]
    cond = c * _sigmoid(c)
    o_ref[...] += jnp.sum(cond * w_ref[...], axis=0, keepdims=True)


def _ada(c, w, b):
    d, n = w.shape
    tk = 256
    out = pl.pallas_call(
        _ada_kernel,
        grid=(d // tk,),
        in_specs=[pl.BlockSpec((tk, 1), lambda k: (k, 0)),
                  pl.BlockSpec((tk, n), lambda k: (k, 0)),
                  pl.BlockSpec((1, n), lambda k: (0, 0))],
        out_specs=pl.BlockSpec((1, n), lambda k: (0, 0)),
        out_shape=jax.ShapeDtypeStruct((1, n), F32),
        compiler_params=_params(1, 40),
        name="ada",
    )(c.reshape(d, 1), w, b.reshape(1, n))
    return out.reshape(6, d)


def _in_kernel(x_ref, gain_ref, mod_ref, w_ref, o_ref, hn_ref):
    tm = x_ref.shape[0]
    rc = 256

    @pl.when(pl.program_id(1) == 0)
    def _():
        scale = gain_ref[...] * (1.0 + mod_ref[1:2, :])
        shift = mod_ref[0:1, :]

        def body(r, carry):
            rows = pl.ds(pl.multiple_of(r * rc, rc), rc)
            x = x_ref[rows, :]
            ms = jnp.mean(x * x, axis=-1, keepdims=True)
            hn_ref[rows, :] = (x * lax.rsqrt(ms + EPS) * scale + shift).astype(BF16)
            return carry

        lax.fori_loop(0, tm // rc, body, 0)

    o_ref[...] = _dot(hn_ref[...], w_ref[...].astype(BF16)).astype(BF16)


def _in_proj(x2d, gain, mod, w):
    s, d = x2d.shape
    n = w.shape[1]
    tm, tn = 1024, 1024
    return pl.pallas_call(
        _in_kernel,
        grid=(s // tm, n // tn),
        in_specs=[pl.BlockSpec((tm, d), lambda i, j: (i, 0)),
                  pl.BlockSpec((1, d), lambda i, j: (0, 0)),
                  pl.BlockSpec((6, d), lambda i, j: (0, 0)),
                  pl.BlockSpec((d, tn), lambda i, j: (0, j))],
        out_specs=pl.BlockSpec((tm, tn), lambda i, j: (i, j)),
        out_shape=jax.ShapeDtypeStruct((s, n), BF16),
        scratch_shapes=[pltpu.VMEM((tm, d), BF16)],
        compiler_params=_params(2, 54),
        name="in_proj",
    )(x2d, gain.reshape(1, d), mod, w)


def _rope_kernel(pos_ref, inv_ref, cos_ref, sin_ref):
    ang = pos_ref[...].astype(F32) * inv_ref[...]
    cos_ref[...] = jnp.cos(ang)
    sin_ref[...] = jnp.sin(ang)


def _rope_tables(positions):
    s = positions.shape[0]
    half = RET_DK // 2
    inv_freq = (ROPE_BASE ** (-jnp.arange(0, RET_DK, 2, dtype=F32) / RET_DK)).reshape(1, half)
    tm = 1024
    return pl.pallas_call(
        _rope_kernel,
        grid=(s // tm,),
        in_specs=[pl.BlockSpec((tm, 1), lambda i: (i, 0)),
                  pl.BlockSpec((1, half), lambda i: (0, 0))],
        out_specs=[pl.BlockSpec((tm, half), lambda i: (i, 0)),
                   pl.BlockSpec((tm, half), lambda i: (i, 0))],
        out_shape=[jax.ShapeDtypeStruct((s, half), F32), jax.ShapeDtypeStruct((s, half), F32)],
        compiler_params=_params(1, 32),
        name="rope",
    )(positions.reshape(s, 1), inv_freq)


POOL_HALO = 16


def _pool_kernel(u_ref, halo_ref, pw_ref, ps_ref, o_ref):
    i = pl.program_id(0)
    g = pl.program_id(1)
    tm = u_ref.shape[0]
    w = jnp.left_shift(2, g)
    r = lax.broadcasted_iota(jnp.int32, (tm, tm), 0)
    c = lax.broadcasted_iota(jnp.int32, (tm, tm), 1)
    band = jnp.where(c <= r, jnp.where(c > r - w, 1.0, 0.0), 0.0).astype(BF16)
    rh = lax.broadcasted_iota(jnp.int32, (tm, POOL_HALO), 0)
    ch = lax.broadcasted_iota(jnp.int32, (tm, POOL_HALO), 1)
    halo_on = jnp.where(i > 0, 1.0, 0.0)
    bandh = (jnp.where(ch > rh + POOL_HALO - w, 1.0, 0.0) * halo_on).astype(BF16)
    u = u_ref[...]
    wsum = _dot(band, u) + _dot(bandh, halo_ref[...])
    t = i * tm + lax.broadcasted_iota(jnp.int32, (tm, 1), 0)
    cnt = jnp.minimum(t + 1, w).astype(F32)
    pooled = wsum / cnt - u.astype(F32)
    o_ref[...] = (_dot(pooled.astype(BF16), pw_ref[...]) * ps_ref[...]).astype(BF16)


def _pool(proj, pool_w_bf16, pool_scale):
    s = proj.shape[0]
    tm = 512
    hb = tm // POOL_HALO
    return pl.pallas_call(
        _pool_kernel,
        grid=(s // tm, POOL_GROUPS),
        in_specs=[pl.BlockSpec((tm, POOL_GDIM), lambda i, g: (i, g)),
                  pl.BlockSpec((POOL_HALO, POOL_GDIM), lambda i, g: (jnp.maximum(i * hb - 1, 0), g)),
                  pl.BlockSpec((None, POOL_GDIM, POOL_GDIM), lambda i, g: (g, 0, 0)),
                  pl.BlockSpec((1, POOL_GDIM), lambda i, g: (0, g))],
        out_specs=pl.BlockSpec((tm, POOL_GDIM), lambda i, g: (i, g)),
        out_shape=jax.ShapeDtypeStruct((s, POOL_GROUPS * POOL_GDIM), BF16),
        compiler_params=_params(2, 32),
        name="pool",
    )(proj, proj, pool_w_bf16, pool_scale.reshape(1, -1))


RET_HPS = 4


def _ret_kernel(cdec_ref, q_ref, k_ref, v_ref, g_ref, cos_ref, sin_ref, dec_ref, qd_ref, kd_ref,
                gain_ref, o_ref, state_ref):
    hp = pl.program_id(0)

    @pl.when(pl.program_id(1) == 0)
    def _():
        state_ref[...] = jnp.zeros_like(state_ref)

    cos = cos_ref[...]
    sin = sin_ref[...]
    half = RET_DK // 2

    def rot(t_ref, col0):
        t1 = t_ref[:, col0:col0 + half].astype(F32)
        t2 = t_ref[:, col0 + half:col0 + RET_DK].astype(F32)
        return t1 * cos - t2 * sin, t2 * cos + t1 * sin

    for j in range(RET_HPS):
        q1, q2 = rot(q_ref, j * RET_DK)
        k1, k2 = rot(k_ref, j * RET_DK)
        qd = qd_ref[j]
        kd = kd_ref[j]
        qb = jnp.concatenate([q1, q2], axis=1).astype(BF16)
        kb = jnp.concatenate([k1, k2], axis=1).astype(BF16)
        qdb = jnp.concatenate([q1 * qd, q2 * qd], axis=1).astype(BF16)
        kdb = jnp.concatenate([k1 * kd, k2 * kd], axis=1).astype(BF16)
        vcols = slice(j * RET_DV, (j + 1) * RET_DV)
        v = v_ref[:, vcols]
        scores = lax.dot_general(qb, kb, (((1,), (1,)), ((), ())), preferred_element_type=F32) * dec_ref[j]
        inner = _dot(scores.astype(BF16), v)
        state = state_ref[j]
        cross = _dot(qdb, state.astype(BF16))
        state_ref[j] = state * cdec_ref[hp * RET_HPS + j] + lax.dot_general(
            kdb, v, (((0,), (0,)), ((), ())), preferred_element_type=F32)
        y = inner + cross
        mu = jnp.mean(y, axis=-1, keepdims=True)
        yc = y - mu
        var = jnp.mean(yc * yc, axis=-1, keepdims=True)
        yn = yc * lax.rsqrt(var + EPS) * gain_ref[:, vcols]
        gate = g_ref[:, vcols].astype(F32)
        o_ref[:, vcols] = (gate * _sigmoid(gate) * yn).astype(BF16)


def _ret_tables():
    c = RET_CHUNK
    log_g = jnp.log(1.0 - 2.0 ** (-5.0 - jnp.arange(RET_HEADS, dtype=F32)))
    idx = jnp.arange(c, dtype=F32)
    diff = idx[:, None] - idx[None, :]
    kscale = RET_DK ** -0.5
    dec = jnp.where(diff[None] >= 0, jnp.exp(jnp.maximum(diff, 0.0)[None] * log_g[:, None, None]), 0.0) * kscale
    qd = jnp.exp((idx + 1.0)[None, :] * log_g[:, None])
    kd = jnp.exp((c - 1.0 - idx)[None, :] * log_g[:, None]) * kscale
    half = RET_DK // 2
    qd = jnp.broadcast_to(qd[:, :, None], (RET_HEADS, c, half))
    kd = jnp.broadcast_to(kd[:, :, None], (RET_HEADS, c, half))
    cdec = jnp.exp(c * log_g)
    return cdec, dec, qd, kd


def _retention(proj, cos, sin, ret_gain):
    s = proj.shape[0]
    c = RET_CHUNK
    half = RET_DK // 2
    cdec, dec, qd, kd = _ret_tables()
    wqk, wv = RET_HPS * RET_DK, RET_HPS * RET_DV
    qb0, kb0 = COL_Q // wqk, COL_K // wqk
    vb0, gb0 = COL_V // wv, COL_G // wv
    return pl.pallas_call(
        _ret_kernel,
        grid=(RET_HEADS // RET_HPS, s // c),
        in_specs=[pl.BlockSpec(memory_space=pltpu.SMEM),
                  pl.BlockSpec((c, wqk), lambda h, i: (i, qb0 + h)),
                  pl.BlockSpec((c, wqk), lambda h, i: (i, kb0 + h)),
                  pl.BlockSpec((c, wv), lambda h, i: (i, vb0 + h)),
                  pl.BlockSpec((c, wv), lambda h, i: (i, gb0 + h)),
                  pl.BlockSpec((c, half), lambda h, i: (i, 0)),
                  pl.BlockSpec((c, half), lambda h, i: (i, 0)),
                  pl.BlockSpec((RET_HPS, c, c), lambda h, i: (h, 0, 0)),
                  pl.BlockSpec((RET_HPS, c, half), lambda h, i: (h, 0, 0)),
                  pl.BlockSpec((RET_HPS, c, half), lambda h, i: (h, 0, 0)),
                  pl.BlockSpec((1, wv), lambda h, i: (0, h))],
        out_specs=pl.BlockSpec((c, wv), lambda h, i: (i, h)),
        out_shape=jax.ShapeDtypeStruct((s, RET_HEADS * RET_DV), BF16),
        scratch_shapes=[pltpu.VMEM((RET_HPS, RET_DK, RET_DV), F32)],
        compiler_params=_params(2, 32),
        name="retention",
    )(cdec, proj, proj, proj, proj, cos, sin, dec, qd, kd, ret_gain.reshape(1, -1))


def _branch_kernel(p_ref, r_ref, ap_ref, ar_ref, wp_ref, wr_ref, o_ref):
    bp = _dot(p_ref[...], wp_ref[...])
    br = _dot(r_ref[...], wr_ref[...])
    ap = ap_ref[...].astype(F32)
    ar = ar_ref[...].astype(F32)
    o_ref[...] = (_sigmoid(ap) * bp + _sigmoid(ar) * br).astype(BF16)


def _branch(pool_out, ret_out, proj, wp, wr):
    s = pool_out.shape[0]
    d = wp.shape[1]
    tm, tn = 512, 512
    ap0, ar0 = COL_APOOL // tn, COL_ARET // tn
    return pl.pallas_call(
        _branch_kernel,
        grid=(s // tm, d // tn),
        in_specs=[pl.BlockSpec((tm, pool_out.shape[1]), lambda i, j: (i, 0)),
                  pl.BlockSpec((tm, ret_out.shape[1]), lambda i, j: (i, 0)),
                  pl.BlockSpec((tm, tn), lambda i, j: (i, ap0 + j)),
                  pl.BlockSpec((tm, tn), lambda i, j: (i, ar0 + j)),
                  pl.BlockSpec((wp.shape[0], tn), lambda i, j: (0, j)),
                  pl.BlockSpec((wr.shape[0], tn), lambda i, j: (0, j))],
        out_specs=pl.BlockSpec((tm, tn), lambda i, j: (i, j)),
        out_shape=jax.ShapeDtypeStruct((s, d), BF16),
        compiler_params=_params(2, 40),
        name="branch",
    )(pool_out, ret_out, proj, proj, wp.astype(BF16), wr.astype(BF16))


def _out_kernel(m_ref, w_ref, x_ref, mod_ref, gain_ref, x1_ref, fnt_ref):
    x1 = x_ref[...] + mod_ref[2:3, :] * _dot(m_ref[...], w_ref[...])
    x1_ref[...] = x1
    ms = jnp.mean(x1 * x1, axis=-1, keepdims=True)
    fn = x1 * lax.rsqrt(ms + EPS) * (gain_ref[...] * (1.0 + mod_ref[4:5, :])) + mod_ref[3:4, :]
    fnt_ref[...] = fn.T.astype(BF16)


def _out_proj(merged, w_bf16, x2d, mod, gain):
    s, d = x2d.shape
    tm = 512
    return pl.pallas_call(
        _out_kernel,
        grid=(s // tm,),
        in_specs=[pl.BlockSpec((tm, d), lambda i: (i, 0)),
                  pl.BlockSpec((d, d), lambda i: (0, 0)),
                  pl.BlockSpec((tm, d), lambda i: (i, 0)),
                  pl.BlockSpec((6, d), lambda i: (0, 0)),
                  pl.BlockSpec((1, d), lambda i: (0, 0))],
        out_specs=[pl.BlockSpec((tm, d), lambda i: (i, 0)),
                   pl.BlockSpec((d, tm), lambda i: (0, i))],
        out_shape=[jax.ShapeDtypeStruct((s, d), F32), jax.ShapeDtypeStruct((d, s), BF16)],
        compiler_params=_params(1, 56),
        name="out_proj",
    )(merged, w_bf16, x2d, mod, gain.reshape(1, d))


def _pscore_kernel(fnt_ref, wqt_ref, keys_ref, s_ref):
    t = fnt_ref.shape[1]
    qt = _dot(wqt_ref[...], fnt_ref[...]).astype(BF16)
    for hp in range(2 * PEER_HEADS):
        sc = _dot(keys_ref[hp], qt[hp * PEER_HALF:(hp + 1) * PEER_HALF, :])
        for lc in range(t // LANES):
            s_ref[hp, lc] = sc[:, lc * LANES:(lc + 1) * LANES]


def _peer_scores(fnt, wqt_bf16, keys_bf16):
    d, s = fnt.shape
    t = 512
    nq = wqt_bf16.shape[0]
    return pl.pallas_call(
        _pscore_kernel,
        grid=(s // t,),
        in_specs=[pl.BlockSpec((d, t), lambda i: (0, i)),
                  pl.BlockSpec((nq, d), lambda i: (0, 0)),
                  pl.BlockSpec((2 * PEER_HEADS, PEER_NKEYS, PEER_HALF), lambda i: (0, 0, 0))],
        out_specs=pl.BlockSpec((2 * PEER_HEADS, t // LANES, PEER_NKEYS, LANES), lambda i: (0, i, 0, 0)),
        out_shape=jax.ShapeDtypeStruct((2 * PEER_HEADS, s // LANES, PEER_NKEYS, LANES), F32),
        compiler_params=_params(1, 48),
        name="peer_scores",
    )(fnt, wqt_bf16, keys_bf16)


SUBLANES = 8
LOG2E = 1.4426950408889634


def _sort_network(n):
    size = 1
    while size < n:
        size *= 2

    def merge(lo, hi, r):
        step = r * 2
        if step < hi - lo:
            yield from merge(lo, hi, step)
            yield from merge(lo + r, hi, step)
            for i in range(lo + r, hi - r, step):
                yield (i, i + r)
        else:
            yield (lo, lo + r)

    def sort(lo, hi):
        if hi - lo >= 1:
            mid = lo + (hi - lo) // 2
            yield from sort(lo, mid)
            yield from sort(mid + 1, hi)
            yield from merge(lo, hi, 1)

    return [(i, j) for i, j in sort(0, size - 1) if j < n]


def _top_values(slabs, k):
    cols = list(slabs)
    for i, j in _sort_network(len(cols)):
        hi = jnp.maximum(cols[i], cols[j])
        cols[j] = jnp.minimum(cols[i], cols[j])
        cols[i] = hi
    vals = []
    for r in range(k):
        m = jnp.max(cols[0], axis=0, keepdims=True)
        vals.append(m)
        depth = min(len(cols), k - r)
        if r == k - 1:
            break
        hit = cols[0] == m
        for q in range(depth - 1):
            cols[q] = jnp.where(hit, cols[q + 1], cols[q])
        if depth == len(cols):
            cols[depth - 1] = jnp.where(hit, -jnp.inf, cols[depth - 1])
    return vals


def _rows_to_slabs(rows, n_slabs, row_id):
    slabs = []
    for g in range(n_slabs):
        slab = jnp.full(row_id.shape, -jnp.inf, F32)
        for q in range(SUBLANES):
            r = g * SUBLANES + q
            if r < len(rows):
                slab = jnp.where(row_id == q, rows[r], slab)
        slabs.append(slab)
    return slabs


PEER_ETILE = 512
PEER_IB = PEER_ETILE // PEER_NKEYS


PEER_NTOP = PEER_TOPK + 1


def _ptopk_kernel(s_ref, alpha_ref, beta_ref, tau_ref):
    n_lc = s_ref.shape[1]
    n_slabs = PEER_NKEYS // SUBLANES
    row_id = lax.broadcasted_iota(jnp.int32, (SUBLANES, LANES), 0)

    def body(lc, carry):
        sl = pl.ds(pl.multiple_of(lc * LANES, LANES), LANES)
        s1 = s_ref[0, lc]
        s2 = s_ref[1, lc]
        v1 = _top_values([s1[g * SUBLANES:(g + 1) * SUBLANES, :] for g in range(n_slabs)], PEER_NTOP)
        v2 = _top_values([s2[g * SUBLANES:(g + 1) * SUBLANES, :] for g in range(n_slabs)], PEER_NTOP)
        n_vs = -(-PEER_NTOP // SUBLANES)
        v1s = _rows_to_slabs(v1, n_vs, row_id)
        v2s = _rows_to_slabs(v2, n_vs, row_id)
        cand = [v1[0] + slab for slab in v2s]
        for r1 in range(1, SUBLANES):
            n_ok = PEER_NTOP // (r1 + 1)
            pair = v1[r1] + v2s[0]
            cand.append(pair if n_ok >= SUBLANES else jnp.where(row_id < n_ok, pair, -jnp.inf))
        for slab in v1s[1:]:
            cand.append(slab + v2[0])
        best = _top_values(cand, PEER_NTOP)
        z = jnp.ones_like(best[0])
        for r in range(1, PEER_TOPK):
            z = z + jnp.exp(best[r] - best[0])
        log_norm = best[0] + jnp.log(z)
        alpha = (s1 - log_norm) * LOG2E
        for g in range(PEER_NKEYS // PEER_IB):
            alpha_ref[g, :, sl] = alpha[g * PEER_IB:(g + 1) * PEER_IB, :]
        beta_ref[lc] = s2 * LOG2E
        tau_ref[:, sl] = (0.5 * (best[PEER_TOPK - 1] + best[PEER_TOPK]) - log_norm) * LOG2E
        return carry

    lax.fori_loop(0, n_lc, body, 0, unroll=4)


def _peer_topk(scores):
    hp, n_chunks, nk, _ = scores.shape
    s = n_chunks * LANES
    t = 1024
    ng = nk // PEER_IB
    return pl.pallas_call(
        _ptopk_kernel,
        grid=(s // t, PEER_HEADS),
        in_specs=[pl.BlockSpec((2, t // LANES, nk, LANES), lambda i, h: (h, i, 0, 0))],
        out_specs=[pl.BlockSpec((None, ng, PEER_IB, t), lambda i, h: (h, 0, 0, i)),
                   pl.BlockSpec((None, t // LANES, nk, LANES), lambda i, h: (h, i, 0, 0)),
                   pl.BlockSpec((None, 1, t), lambda i, h: (h, 0, i))],
        out_shape=[jax.ShapeDtypeStruct((PEER_HEADS, ng, PEER_IB, s), F32),
                   jax.ShapeDtypeStruct((PEER_HEADS, n_chunks, nk, LANES), F32),
                   jax.ShapeDtypeStruct((PEER_HEADS, 1, s), F32)],
        compiler_params=_params(2, 32),
        name="peer_topk",
    )(scores)


MXU_COLS = 256
Z_ROWS = 128
O_ROWS = 256
GATE_ROWS = 16


def _gelu_tanh(x):
    return 0.5 * x * (1.0 + jnp.tanh(0.7978845608028654 * (x + 0.044715 * (x * x * x))))


def _pdense_step(fnt_ref, u_ref, vt_ref, alpha_ref, beta_ref, tau_ref, o_ref,
                 z_prev_ref, z_next_ref, a_prev_ref, a_next_ref):
    t = fnt_ref.shape[1]
    n_lc = t // LANES
    d_model = vt_ref.shape[0]
    n_exp = u_ref.shape[0]
    mxu_chunks = []
    for c0 in range(0, t, MXU_COLS):
        cols = slice(c0, c0 + MXU_COLS)
        zs = [("z", cols, slice(r0, r0 + Z_ROWS)) for r0 in range(0, n_exp, Z_ROWS)]
        os_ = [("o", cols, slice(r0, r0 + O_ROWS)) for r0 in range(0, d_model, O_ROWS)]
        per_z = len(os_) // len(zs)
        for k, zp in enumerate(zs):
            mxu_chunks.append(zp)
            mxu_chunks.extend(os_[k * per_z:(k + 1) * per_z])
    n_jb = PEER_NKEYS // GATE_ROWS
    n_blocks = n_lc * n_jb
    place = {}
    for k, chunk in enumerate(mxu_chunks):
        place.setdefault(k * n_blocks // len(mxu_chunks), []).append(chunk)
    for lc in range(n_lc):
        sl = slice(lc * LANES, (lc + 1) * LANES)
        for jb in range(n_jb):
            for kind, cols, mrows in place.get(lc * n_jb + jb, ()):
                slabs = range(cols.start // LANES, cols.stop // LANES)
                if kind == "z":
                    zc = _dot(u_ref[mrows, :], fnt_ref[:, cols])
                    for k, slab in enumerate(slabs):
                        z_next_ref[slab, mrows, :] = zc[:, k * LANES:(k + 1) * LANES]
                else:
                    ac = jnp.concatenate([a_prev_ref[slab] for slab in slabs], axis=1)
                    o_ref[mrows, cols] += _dot(vt_ref[mrows, :], ac)
            jrows = slice(jb * GATE_ROWS, (jb + 1) * GATE_ROWS)
            accs = [jnp.zeros((GATE_ROWS, LANES), F32) for _ in range(PEER_IB)]
            for h in range(PEER_HEADS):
                beta = beta_ref[h, lc, jrows, :]
                tau = tau_ref[h, :, sl]
                for il in range(PEER_IB):
                    lg = alpha_ref[h, il:il + 1, sl] + beta
                    accs[il] = accs[il] + jnp.where(lg >= tau, jnp.exp2(lg), 0.0)
            for il in range(PEER_IB):
                rows = slice(il * PEER_NKEYS + jb * GATE_ROWS, il * PEER_NKEYS + (jb + 1) * GATE_ROWS)
                a_next_ref[lc, rows, :] = (accs[il] * _gelu_tanh(z_prev_ref[lc, rows, :])).astype(BF16)


def _pdense_kernel(fnt_ref, u_ref, vt_ref, alpha_ref, beta_ref, tau_ref, x1_ref, mod_ref, gain_ref, y_ref,
                   z0_ref, z1_ref, a0_ref, a1_ref, acc_ref, *, n_et, apply_norm):
    s = pl.program_id(0)
    ins = (fnt_ref, u_ref, vt_ref, alpha_ref, beta_ref, tau_ref, acc_ref)
    e_out = (s + n_et - 2) % n_et

    @pl.when(s == 0)
    def _():
        z1_ref[...] = jnp.zeros_like(z1_ref)
        a1_ref[...] = jnp.zeros_like(a1_ref)

    @pl.when((s == 0) | (e_out == 0))
    def _():
        acc_ref[...] = jnp.zeros_like(acc_ref)

    @pl.when(s % 2 == 0)
    def _():
        _pdense_step(*ins, z1_ref, z0_ref, a1_ref, a0_ref)

    @pl.when(s % 2 == 1)
    def _():
        _pdense_step(*ins, z0_ref, z1_ref, a0_ref, a1_ref)

    @pl.when((s >= 2) & (e_out == n_et - 1))
    def _():
        x2 = x1_ref[...] + mod_ref[5:6, :] * acc_ref[...].T
        if apply_norm:
            ms = jnp.mean(x2 * x2, axis=-1, keepdims=True)
            x2 = x2 * lax.rsqrt(ms + EPS) * gain_ref[...]
        y_ref[...] = x2


def _peer_dense(fnt, u_bf16, vt_bf16, alpha, beta, tau, x1, mod, gain, apply_norm):
    d, s = fnt.shape
    ne = u_bf16.shape[0]
    t = 512
    et = PEER_ETILE
    n_et = ne // et
    assert n_et % 2 == 0
    nk = beta.shape[2]
    n_lc = t // LANES
    n_pairs = (s // t) * n_et

    def pair(step, lag):
        p = jnp.clip(step - lag, 0, n_pairs - 1)
        return p // n_et, p % n_et

    return pl.pallas_call(
        functools.partial(_pdense_kernel, n_et=n_et, apply_norm=apply_norm),
        grid=(n_pairs + 2,),
        in_specs=[pl.BlockSpec((d, t), lambda p: (0, pair(p, 0)[0])),
                  pl.BlockSpec((et, d), lambda p: (pair(p, 0)[1], 0)),
                  pl.BlockSpec((None, d, et), lambda p: (pair(p, 2)[1], 0, 0)),
                  pl.BlockSpec((PEER_HEADS, None, PEER_IB, t), lambda p: (0, pair(p, 1)[1], 0, pair(p, 1)[0])),
                  pl.BlockSpec((PEER_HEADS, n_lc, nk, LANES), lambda p: (0, pair(p, 1)[0], 0, 0)),
                  pl.BlockSpec((PEER_HEADS, 1, t), lambda p: (0, 0, pair(p, 1)[0])),
                  pl.BlockSpec((t, d), lambda p: (pair(p, 2)[0], 0)),
                  pl.BlockSpec((6, d), lambda p: (0, 0)),
                  pl.BlockSpec((1, d), lambda p: (0, 0))],
        out_specs=pl.BlockSpec((t, d), lambda p: (pair(p, 2)[0], 0)),
        out_shape=jax.ShapeDtypeStruct((s, d), F32),
        scratch_shapes=[pltpu.VMEM((n_lc, et, LANES), F32), pltpu.VMEM((n_lc, et, LANES), F32),
                        pltpu.VMEM((n_lc, et, LANES), BF16), pltpu.VMEM((n_lc, et, LANES), BF16),
                        pltpu.VMEM((d, t), F32)],
        compiler_params=_params(1, 56),
        name="peer_dense",
    )(fnt, u_bf16, vt_bf16, alpha, beta, tau, x1, mod, gain.reshape(1, d))


def kernel(x, c, positions, norm_mix_gain, w_ada, b_ada, w_in, pool_w, pool_scale, ret_norm_gain,
           w_branch_pool, w_branch_ret, w_out, norm_ffn_gain, peer_w_query, peer_sub_keys, peer_u, peer_v,
           final_norm_gain):
    batch, s, d = x.shape
    assert batch == 1 and d == D_MODEL and s % 1024 == 0
    depth = w_in.shape[0]
    xs = x.reshape(s, d)
    cos, sin = _rope_tables(positions.reshape(s))
    for l in range(depth):
        mod = _ada(c, w_ada[l], b_ada[l])
        proj = _in_proj(xs, norm_mix_gain[l], mod, w_in[l])
        pool_out = _pool(proj, pool_w[l].astype(BF16), pool_scale[l])
        ret_out = _retention(proj, cos, sin, ret_norm_gain[l])
        merged = _branch(pool_out, ret_out, proj, w_branch_pool[l], w_branch_ret[l])
        x1, fnt = _out_proj(merged, w_out[l].astype(BF16), xs, mod, norm_ffn_gain[l])
        keys = peer_sub_keys[l].reshape(2 * PEER_HEADS, PEER_NKEYS, PEER_HALF).astype(BF16)
        scores = _peer_scores(fnt, peer_w_query[l].T.astype(BF16), keys)
        alpha, beta, tau = _peer_topk(scores)
        vt_tiles = peer_v[l].astype(BF16).reshape(N_EXPERTS // PEER_ETILE, PEER_ETILE, d).transpose(0, 2, 1)
        xs = _peer_dense(fnt, peer_u[l].astype(BF16), vt_tiles, alpha, beta, tau, x1, mod, final_norm_gain,
                         apply_norm=(l == depth - 1))
    return xs.reshape(batch, s, d)
```

```python
import functools

import jax
import jax.numpy as jnp
from jax import lax
from jax.experimental import pallas as pl
from jax.experimental.pallas import tpu as pltpu

F32 = jnp.float32
BF16 = jnp.bfloat16

D_MODEL = 2048
EPS = 1e-6
POOL_GROUPS = 4
POOL_GDIM = 512
RET_HEADS = 8
RET_DK = 256
RET_DV = 512
ROPE_BASE = 10000.0
PEER_HEADS = 8
PEER_NKEYS = 128
PEER_HALF = 128
PEER_TOPK = 16
N_EXPERTS = PEER_NKEYS * PEER_NKEYS

COL_POOL, COL_Q, COL_K, COL_V, COL_G, COL_APOOL, COL_ARET = 0, 2048, 4096, 6144, 10240, 14336, 16384
IN_COLS = 18432

LANES = 128
MIB = 1024 * 1024

RET_CHUNK = 256


def _params(n_axes, vmem_mib, flags=None):
    return pltpu.CompilerParams(dimension_semantics=("arbitrary",) * n_axes,
                                vmem_limit_bytes=vmem_mib * MIB, flags=flags)


def _dot(a, b):
    return jnp.dot(a, b, preferred_element_type=F32)


def _sigmoid(x):
    return 1.0 / (1.0 + jnp.exp(-x))


def _ada_kernel(c_ref, w_ref, b_ref, o_ref):
    @pl.when(pl.program_id(0) == 0)
    def _():
        o_ref[...] = b_ref[...]

    c = c_ref[...]
    cond = c * _sigmoid(c)
    o_ref[...] += jnp.sum(cond * w_ref[...], axis=0, keepdims=True)


def _ada(c, w, b):
    d, n = w.shape
    tk = 256
    out = pl.pallas_call(
        _ada_kernel,
        grid=(d // tk,),
        in_specs=[pl.BlockSpec((tk, 1), lambda k: (k, 0)),
                  pl.BlockSpec((tk, n), lambda k: (k, 0)),
                  pl.BlockSpec((1, n), lambda k: (0, 0))],
        out_specs=pl.BlockSpec((1, n), lambda k: (0, 0)),
        out_shape=jax.ShapeDtypeStruct((1, n), F32),
        compiler_params=_params(1, 40),
        name="ada",
    )(c.reshape(d, 1), w, b.reshape(1, n))
    return out.reshape(6, d)


def _in_kernel(x_ref, gain_ref, mod_ref, w_ref, o_ref, hn_ref):
    tm = x_ref.shape[0]
    rc = 256

    @pl.when(pl.program_id(1) == 0)
    def _():
        scale = gain_ref[...] * (1.0 + mod_ref[1:2, :])
        shift = mod_ref[0:1, :]

        def body(r, carry):
            rows = pl.ds(pl.multiple_of(r * rc, rc), rc)
            x = x_ref[rows, :]
            ms = jnp.mean(x * x, axis=-1, keepdims=True)
            hn_ref[rows, :] = (x * lax.rsqrt(ms + EPS) * scale + shift).astype(BF16)
            return carry

        lax.fori_loop(0, tm // rc, body, 0)

    o_ref[...] = _dot(hn_ref[...], w_ref[...].astype(BF16)).astype(BF16)


def _in_proj(x2d, gain, mod, w):
    s, d = x2d.shape
    n = w.shape[1]
    tm, tn = 1024, 1024
    return pl.pallas_call(
        _in_kernel,
        grid=(s // tm, n // tn),
        in_specs=[pl.BlockSpec((tm, d), lambda i, j: (i, 0)),
                  pl.BlockSpec((1, d), lambda i, j: (0, 0)),
                  pl.BlockSpec((6, d), lambda i, j: (0, 0)),
                  pl.BlockSpec((d, tn), lambda i, j: (0, j))],
        out_specs=pl.BlockSpec((tm, tn), lambda i, j: (i, j)),
        out_shape=jax.ShapeDtypeStruct((s, n), BF16),
        scratch_shapes=[pltpu.VMEM((tm, d), BF16)],
        compiler_params=_params(2, 54),
        name="in_proj",
    )(x2d, gain.reshape(1, d), mod, w)


def _rope_kernel(pos_ref, inv_ref, cos_ref, sin_ref):
    ang = pos_ref[...].astype(F32) * inv_ref[...]
    cos_ref[...] = jnp.cos(ang)
    sin_ref[...] = jnp.sin(ang)


def _rope_tables(positions):
    s = positions.shape[0]
    half = RET_DK // 2
    inv_freq = (ROPE_BASE ** (-jnp.arange(0, RET_DK, 2, dtype=F32) / RET_DK)).reshape(1, half)
    tm = 1024
    return pl.pallas_call(
        _rope_kernel,
        grid=(s // tm,),
        in_specs=[pl.BlockSpec((tm, 1), lambda i: (i, 0)),
                  pl.BlockSpec((1, half), lambda i: (0, 0))],
        out_specs=[pl.BlockSpec((tm, half), lambda i: (i, 0)),
                   pl.BlockSpec((tm, half), lambda i: (i, 0))],
        out_shape=[jax.ShapeDtypeStruct((s, half), F32), jax.ShapeDtypeStruct((s, half), F32)],
        compiler_params=_params(1, 32),
        name="rope",
    )(positions.reshape(s, 1), inv_freq)


POOL_HALO = 16


def _pool_kernel(u_ref, halo_ref, pw_ref, ps_ref, o_ref):
    i = pl.program_id(0)
    g = pl.program_id(1)
    tm = u_ref.shape[0]
    w = jnp.left_shift(2, g)
    r = lax.broadcasted_iota(jnp.int32, (tm, tm), 0)
    c = lax.broadcasted_iota(jnp.int32, (tm, tm), 1)
    band = jnp.where(c <= r, jnp.where(c > r - w, 1.0, 0.0), 0.0).astype(BF16)
    rh = lax.broadcasted_iota(jnp.int32, (tm, POOL_HALO), 0)
    ch = lax.broadcasted_iota(jnp.int32, (tm, POOL_HALO), 1)
    halo_on = jnp.where(i > 0, 1.0, 0.0)
    bandh = (jnp.where(ch > rh + POOL_HALO - w, 1.0, 0.0) * halo_on).astype(BF16)
    u = u_ref[...]
    wsum = _dot(band, u) + _dot(bandh, halo_ref[...])
    t = i * tm + lax.broadcasted_iota(jnp.int32, (tm, 1), 0)
    cnt = jnp.minimum(t + 1, w).astype(F32)
    pooled = wsum / cnt - u.astype(F32)
    o_ref[...] = (_dot(pooled.astype(BF16), pw_ref[...]) * ps_ref[...]).astype(BF16)


def _pool(proj, pool_w_bf16, pool_scale):
    s = proj.shape[0]
    tm = 512
    hb = tm // POOL_HALO
    return pl.pallas_call(
        _pool_kernel,
        grid=(s // tm, POOL_GROUPS),
        in_specs=[pl.BlockSpec((tm, POOL_GDIM), lambda i, g: (i, g)),
                  pl.BlockSpec((POOL_HALO, POOL_GDIM), lambda i, g: (jnp.maximum(i * hb - 1, 0), g)),
                  pl.BlockSpec((None, POOL_GDIM, POOL_GDIM), lambda i, g: (g, 0, 0)),
                  pl.BlockSpec((1, POOL_GDIM), lambda i, g: (0, g))],
        out_specs=pl.BlockSpec((tm, POOL_GDIM), lambda i, g: (i, g)),
        out_shape=jax.ShapeDtypeStruct((s, POOL_GROUPS * POOL_GDIM), BF16),
        compiler_params=_params(2, 32),
        name="pool",
    )(proj, proj, pool_w_bf16, pool_scale.reshape(1, -1))


RET_HPS = 4


def _ret_kernel(cdec_ref, q_ref, k_ref, v_ref, g_ref, cos_ref, sin_ref, dec_ref, qd_ref, kd_ref,
                gain_ref, pu_ref, pv_ref, o_ref, pub_ref, pvt_ref, state_ref):
    hp = pl.program_id(0)

    pub_ref[...] = pu_ref[...].astype(BF16)
    pvt_ref[...] = pv_ref[...].T.astype(BF16)

    @pl.when(pl.program_id(1) == 0)
    def _():
        state_ref[...] = jnp.zeros_like(state_ref)

    cos = cos_ref[...]
    sin = sin_ref[...]
    half = RET_DK // 2

    def rot(t_ref, col0):
        t1 = t_ref[:, col0:col0 + half].astype(F32)
        t2 = t_ref[:, col0 + half:col0 + RET_DK].astype(F32)
        return t1 * cos - t2 * sin, t2 * cos + t1 * sin

    for j in range(RET_HPS):
        q1, q2 = rot(q_ref, j * RET_DK)
        k1, k2 = rot(k_ref, j * RET_DK)
        qd = qd_ref[j]
        kd = kd_ref[j]
        qb = jnp.concatenate([q1, q2], axis=1).astype(BF16)
        kb = jnp.concatenate([k1, k2], axis=1).astype(BF16)
        qdb = jnp.concatenate([q1 * qd, q2 * qd], axis=1).astype(BF16)
        kdb = jnp.concatenate([k1 * kd, k2 * kd], axis=1).astype(BF16)
        vcols = slice(j * RET_DV, (j + 1) * RET_DV)
        v = v_ref[:, vcols]
        scores = lax.dot_general(qb, kb, (((1,), (1,)), ((), ())), preferred_element_type=F32) * dec_ref[j]
        inner = _dot(scores.astype(BF16), v)
        state = state_ref[j]
        cross = _dot(qdb, state.astype(BF16))
        state_ref[j] = state * cdec_ref[hp * RET_HPS + j] + lax.dot_general(
            kdb, v, (((0,), (0,)), ((), ())), preferred_element_type=F32)
        y = inner + cross
        mu = jnp.mean(y, axis=-1, keepdims=True)
        yc = y - mu
        var = jnp.mean(yc * yc, axis=-1, keepdims=True)
        yn = yc * lax.rsqrt(var + EPS) * gain_ref[:, vcols]
        gate = g_ref[:, vcols].astype(F32)
        o_ref[:, vcols] = (gate * _sigmoid(gate) * yn).astype(BF16)


def _ret_tables():
    c = RET_CHUNK
    log_g = jnp.log(1.0 - 2.0 ** (-5.0 - jnp.arange(RET_HEADS, dtype=F32)))
    idx = jnp.arange(c, dtype=F32)
    diff = idx[:, None] - idx[None, :]
    kscale = RET_DK ** -0.5
    dec = jnp.where(diff[None] >= 0, jnp.exp(jnp.maximum(diff, 0.0)[None] * log_g[:, None, None]), 0.0) * kscale
    qd = jnp.exp((idx + 1.0)[None, :] * log_g[:, None])
    kd = jnp.exp((c - 1.0 - idx)[None, :] * log_g[:, None]) * kscale
    half = RET_DK // 2
    qd = jnp.broadcast_to(qd[:, :, None], (RET_HEADS, c, half))
    kd = jnp.broadcast_to(kd[:, :, None], (RET_HEADS, c, half))
    cdec = jnp.exp(c * log_g)
    return cdec, dec, qd, kd


def _retention(proj, cos, sin, ret_gain, peer_u, peer_v):
    s = proj.shape[0]
    c = RET_CHUNK
    half = RET_DK // 2
    cdec, dec, qd, kd = _ret_tables()
    wqk, wv = RET_HPS * RET_DK, RET_HPS * RET_DV
    qb0, kb0 = COL_Q // wqk, COL_K // wqk
    vb0, gb0 = COL_V // wv, COL_G // wv
    n_chunks = s // c
    ne, d = peer_u.shape
    rows = ne // ((RET_HEADS // RET_HPS) * n_chunks)
    assert rows % LANES == 0 and PEER_ETILE % rows == 0
    per_tile = PEER_ETILE // rows

    def blk(h, i):
        return h * n_chunks + i

    return pl.pallas_call(
        _ret_kernel,
        grid=(RET_HEADS // RET_HPS, n_chunks),
        in_specs=[pl.BlockSpec(memory_space=pltpu.SMEM),
                  pl.BlockSpec((c, wqk), lambda h, i: (i, qb0 + h)),
                  pl.BlockSpec((c, wqk), lambda h, i: (i, kb0 + h)),
                  pl.BlockSpec((c, wv), lambda h, i: (i, vb0 + h)),
                  pl.BlockSpec((c, wv), lambda h, i: (i, gb0 + h)),
                  pl.BlockSpec((c, half), lambda h, i: (i, 0)),
                  pl.BlockSpec((c, half), lambda h, i: (i, 0)),
                  pl.BlockSpec((RET_HPS, c, c), lambda h, i: (h, 0, 0)),
                  pl.BlockSpec((RET_HPS, c, half), lambda h, i: (h, 0, 0)),
                  pl.BlockSpec((RET_HPS, c, half), lambda h, i: (h, 0, 0)),
                  pl.BlockSpec((1, wv), lambda h, i: (0, h)),
                  pl.BlockSpec((rows, d), lambda h, i: (blk(h, i), 0)),
                  pl.BlockSpec((rows, d), lambda h, i: (blk(h, i), 0))],
        out_specs=[pl.BlockSpec((c, wv), lambda h, i: (i, h)),
                   pl.BlockSpec((rows, d), lambda h, i: (blk(h, i), 0)),
                   pl.BlockSpec((None, d, rows), lambda h, i: (blk(h, i) // per_tile, 0, blk(h, i) % per_tile))],
        out_shape=[jax.ShapeDtypeStruct((s, RET_HEADS * RET_DV), BF16),
                   jax.ShapeDtypeStruct((ne, d), BF16),
                   jax.ShapeDtypeStruct((ne // PEER_ETILE, d, PEER_ETILE), BF16)],
        scratch_shapes=[pltpu.VMEM((RET_HPS, RET_DK, RET_DV), F32)],
        compiler_params=_params(2, 40),
        name="retention",
    )(cdec, proj, proj, proj, proj, cos, sin, dec, qd, kd, ret_gain.reshape(1, -1), peer_u, peer_v)


def _branch_kernel(p_ref, r_ref, ap_ref, ar_ref, wp_ref, wr_ref, o_ref):
    bp = _dot(p_ref[...], wp_ref[...])
    br = _dot(r_ref[...], wr_ref[...])
    ap = ap_ref[...].astype(F32)
    ar = ar_ref[...].astype(F32)
    o_ref[...] = (_sigmoid(ap) * bp + _sigmoid(ar) * br).astype(BF16)


def _branch(pool_out, ret_out, proj, wp, wr):
    s = pool_out.shape[0]
    d = wp.shape[1]
    tm, tn = 512, 512
    ap0, ar0 = COL_APOOL // tn, COL_ARET // tn
    return pl.pallas_call(
        _branch_kernel,
        grid=(s // tm, d // tn),
        in_specs=[pl.BlockSpec((tm, pool_out.shape[1]), lambda i, j: (i, 0)),
                  pl.BlockSpec((tm, ret_out.shape[1]), lambda i, j: (i, 0)),
                  pl.BlockSpec((tm, tn), lambda i, j: (i, ap0 + j)),
                  pl.BlockSpec((tm, tn), lambda i, j: (i, ar0 + j)),
                  pl.BlockSpec((wp.shape[0], tn), lambda i, j: (0, j)),
                  pl.BlockSpec((wr.shape[0], tn), lambda i, j: (0, j))],
        out_specs=pl.BlockSpec((tm, tn), lambda i, j: (i, j)),
        out_shape=jax.ShapeDtypeStruct((s, d), BF16),
        compiler_params=_params(2, 40),
        name="branch",
    )(pool_out, ret_out, proj, proj, wp.astype(BF16), wr.astype(BF16))


def _out_kernel(m_ref, w_ref, x_ref, mod_ref, gain_ref, x1_ref, fnt_ref):
    x1 = x_ref[...] + mod_ref[2:3, :] * _dot(m_ref[...], w_ref[...])
    x1_ref[...] = x1
    ms = jnp.mean(x1 * x1, axis=-1, keepdims=True)
    fn = x1 * lax.rsqrt(ms + EPS) * (gain_ref[...] * (1.0 + mod_ref[4:5, :])) + mod_ref[3:4, :]
    fnt_ref[...] = fn.T.astype(BF16)


def _out_proj(merged, w_bf16, x2d, mod, gain):
    s, d = x2d.shape
    tm = 512
    return pl.pallas_call(
        _out_kernel,
        grid=(s // tm,),
        in_specs=[pl.BlockSpec((tm, d), lambda i: (i, 0)),
                  pl.BlockSpec((d, d), lambda i: (0, 0)),
                  pl.BlockSpec((tm, d), lambda i: (i, 0)),
                  pl.BlockSpec((6, d), lambda i: (0, 0)),
                  pl.BlockSpec((1, d), lambda i: (0, 0))],
        out_specs=[pl.BlockSpec((tm, d), lambda i: (i, 0)),
                   pl.BlockSpec((d, tm), lambda i: (0, i))],
        out_shape=[jax.ShapeDtypeStruct((s, d), F32), jax.ShapeDtypeStruct((d, s), BF16)],
        compiler_params=_params(1, 56),
        name="out_proj",
    )(merged, w_bf16, x2d, mod, gain.reshape(1, d))


def _pscore_kernel(fnt_ref, wqt_ref, keys_ref, s_ref):
    t = fnt_ref.shape[1]
    qt = _dot(wqt_ref[...], fnt_ref[...]).astype(BF16)
    for hp in range(2 * PEER_HEADS):
        sc = _dot(keys_ref[hp], qt[hp * PEER_HALF:(hp + 1) * PEER_HALF, :])
        for lc in range(t // LANES):
            s_ref[hp, lc] = sc[:, lc * LANES:(lc + 1) * LANES]


def _peer_scores(fnt, wqt_bf16, keys_bf16):
    d, s = fnt.shape
    t = 512
    nq = wqt_bf16.shape[0]
    return pl.pallas_call(
        _pscore_kernel,
        grid=(s // t,),
        in_specs=[pl.BlockSpec((d, t), lambda i: (0, i)),
                  pl.BlockSpec((nq, d), lambda i: (0, 0)),
                  pl.BlockSpec((2 * PEER_HEADS, PEER_NKEYS, PEER_HALF), lambda i: (0, 0, 0))],
        out_specs=pl.BlockSpec((2 * PEER_HEADS, t // LANES, PEER_NKEYS, LANES), lambda i: (0, i, 0, 0)),
        out_shape=jax.ShapeDtypeStruct((2 * PEER_HEADS, s // LANES, PEER_NKEYS, LANES), F32),
        compiler_params=_params(1, 48),
        name="peer_scores",
    )(fnt, wqt_bf16, keys_bf16)


SUBLANES = 8
LOG2E = 1.4426950408889634


def _sort_network(n):
    size = 1
    while size < n:
        size *= 2

    def merge(lo, hi, r):
        step = r * 2
        if step < hi - lo:
            yield from merge(lo, hi, step)
            yield from merge(lo + r, hi, step)
            for i in range(lo + r, hi - r, step):
                yield (i, i + r)
        else:
            yield (lo, lo + r)

    def sort(lo, hi):
        if hi - lo >= 1:
            mid = lo + (hi - lo) // 2
            yield from sort(lo, mid)
            yield from sort(mid + 1, hi)
            yield from merge(lo, hi, 1)

    return [(i, j) for i, j in sort(0, size - 1) if j < n]


def _top_values(slabs, k):
    cols = list(slabs)
    for i, j in _sort_network(len(cols)):
        hi = jnp.maximum(cols[i], cols[j])
        cols[j] = jnp.minimum(cols[i], cols[j])
        cols[i] = hi
    vals = []
    for r in range(k):
        m = jnp.max(cols[0], axis=0, keepdims=True)
        vals.append(m)
        depth = min(len(cols), k - r)
        if r == k - 1:
            break
        hit = cols[0] == m
        for q in range(depth - 1):
            cols[q] = jnp.where(hit, cols[q + 1], cols[q])
        if depth == len(cols):
            cols[depth - 1] = jnp.where(hit, -jnp.inf, cols[depth - 1])
    return vals


def _rows_to_slabs(rows, n_slabs, row_id):
    slabs = []
    for g in range(n_slabs):
        slab = jnp.full(row_id.shape, -jnp.inf, F32)
        for q in range(SUBLANES):
            r = g * SUBLANES + q
            if r < len(rows):
                slab = jnp.where(row_id == q, rows[r], slab)
        slabs.append(slab)
    return slabs


PEER_ETILE = 512
PEER_IB = PEER_ETILE // PEER_NKEYS


PEER_NTOP = PEER_TOPK + 1


def _ptopk_kernel(s_ref, alpha_ref, beta_ref, tau_ref):
    n_lc = s_ref.shape[1]
    n_slabs = PEER_NKEYS // SUBLANES
    row_id = lax.broadcasted_iota(jnp.int32, (SUBLANES, LANES), 0)

    def body(lc, carry):
        sl = pl.ds(pl.multiple_of(lc * LANES, LANES), LANES)
        s1 = s_ref[0, lc]
        s2 = s_ref[1, lc]
        v1 = _top_values([s1[g * SUBLANES:(g + 1) * SUBLANES, :] for g in range(n_slabs)], PEER_NTOP)
        v2 = _top_values([s2[g * SUBLANES:(g + 1) * SUBLANES, :] for g in range(n_slabs)], PEER_NTOP)
        n_vs = -(-PEER_NTOP // SUBLANES)
        v1s = _rows_to_slabs(v1, n_vs, row_id)
        v2s = _rows_to_slabs(v2, n_vs, row_id)
        cand = [v1[0] + slab for slab in v2s]
        for r1 in range(1, SUBLANES):
            n_ok = PEER_NTOP // (r1 + 1)
            pair = v1[r1] + v2s[0]
            cand.append(pair if n_ok >= SUBLANES else jnp.where(row_id < n_ok, pair, -jnp.inf))
        for slab in v1s[1:]:
            cand.append(slab + v2[0])
        best = _top_values(cand, PEER_NTOP)
        z = jnp.ones_like(best[0])
        for r in range(1, PEER_TOPK):
            z = z + jnp.exp(best[r] - best[0])
        log_norm = best[0] + jnp.log(z)
        alpha = (s1 - log_norm) * LOG2E
        for g in range(PEER_NKEYS // PEER_IB):
            alpha_ref[g, :, sl] = alpha[g * PEER_IB:(g + 1) * PEER_IB, :]
        beta_ref[lc] = s2 * LOG2E
        tau_ref[:, sl] = (0.5 * (best[PEER_TOPK - 1] + best[PEER_TOPK]) - log_norm) * LOG2E
        return carry

    lax.fori_loop(0, n_lc, body, 0, unroll=4)


def _peer_topk(scores):
    hp, n_chunks, nk, _ = scores.shape
    s = n_chunks * LANES
    t = 1024
    ng = nk // PEER_IB
    return pl.pallas_call(
        _ptopk_kernel,
        grid=(s // t, PEER_HEADS),
        in_specs=[pl.BlockSpec((2, t // LANES, nk, LANES), lambda i, h: (h, i, 0, 0))],
        out_specs=[pl.BlockSpec((None, ng, PEER_IB, t), lambda i, h: (h, 0, 0, i)),
                   pl.BlockSpec((None, t // LANES, nk, LANES), lambda i, h: (h, i, 0, 0)),
                   pl.BlockSpec((None, 1, t), lambda i, h: (h, 0, i))],
        out_shape=[jax.ShapeDtypeStruct((PEER_HEADS, ng, PEER_IB, s), F32),
                   jax.ShapeDtypeStruct((PEER_HEADS, n_chunks, nk, LANES), F32),
                   jax.ShapeDtypeStruct((PEER_HEADS, 1, s), F32)],
        compiler_params=_params(2, 32),
        name="peer_topk",
    )(scores)


MXU_COLS = 256
Z_ROWS = 128
O_ROWS = 256
GATE_ROWS = 16


def _gelu_tanh(x):
    return 0.5 * x * (1.0 + jnp.tanh(0.7978845608028654 * (x + 0.044715 * (x * x * x))))


def _pdense_step(fnt_ref, u_ref, vt_ref, alpha_ref, beta_ref, tau_ref, o_ref,
                 z_prev_ref, z_next_ref, a_prev_ref, a_next_ref):
    t = fnt_ref.shape[1]
    n_lc = t // LANES
    d_model = vt_ref.shape[0]
    n_exp = u_ref.shape[0]
    mxu_chunks = []
    for c0 in range(0, t, MXU_COLS):
        cols = slice(c0, c0 + MXU_COLS)
        zs = [("z", cols, slice(r0, r0 + Z_ROWS)) for r0 in range(0, n_exp, Z_ROWS)]
        os_ = [("o", cols, slice(r0, r0 + O_ROWS)) for r0 in range(0, d_model, O_ROWS)]
        per_z = len(os_) // len(zs)
        for k, zp in enumerate(zs):
            mxu_chunks.append(zp)
            mxu_chunks.extend(os_[k * per_z:(k + 1) * per_z])
    n_jb = PEER_NKEYS // GATE_ROWS
    n_blocks = n_lc * n_jb
    place = {}
    for k, chunk in enumerate(mxu_chunks):
        place.setdefault(k * n_blocks // len(mxu_chunks), []).append(chunk)
    for lc in range(n_lc):
        sl = slice(lc * LANES, (lc + 1) * LANES)
        for jb in range(n_jb):
            for kind, cols, mrows in place.get(lc * n_jb + jb, ()):
                slabs = range(cols.start // LANES, cols.stop // LANES)
                if kind == "z":
                    zc = _dot(u_ref[mrows, :], fnt_ref[:, cols])
                    for k, slab in enumerate(slabs):
                        z_next_ref[slab, mrows, :] = zc[:, k * LANES:(k + 1) * LANES]
                else:
                    ac = jnp.concatenate([a_prev_ref[slab] for slab in slabs], axis=1)
                    o_ref[mrows, cols] += _dot(vt_ref[mrows, :], ac)
            jrows = slice(jb * GATE_ROWS, (jb + 1) * GATE_ROWS)
            accs = [jnp.zeros((GATE_ROWS, LANES), F32) for _ in range(PEER_IB)]
            for h in range(PEER_HEADS):
                beta = beta_ref[h, lc, jrows, :]
                tau = tau_ref[h, :, sl]
                for il in range(PEER_IB):
                    lg = alpha_ref[h, il:il + 1, sl] + beta
                    accs[il] = accs[il] + jnp.where(lg >= tau, jnp.exp2(lg), 0.0)
            for il in range(PEER_IB):
                rows = slice(il * PEER_NKEYS + jb * GATE_ROWS, il * PEER_NKEYS + (jb + 1) * GATE_ROWS)
                a_next_ref[lc, rows, :] = (accs[il] * _gelu_tanh(z_prev_ref[lc, rows, :])).astype(BF16)


def _pdense_kernel(fnt_ref, u_ref, vt_ref, alpha_ref, beta_ref, tau_ref, x1_ref, mod_ref, gain_ref, y_ref,
                   z0_ref, z1_ref, a0_ref, a1_ref, acc_ref, *, n_et, apply_norm):
    s = pl.program_id(0)
    ins = (fnt_ref, u_ref, vt_ref, alpha_ref, beta_ref, tau_ref, acc_ref)
    e_out = (s + n_et - 2) % n_et

    @pl.when(s == 0)
    def _():
        z1_ref[...] = jnp.zeros_like(z1_ref)
        a1_ref[...] = jnp.zeros_like(a1_ref)

    @pl.when((s == 0) | (e_out == 0))
    def _():
        acc_ref[...] = jnp.zeros_like(acc_ref)

    @pl.when(s % 2 == 0)
    def _():
        _pdense_step(*ins, z1_ref, z0_ref, a1_ref, a0_ref)

    @pl.when(s % 2 == 1)
    def _():
        _pdense_step(*ins, z0_ref, z1_ref, a0_ref, a1_ref)

    @pl.when((s >= 2) & (e_out == n_et - 1))
    def _():
        x2 = x1_ref[...] + mod_ref[5:6, :] * acc_ref[...].T
        if apply_norm:
            ms = jnp.mean(x2 * x2, axis=-1, keepdims=True)
            x2 = x2 * lax.rsqrt(ms + EPS) * gain_ref[...]
        y_ref[...] = x2


def _peer_dense(fnt, u_bf16, vt_bf16, alpha, beta, tau, x1, mod, gain, apply_norm):
    d, s = fnt.shape
    ne = u_bf16.shape[0]
    t = 512
    et = PEER_ETILE
    n_et = ne // et
    assert n_et % 2 == 0
    nk = beta.shape[2]
    n_lc = t // LANES
    n_pairs = (s // t) * n_et

    def pair(step, lag):
        p = jnp.clip(step - lag, 0, n_pairs - 1)
        return p // n_et, p % n_et

    return pl.pallas_call(
        functools.partial(_pdense_kernel, n_et=n_et, apply_norm=apply_norm),
        grid=(n_pairs + 2,),
        in_specs=[pl.BlockSpec((d, t), lambda p: (0, pair(p, 0)[0])),
                  pl.BlockSpec((et, d), lambda p: (pair(p, 0)[1], 0)),
                  pl.BlockSpec((None, d, et), lambda p: (pair(p, 2)[1], 0, 0)),
                  pl.BlockSpec((PEER_HEADS, None, PEER_IB, t), lambda p: (0, pair(p, 1)[1], 0, pair(p, 1)[0])),
                  pl.BlockSpec((PEER_HEADS, n_lc, nk, LANES), lambda p: (0, pair(p, 1)[0], 0, 0)),
                  pl.BlockSpec((PEER_HEADS, 1, t), lambda p: (0, 0, pair(p, 1)[0])),
                  pl.BlockSpec((t, d), lambda p: (pair(p, 2)[0], 0)),
                  pl.BlockSpec((6, d), lambda p: (0, 0)),
                  pl.BlockSpec((1, d), lambda p: (0, 0))],
        out_specs=pl.BlockSpec((t, d), lambda p: (pair(p, 2)[0], 0)),
        out_shape=jax.ShapeDtypeStruct((s, d), F32),
        scratch_shapes=[pltpu.VMEM((n_lc, et, LANES), F32), pltpu.VMEM((n_lc, et, LANES), F32),
                        pltpu.VMEM((n_lc, et, LANES), BF16), pltpu.VMEM((n_lc, et, LANES), BF16),
                        pltpu.VMEM((d, t), F32)],
        compiler_params=_params(1, 56),
        name="peer_dense",
    )(fnt, u_bf16, vt_bf16, alpha, beta, tau, x1, mod, gain.reshape(1, d))


def kernel(x, c, positions, norm_mix_gain, w_ada, b_ada, w_in, pool_w, pool_scale, ret_norm_gain,
           w_branch_pool, w_branch_ret, w_out, norm_ffn_gain, peer_w_query, peer_sub_keys, peer_u, peer_v,
           final_norm_gain):
    batch, s, d = x.shape
    assert batch == 1 and d == D_MODEL and s % 1024 == 0
    depth = w_in.shape[0]
    xs = x.reshape(s, d)
    cos, sin = _rope_tables(positions.reshape(s))
    for l in range(depth):
        mod = _ada(c, w_ada[l], b_ada[l])
        proj = _in_proj(xs, norm_mix_gain[l], mod, w_in[l])
        pool_out = _pool(proj, pool_w[l].astype(BF16), pool_scale[l])
        ret_out, u_bf16, vt_tiles = _retention(proj, cos, sin, ret_norm_gain[l], peer_u[l], peer_v[l])
        merged = _branch(pool_out, ret_out, proj, w_branch_pool[l], w_branch_ret[l])
        x1, fnt = _out_proj(merged, w_out[l].astype(BF16), xs, mod, norm_ffn_gain[l])
        keys = peer_sub_keys[l].reshape(2 * PEER_HEADS, PEER_NKEYS, PEER_HALF).astype(BF16)
        scores = _peer_scores(fnt, peer_w_query[l].T.astype(BF16), keys)
        alpha, beta, tau = _peer_topk(scores)
        xs = _peer_dense(fnt, u_bf16, vt_tiles, alpha, beta, tau, x1, mod, final_norm_gain,
                         apply_norm=(l == depth - 1))
    return xs.reshape(batch, s, d)
```

```python
import functools

import jax
import jax.numpy as jnp
from jax import lax
from jax.experimental import pallas as pl
from jax.experimental.pallas import tpu as pltpu

F32 = jnp.float32
BF16 = jnp.bfloat16

D_MODEL = 2048
EPS = 1e-6
POOL_GROUPS = 4
POOL_GDIM = 512
RET_HEADS = 8
RET_DK = 256
RET_DV = 512
ROPE_BASE = 10000.0
PEER_HEADS = 8
PEER_NKEYS = 128
PEER_HALF = 128
PEER_TOPK = 16
N_EXPERTS = PEER_NKEYS * PEER_NKEYS

COL_POOL, COL_Q, COL_K, COL_V, COL_G, COL_APOOL, COL_ARET = 0, 2048, 4096, 6144, 10240, 14336, 16384
IN_COLS = 18432

LANES = 128
MIB = 1024 * 1024

RET_CHUNK = 256


def _params(n_axes, vmem_mib, flags=None):
    return pltpu.CompilerParams(dimension_semantics=("arbitrary",) * n_axes,
                                vmem_limit_bytes=vmem_mib * MIB, flags=flags)


def _dot(a, b):
    return jnp.dot(a, b, preferred_element_type=F32)


def _sigmoid(x):
    return 1.0 / (1.0 + jnp.exp(-x))


def _ada_kernel(c_ref, w_ref, b_ref, o_ref):
    @pl.when(pl.program_id(0) == 0)
    def _():
        o_ref[...] = b_ref[...]

    c = c_ref[...]
    cond = c * _sigmoid(c)
    o_ref[...] += jnp.sum(cond * w_ref[...], axis=0, keepdims=True)


def _ada(c, w, b):
    d, n = w.shape
    tk = 256
    out = pl.pallas_call(
        _ada_kernel,
        grid=(d // tk,),
        in_specs=[pl.BlockSpec((tk, 1), lambda k: (k, 0)),
                  pl.BlockSpec((tk, n), lambda k: (k, 0)),
                  pl.BlockSpec((1, n), lambda k: (0, 0))],
        out_specs=pl.BlockSpec((1, n), lambda k: (0, 0)),
        out_shape=jax.ShapeDtypeStruct((1, n), F32),
        compiler_params=_params(1, 40),
        name="ada",
    )(c.reshape(d, 1), w, b.reshape(1, n))
    return out.reshape(6, d)


def _in_kernel(x_ref, gain_ref, mod_ref, w_ref, o_ref, hn_ref):
    tm = x_ref.shape[0]
    rc = 256

    @pl.when(pl.program_id(1) == 0)
    def _():
        scale = gain_ref[...] * (1.0 + mod_ref[1:2, :])
        shift = mod_ref[0:1, :]

        def body(r, carry):
            rows = pl.ds(pl.multiple_of(r * rc, rc), rc)
            x = x_ref[rows, :]
            ms = jnp.mean(x * x, axis=-1, keepdims=True)
            hn_ref[rows, :] = (x * lax.rsqrt(ms + EPS) * scale + shift).astype(BF16)
            return carry

        lax.fori_loop(0, tm // rc, body, 0)

    o_ref[...] = _dot(hn_ref[...], w_ref[...].astype(BF16)).astype(BF16)


def _in_proj(x2d, gain, mod, w):
    s, d = x2d.shape
    n = w.shape[1]
    tm, tn = 1024, 1024
    return pl.pallas_call(
        _in_kernel,
        grid=(s // tm, n // tn),
        in_specs=[pl.BlockSpec((tm, d), lambda i, j: (i, 0)),
                  pl.BlockSpec((1, d), lambda i, j: (0, 0)),
                  pl.BlockSpec((6, d), lambda i, j: (0, 0)),
                  pl.BlockSpec((d, tn), lambda i, j: (0, j))],
        out_specs=pl.BlockSpec((tm, tn), lambda i, j: (i, j)),
        out_shape=jax.ShapeDtypeStruct((s, n), BF16),
        scratch_shapes=[pltpu.VMEM((tm, d), BF16)],
        compiler_params=_params(2, 54),
        name="in_proj",
    )(x2d, gain.reshape(1, d), mod, w)


def _rope_kernel(pos_ref, inv_ref, cos_ref, sin_ref):
    ang = pos_ref[...].astype(F32) * inv_ref[...]
    cos_ref[...] = jnp.cos(ang)
    sin_ref[...] = jnp.sin(ang)


def _rope_tables(positions):
    s = positions.shape[0]
    half = RET_DK // 2
    inv_freq = (ROPE_BASE ** (-jnp.arange(0, RET_DK, 2, dtype=F32) / RET_DK)).reshape(1, half)
    tm = 1024
    return pl.pallas_call(
        _rope_kernel,
        grid=(s // tm,),
        in_specs=[pl.BlockSpec((tm, 1), lambda i: (i, 0)),
                  pl.BlockSpec((1, half), lambda i: (0, 0))],
        out_specs=[pl.BlockSpec((tm, half), lambda i: (i, 0)),
                   pl.BlockSpec((tm, half), lambda i: (i, 0))],
        out_shape=[jax.ShapeDtypeStruct((s, half), F32), jax.ShapeDtypeStruct((s, half), F32)],
        compiler_params=_params(1, 32),
        name="rope",
    )(positions.reshape(s, 1), inv_freq)


POOL_HALO = 16


def _pool_kernel(u_ref, halo_ref, pw_ref, ps_ref, o_ref):
    i = pl.program_id(0)
    g = pl.program_id(1)
    tm = u_ref.shape[0]
    w = jnp.left_shift(2, g)
    r = lax.broadcasted_iota(jnp.int32, (tm, tm), 0)
    c = lax.broadcasted_iota(jnp.int32, (tm, tm), 1)
    band = jnp.where(c <= r, jnp.where(c > r - w, 1.0, 0.0), 0.0).astype(BF16)
    rh = lax.broadcasted_iota(jnp.int32, (tm, POOL_HALO), 0)
    ch = lax.broadcasted_iota(jnp.int32, (tm, POOL_HALO), 1)
    halo_on = jnp.where(i > 0, 1.0, 0.0)
    bandh = (jnp.where(ch > rh + POOL_HALO - w, 1.0, 0.0) * halo_on).astype(BF16)
    u = u_ref[...]
    wsum = _dot(band, u) + _dot(bandh, halo_ref[...])
    t = i * tm + lax.broadcasted_iota(jnp.int32, (tm, 1), 0)
    cnt = jnp.minimum(t + 1, w).astype(F32)
    pooled = wsum / cnt - u.astype(F32)
    o_ref[...] = (_dot(pooled.astype(BF16), pw_ref[...]) * ps_ref[...]).astype(BF16)


def _pool(proj, pool_w_bf16, pool_scale):
    s = proj.shape[0]
    tm = 512
    hb = tm // POOL_HALO
    return pl.pallas_call(
        _pool_kernel,
        grid=(s // tm, POOL_GROUPS),
        in_specs=[pl.BlockSpec((tm, POOL_GDIM), lambda i, g: (i, g)),
                  pl.BlockSpec((POOL_HALO, POOL_GDIM), lambda i, g: (jnp.maximum(i * hb - 1, 0), g)),
                  pl.BlockSpec((None, POOL_GDIM, POOL_GDIM), lambda i, g: (g, 0, 0)),
                  pl.BlockSpec((1, POOL_GDIM), lambda i, g: (0, g))],
        out_specs=pl.BlockSpec((tm, POOL_GDIM), lambda i, g: (i, g)),
        out_shape=jax.ShapeDtypeStruct((s, POOL_GROUPS * POOL_GDIM), BF16),
        compiler_params=_params(2, 32),
        name="pool",
    )(proj, proj, pool_w_bf16, pool_scale.reshape(1, -1))


RET_HPS = 4


def _ret_kernel(cdec_ref, q_ref, k_ref, v_ref, g_ref, cos_ref, sin_ref, dec_ref, qd_ref, kd_ref,
                gain_ref, pu_ref, pv_ref, o_ref, pub_ref, pvt_ref, state_ref):
    hp = pl.program_id(0)

    pub_ref[...] = pu_ref[...].astype(BF16)
    pvt_ref[...] = pv_ref[...].T.astype(BF16)

    @pl.when(pl.program_id(1) == 0)
    def _():
        state_ref[...] = jnp.zeros_like(state_ref)

    cos = cos_ref[...]
    sin = sin_ref[...]
    half = RET_DK // 2

    def rot(t_ref, col0):
        t1 = t_ref[:, col0:col0 + half].astype(F32)
        t2 = t_ref[:, col0 + half:col0 + RET_DK].astype(F32)
        return t1 * cos - t2 * sin, t2 * cos + t1 * sin

    for j in range(RET_HPS):
        q1, q2 = rot(q_ref, j * RET_DK)
        k1, k2 = rot(k_ref, j * RET_DK)
        qd = qd_ref[j]
        kd = kd_ref[j]
        qb = jnp.concatenate([q1, q2], axis=1).astype(BF16)
        kb = jnp.concatenate([k1, k2], axis=1).astype(BF16)
        qdb = jnp.concatenate([q1 * qd, q2 * qd], axis=1).astype(BF16)
        kdb = jnp.concatenate([k1 * kd, k2 * kd], axis=1).astype(BF16)
        vcols = slice(j * RET_DV, (j + 1) * RET_DV)
        v = v_ref[:, vcols]
        scores = lax.dot_general(qb, kb, (((1,), (1,)), ((), ())), preferred_element_type=F32) * dec_ref[j]
        inner = _dot(scores.astype(BF16), v)
        state = state_ref[j]
        cross = _dot(qdb, state.astype(BF16))
        state_ref[j] = state * cdec_ref[hp * RET_HPS + j] + lax.dot_general(
            kdb, v, (((0,), (0,)), ((), ())), preferred_element_type=F32)
        y = inner + cross
        mu = jnp.mean(y, axis=-1, keepdims=True)
        yc = y - mu
        var = jnp.mean(yc * yc, axis=-1, keepdims=True)
        yn = yc * lax.rsqrt(var + EPS) * gain_ref[:, vcols]
        gate = g_ref[:, vcols].astype(F32)
        o_ref[:, vcols] = (gate * _sigmoid(gate) * yn).astype(BF16)


def _ret_tables():
    c = RET_CHUNK
    log_g = jnp.log(1.0 - 2.0 ** (-5.0 - jnp.arange(RET_HEADS, dtype=F32)))
    idx = jnp.arange(c, dtype=F32)
    diff = idx[:, None] - idx[None, :]
    kscale = RET_DK ** -0.5
    dec = jnp.where(diff[None] >= 0, jnp.exp(jnp.maximum(diff, 0.0)[None] * log_g[:, None, None]), 0.0) * kscale
    qd = jnp.exp((idx + 1.0)[None, :] * log_g[:, None])
    kd = jnp.exp((c - 1.0 - idx)[None, :] * log_g[:, None]) * kscale
    half = RET_DK // 2
    qd = jnp.broadcast_to(qd[:, :, None], (RET_HEADS, c, half))
    kd = jnp.broadcast_to(kd[:, :, None], (RET_HEADS, c, half))
    cdec = jnp.exp(c * log_g)
    return cdec, dec, qd, kd


def _retention(proj, cos, sin, ret_gain, peer_u, peer_v):
    s = proj.shape[0]
    c = RET_CHUNK
    half = RET_DK // 2
    cdec, dec, qd, kd = _ret_tables()
    wqk, wv = RET_HPS * RET_DK, RET_HPS * RET_DV
    qb0, kb0 = COL_Q // wqk, COL_K // wqk
    vb0, gb0 = COL_V // wv, COL_G // wv
    n_chunks = s // c
    ne, d = peer_u.shape
    rows = ne // ((RET_HEADS // RET_HPS) * n_chunks)
    assert rows % LANES == 0 and PEER_ETILE % rows == 0
    per_tile = PEER_ETILE // rows

    def blk(h, i):
        return h * n_chunks + i

    return pl.pallas_call(
        _ret_kernel,
        grid=(RET_HEADS // RET_HPS, n_chunks),
        in_specs=[pl.BlockSpec(memory_space=pltpu.SMEM),
                  pl.BlockSpec((c, wqk), lambda h, i: (i, qb0 + h)),
                  pl.BlockSpec((c, wqk), lambda h, i: (i, kb0 + h)),
                  pl.BlockSpec((c, wv), lambda h, i: (i, vb0 + h)),
                  pl.BlockSpec((c, wv), lambda h, i: (i, gb0 + h)),
                  pl.BlockSpec((c, half), lambda h, i: (i, 0)),
                  pl.BlockSpec((c, half), lambda h, i: (i, 0)),
                  pl.BlockSpec((RET_HPS, c, c), lambda h, i: (h, 0, 0)),
                  pl.BlockSpec((RET_HPS, c, half), lambda h, i: (h, 0, 0)),
                  pl.BlockSpec((RET_HPS, c, half), lambda h, i: (h, 0, 0)),
                  pl.BlockSpec((1, wv), lambda h, i: (0, h)),
                  pl.BlockSpec((rows, d), lambda h, i: (blk(h, i), 0)),
                  pl.BlockSpec((rows, d), lambda h, i: (blk(h, i), 0))],
        out_specs=[pl.BlockSpec((c, wv), lambda h, i: (i, h)),
                   pl.BlockSpec((rows, d), lambda h, i: (blk(h, i), 0)),
                   pl.BlockSpec((None, d, rows), lambda h, i: (blk(h, i) // per_tile, 0, blk(h, i) % per_tile))],
        out_shape=[jax.ShapeDtypeStruct((s, RET_HEADS * RET_DV), BF16),
                   jax.ShapeDtypeStruct((ne, d), BF16),
                   jax.ShapeDtypeStruct((ne // PEER_ETILE, d, PEER_ETILE), BF16)],
        scratch_shapes=[pltpu.VMEM((RET_HPS, RET_DK, RET_DV), F32)],
        compiler_params=_params(2, 40),
        name="retention",
    )(cdec, proj, proj, proj, proj, cos, sin, dec, qd, kd, ret_gain.reshape(1, -1), peer_u, peer_v)


def _branch_kernel(p_ref, r_ref, ap_ref, ar_ref, wp_ref, wr_ref, o_ref):
    bp = _dot(p_ref[...], wp_ref[...])
    br = _dot(r_ref[...], wr_ref[...])
    ap = ap_ref[...].astype(F32)
    ar = ar_ref[...].astype(F32)
    o_ref[...] = (_sigmoid(ap) * bp + _sigmoid(ar) * br).astype(BF16)


def _branch(pool_out, ret_out, proj, wp, wr):
    s = pool_out.shape[0]
    d = wp.shape[1]
    tm, tn = 512, 1024
    ap0, ar0 = COL_APOOL // tn, COL_ARET // tn
    return pl.pallas_call(
        _branch_kernel,
        grid=(s // tm, d // tn),
        in_specs=[pl.BlockSpec((tm, pool_out.shape[1]), lambda i, j: (i, 0)),
                  pl.BlockSpec((tm, ret_out.shape[1]), lambda i, j: (i, 0)),
                  pl.BlockSpec((tm, tn), lambda i, j: (i, ap0 + j)),
                  pl.BlockSpec((tm, tn), lambda i, j: (i, ar0 + j)),
                  pl.BlockSpec((wp.shape[0], tn), lambda i, j: (0, j)),
                  pl.BlockSpec((wr.shape[0], tn), lambda i, j: (0, j))],
        out_specs=pl.BlockSpec((tm, tn), lambda i, j: (i, j)),
        out_shape=jax.ShapeDtypeStruct((s, d), BF16),
        compiler_params=_params(2, 50),
        name="branch",
    )(pool_out, ret_out, proj, proj, wp.astype(BF16), wr.astype(BF16))


def _out_kernel(m_ref, w_ref, x_ref, mod_ref, gain_ref, x1_ref, fnt_ref):
    x1 = x_ref[...] + mod_ref[2:3, :] * _dot(m_ref[...], w_ref[...])
    x1_ref[...] = x1
    ms = jnp.mean(x1 * x1, axis=-1, keepdims=True)
    fn = x1 * lax.rsqrt(ms + EPS) * (gain_ref[...] * (1.0 + mod_ref[4:5, :])) + mod_ref[3:4, :]
    fnt_ref[...] = fn.T.astype(BF16)


def _out_proj(merged, w_bf16, x2d, mod, gain):
    s, d = x2d.shape
    tm = 512
    return pl.pallas_call(
        _out_kernel,
        grid=(s // tm,),
        in_specs=[pl.BlockSpec((tm, d), lambda i: (i, 0)),
                  pl.BlockSpec((d, d), lambda i: (0, 0)),
                  pl.BlockSpec((tm, d), lambda i: (i, 0)),
                  pl.BlockSpec((6, d), lambda i: (0, 0)),
                  pl.BlockSpec((1, d), lambda i: (0, 0))],
        out_specs=[pl.BlockSpec((tm, d), lambda i: (i, 0)),
                   pl.BlockSpec((d, tm), lambda i: (0, i))],
        out_shape=[jax.ShapeDtypeStruct((s, d), F32), jax.ShapeDtypeStruct((d, s), BF16)],
        compiler_params=_params(1, 56),
        name="out_proj",
    )(merged, w_bf16, x2d, mod, gain.reshape(1, d))


def _pscore_kernel(fnt_ref, wqt_ref, keys_ref, s_ref):
    t = fnt_ref.shape[1]
    qt = _dot(wqt_ref[...], fnt_ref[...]).astype(BF16)
    for hp in range(2 * PEER_HEADS):
        sc = _dot(keys_ref[hp], qt[hp * PEER_HALF:(hp + 1) * PEER_HALF, :])
        for lc in range(t // LANES):
            s_ref[hp, lc] = sc[:, lc * LANES:(lc + 1) * LANES]


def _peer_scores(fnt, wqt_bf16, keys_bf16):
    d, s = fnt.shape
    t = 512
    nq = wqt_bf16.shape[0]
    return pl.pallas_call(
        _pscore_kernel,
        grid=(s // t,),
        in_specs=[pl.BlockSpec((d, t), lambda i: (0, i)),
                  pl.BlockSpec((nq, d), lambda i: (0, 0)),
                  pl.BlockSpec((2 * PEER_HEADS, PEER_NKEYS, PEER_HALF), lambda i: (0, 0, 0))],
        out_specs=pl.BlockSpec((2 * PEER_HEADS, t // LANES, PEER_NKEYS, LANES), lambda i: (0, i, 0, 0)),
        out_shape=jax.ShapeDtypeStruct((2 * PEER_HEADS, s // LANES, PEER_NKEYS, LANES), F32),
        compiler_params=_params(1, 48),
        name="peer_scores",
    )(fnt, wqt_bf16, keys_bf16)


SUBLANES = 8
LOG2E = 1.4426950408889634


def _sort_network(n):
    size = 1
    while size < n:
        size *= 2

    def merge(lo, hi, r):
        step = r * 2
        if step < hi - lo:
            yield from merge(lo, hi, step)
            yield from merge(lo + r, hi, step)
            for i in range(lo + r, hi - r, step):
                yield (i, i + r)
        else:
            yield (lo, lo + r)

    def sort(lo, hi):
        if hi - lo >= 1:
            mid = lo + (hi - lo) // 2
            yield from sort(lo, mid)
            yield from sort(mid + 1, hi)
            yield from merge(lo, hi, 1)

    return [(i, j) for i, j in sort(0, size - 1) if j < n]


def _top_values(slabs, k):
    cols = list(slabs)
    for i, j in _sort_network(len(cols)):
        hi = jnp.maximum(cols[i], cols[j])
        cols[j] = jnp.minimum(cols[i], cols[j])
        cols[i] = hi
    vals = []
    for r in range(k):
        m = jnp.max(cols[0], axis=0, keepdims=True)
        vals.append(m)
        depth = min(len(cols), k - r)
        if r == k - 1:
            break
        hit = cols[0] == m
        for q in range(depth - 1):
            cols[q] = jnp.where(hit, cols[q + 1], cols[q])
        if depth == len(cols):
            cols[depth - 1] = jnp.where(hit, -jnp.inf, cols[depth - 1])
    return vals


def _rows_to_slabs(rows, n_slabs, row_id):
    slabs = []
    for g in range(n_slabs):
        slab = jnp.full(row_id.shape, -jnp.inf, F32)
        for q in range(SUBLANES):
            r = g * SUBLANES + q
            if r < len(rows):
                slab = jnp.where(row_id == q, rows[r], slab)
        slabs.append(slab)
    return slabs


PEER_ETILE = 512
PEER_IB = PEER_ETILE // PEER_NKEYS


PEER_NTOP = PEER_TOPK + 1


def _ptopk_kernel(s_ref, alpha_ref, beta_ref, tau_ref):
    n_lc = s_ref.shape[1]
    n_slabs = PEER_NKEYS // SUBLANES
    row_id = lax.broadcasted_iota(jnp.int32, (SUBLANES, LANES), 0)

    def body(lc, carry):
        sl = pl.ds(pl.multiple_of(lc * LANES, LANES), LANES)
        s1 = s_ref[0, lc]
        s2 = s_ref[1, lc]
        v1 = _top_values([s1[g * SUBLANES:(g + 1) * SUBLANES, :] for g in range(n_slabs)], PEER_NTOP)
        v2 = _top_values([s2[g * SUBLANES:(g + 1) * SUBLANES, :] for g in range(n_slabs)], PEER_NTOP)
        n_vs = -(-PEER_NTOP // SUBLANES)
        v1s = _rows_to_slabs(v1, n_vs, row_id)
        v2s = _rows_to_slabs(v2, n_vs, row_id)
        cand = [v1[0] + slab for slab in v2s]
        for r1 in range(1, SUBLANES):
            n_ok = PEER_NTOP // (r1 + 1)
            pair = v1[r1] + v2s[0]
            cand.append(pair if n_ok >= SUBLANES else jnp.where(row_id < n_ok, pair, -jnp.inf))
        for slab in v1s[1:]:
            cand.append(slab + v2[0])
        best = _top_values(cand, PEER_NTOP)
        z = jnp.ones_like(best[0])
        for r in range(1, PEER_TOPK):
            z = z + jnp.exp(best[r] - best[0])
        log_norm = best[0] + jnp.log(z)
        alpha = (s1 - log_norm) * LOG2E
        for g in range(PEER_NKEYS // PEER_IB):
            alpha_ref[g, :, sl] = alpha[g * PEER_IB:(g + 1) * PEER_IB, :]
        beta_ref[lc] = s2 * LOG2E
        tau_ref[:, sl] = (0.5 * (best[PEER_TOPK - 1] + best[PEER_TOPK]) - log_norm) * LOG2E
        return carry

    lax.fori_loop(0, n_lc, body, 0, unroll=4)


def _peer_topk(scores):
    hp, n_chunks, nk, _ = scores.shape
    s = n_chunks * LANES
    t = 1024
    ng = nk // PEER_IB
    return pl.pallas_call(
        _ptopk_kernel,
        grid=(s // t, PEER_HEADS),
        in_specs=[pl.BlockSpec((2, t // LANES, nk, LANES), lambda i, h: (h, i, 0, 0))],
        out_specs=[pl.BlockSpec((None, ng, PEER_IB, t), lambda i, h: (h, 0, 0, i)),
                   pl.BlockSpec((None, t // LANES, nk, LANES), lambda i, h: (h, i, 0, 0)),
                   pl.BlockSpec((None, 1, t), lambda i, h: (h, 0, i))],
        out_shape=[jax.ShapeDtypeStruct((PEER_HEADS, ng, PEER_IB, s), F32),
                   jax.ShapeDtypeStruct((PEER_HEADS, n_chunks, nk, LANES), F32),
                   jax.ShapeDtypeStruct((PEER_HEADS, 1, s), F32)],
        compiler_params=_params(2, 32),
        name="peer_topk",
    )(scores)


MXU_COLS = 256
Z_ROWS = 128
O_ROWS = 256
GATE_ROWS = 16


def _gelu_tanh(x):
    return 0.5 * x * (1.0 + jnp.tanh(0.7978845608028654 * (x + 0.044715 * (x * x * x))))


def _pdense_step(fnt_ref, u_ref, vt_ref, alpha_ref, beta_ref, tau_ref, o_ref,
                 z_prev_ref, z_next_ref, a_prev_ref, a_next_ref):
    t = fnt_ref.shape[1]
    n_lc = t // LANES
    d_model = vt_ref.shape[0]
    n_exp = u_ref.shape[0]
    mxu_chunks = []
    for c0 in range(0, t, MXU_COLS):
        cols = slice(c0, c0 + MXU_COLS)
        zs = [("z", cols, slice(r0, r0 + Z_ROWS)) for r0 in range(0, n_exp, Z_ROWS)]
        os_ = [("o", cols, slice(r0, r0 + O_ROWS)) for r0 in range(0, d_model, O_ROWS)]
        per_z = len(os_) // len(zs)
        for k, zp in enumerate(zs):
            mxu_chunks.append(zp)
            mxu_chunks.extend(os_[k * per_z:(k + 1) * per_z])
    n_jb = PEER_NKEYS // GATE_ROWS
    n_blocks = n_lc * n_jb
    place = {}
    for k, chunk in enumerate(mxu_chunks):
        place.setdefault(k * n_blocks // len(mxu_chunks), []).append(chunk)
    for lc in range(n_lc):
        sl = slice(lc * LANES, (lc + 1) * LANES)
        for jb in range(n_jb):
            for kind, cols, mrows in place.get(lc * n_jb + jb, ()):
                slabs = range(cols.start // LANES, cols.stop // LANES)
                if kind == "z":
                    zc = _dot(u_ref[mrows, :], fnt_ref[:, cols])
                    for k, slab in enumerate(slabs):
                        z_next_ref[slab, mrows, :] = zc[:, k * LANES:(k + 1) * LANES]
                else:
                    ac = jnp.concatenate([a_prev_ref[slab] for slab in slabs], axis=1)
                    o_ref[mrows, cols] += _dot(vt_ref[mrows, :], ac)
            jrows = slice(jb * GATE_ROWS, (jb + 1) * GATE_ROWS)
            accs = [jnp.zeros((GATE_ROWS, LANES), F32) for _ in range(PEER_IB)]
            for h in range(PEER_HEADS):
                beta = beta_ref[h, lc, jrows, :]
                tau = tau_ref[h, :, sl]
                for il in range(PEER_IB):
                    lg = alpha_ref[h, il:il + 1, sl] + beta
                    accs[il] = accs[il] + jnp.where(lg >= tau, jnp.exp2(lg), 0.0)
            for il in range(PEER_IB):
                rows = slice(il * PEER_NKEYS + jb * GATE_ROWS, il * PEER_NKEYS + (jb + 1) * GATE_ROWS)
                a_next_ref[lc, rows, :] = (accs[il] * _gelu_tanh(z_prev_ref[lc, rows, :])).astype(BF16)


def _pdense_kernel(fnt_ref, u_ref, vt_ref, alpha_ref, beta_ref, tau_ref, x1_ref, mod_ref, gain_ref, y_ref,
                   z0_ref, z1_ref, a0_ref, a1_ref, acc_ref, *, n_et, apply_norm):
    s = pl.program_id(0)
    ins = (fnt_ref, u_ref, vt_ref, alpha_ref, beta_ref, tau_ref, acc_ref)
    e_out = (s + n_et - 2) % n_et

    @pl.when(s == 0)
    def _():
        z1_ref[...] = jnp.zeros_like(z1_ref)
        a1_ref[...] = jnp.zeros_like(a1_ref)

    @pl.when((s == 0) | (e_out == 0))
    def _():
        acc_ref[...] = jnp.zeros_like(acc_ref)

    @pl.when(s % 2 == 0)
    def _():
        _pdense_step(*ins, z1_ref, z0_ref, a1_ref, a0_ref)

    @pl.when(s % 2 == 1)
    def _():
        _pdense_step(*ins, z0_ref, z1_ref, a0_ref, a1_ref)

    @pl.when((s >= 2) & (e_out == n_et - 1))
    def _():
        x2 = x1_ref[...] + mod_ref[5:6, :] * acc_ref[...].T
        if apply_norm:
            ms = jnp.mean(x2 * x2, axis=-1, keepdims=True)
            x2 = x2 * lax.rsqrt(ms + EPS) * gain_ref[...]
        y_ref[...] = x2


def _peer_dense(fnt, u_bf16, vt_bf16, alpha, beta, tau, x1, mod, gain, apply_norm):
    d, s = fnt.shape
    ne = u_bf16.shape[0]
    t = 512
    et = PEER_ETILE
    n_et = ne // et
    assert n_et % 2 == 0
    nk = beta.shape[2]
    n_lc = t // LANES
    n_pairs = (s // t) * n_et

    def pair(step, lag):
        p = jnp.clip(step - lag, 0, n_pairs - 1)
        return p // n_et, p % n_et

    return pl.pallas_call(
        functools.partial(_pdense_kernel, n_et=n_et, apply_norm=apply_norm),
        grid=(n_pairs + 2,),
        in_specs=[pl.BlockSpec((d, t), lambda p: (0, pair(p, 0)[0])),
                  pl.BlockSpec((et, d), lambda p: (pair(p, 0)[1], 0)),
                  pl.BlockSpec((None, d, et), lambda p: (pair(p, 2)[1], 0, 0)),
                  pl.BlockSpec((PEER_HEADS, None, PEER_IB, t), lambda p: (0, pair(p, 1)[1], 0, pair(p, 1)[0])),
                  pl.BlockSpec((PEER_HEADS, n_lc, nk, LANES), lambda p: (0, pair(p, 1)[0], 0, 0)),
                  pl.BlockSpec((PEER_HEADS, 1, t), lambda p: (0, 0, pair(p, 1)[0])),
                  pl.BlockSpec((t, d), lambda p: (pair(p, 2)[0], 0)),
                  pl.BlockSpec((6, d), lambda p: (0, 0)),
                  pl.BlockSpec((1, d), lambda p: (0, 0))],
        out_specs=pl.BlockSpec((t, d), lambda p: (pair(p, 2)[0], 0)),
        out_shape=jax.ShapeDtypeStruct((s, d), F32),
        scratch_shapes=[pltpu.VMEM((n_lc, et, LANES), F32), pltpu.VMEM((n_lc, et, LANES), F32),
                        pltpu.VMEM((n_lc, et, LANES), BF16), pltpu.VMEM((n_lc, et, LANES), BF16),
                        pltpu.VMEM((d, t), F32)],
        compiler_params=_params(1, 56),
        name="peer_dense",
    )(fnt, u_bf16, vt_bf16, alpha, beta, tau, x1, mod, gain.reshape(1, d))


def kernel(x, c, positions, norm_mix_gain, w_ada, b_ada, w_in, pool_w, pool_scale, ret_norm_gain,
           w_branch_pool, w_branch_ret, w_out, norm_ffn_gain, peer_w_query, peer_sub_keys, peer_u, peer_v,
           final_norm_gain):
    batch, s, d = x.shape
    assert batch == 1 and d == D_MODEL and s % 1024 == 0
    depth = w_in.shape[0]
    xs = x.reshape(s, d)
    cos, sin = _rope_tables(positions.reshape(s))
    for l in range(depth):
        mod = _ada(c, w_ada[l], b_ada[l])
        proj = _in_proj(xs, norm_mix_gain[l], mod, w_in[l])
        pool_out = _pool(proj, pool_w[l].astype(BF16), pool_scale[l])
        ret_out, u_bf16, vt_tiles = _retention(proj, cos, sin, ret_norm_gain[l], peer_u[l], peer_v[l])
        merged = _branch(pool_out, ret_out, proj, w_branch_pool[l], w_branch_ret[l])
        x1, fnt = _out_proj(merged, w_out[l].astype(BF16), xs, mod, norm_ffn_gain[l])
        keys = peer_sub_keys[l].reshape(2 * PEER_HEADS, PEER_NKEYS, PEER_HALF).astype(BF16)
        scores = _peer_scores(fnt, peer_w_query[l].T.astype(BF16), keys)
        alpha, beta, tau = _peer_topk(scores)
        xs = _peer_dense(fnt, u_bf16, vt_tiles, alpha, beta, tau, x1, mod, final_norm_gain,
                         apply_norm=(l == depth - 1))
    return xs.reshape(batch, s, d)
```

```python
import functools

import jax
import jax.numpy as jnp
from jax import lax
from jax.experimental import pallas as pl
from jax.experimental.pallas import tpu as pltpu

F32 = jnp.float32
BF16 = jnp.bfloat16

D_MODEL = 2048
EPS = 1e-6
POOL_GROUPS = 4
POOL_GDIM = 512
RET_HEADS = 8
RET_DK = 256
RET_DV = 512
ROPE_BASE = 10000.0
PEER_HEADS = 8
PEER_NKEYS = 128
PEER_HALF = 128
PEER_TOPK = 16

COL_POOL, COL_Q, COL_K, COL_V, COL_G, COL_APOOL, COL_ARET = 0, 2048, 4096, 6144, 10240, 14336, 16384

LANES = 128
MIB = 1024 * 1024

RET_CHUNK = 256


def _params(n_axes, vmem_mib, flags=None):
    return pltpu.CompilerParams(dimension_semantics=("arbitrary",) * n_axes,
                                vmem_limit_bytes=vmem_mib * MIB, flags=flags)


def _dot(a, b):
    return jnp.dot(a, b, preferred_element_type=F32)


def _sigmoid(x):
    return 1.0 / (1.0 + jnp.exp(-x))


def _ada_kernel(c_ref, w_ref, b_ref, o_ref):
    @pl.when(pl.program_id(0) == 0)
    def _():
        o_ref[...] = b_ref[...]

    c = c_ref[...]
    cond = c * _sigmoid(c)
    o_ref[...] += jnp.sum(cond * w_ref[...], axis=0, keepdims=True)


def _ada(c, w, b):
    d, n = w.shape
    tk = 256
    out = pl.pallas_call(
        _ada_kernel,
        grid=(d // tk,),
        in_specs=[pl.BlockSpec((tk, 1), lambda k: (k, 0)),
                  pl.BlockSpec((tk, n), lambda k: (k, 0)),
                  pl.BlockSpec((1, n), lambda k: (0, 0))],
        out_specs=pl.BlockSpec((1, n), lambda k: (0, 0)),
        out_shape=jax.ShapeDtypeStruct((1, n), F32),
        compiler_params=_params(1, 40),
        name="ada",
    )(c.reshape(d, 1), w, b.reshape(1, n))
    return out.reshape(6, d)


def _in_kernel(x_ref, gain_ref, mod_ref, w_ref, o_ref, hn_ref):
    tm = x_ref.shape[0]
    rc = 256

    @pl.when(pl.program_id(1) == 0)
    def _():
        scale = gain_ref[...] * (1.0 + mod_ref[1:2, :])
        shift = mod_ref[0:1, :]

        def body(r, carry):
            rows = pl.ds(pl.multiple_of(r * rc, rc), rc)
            x = x_ref[rows, :]
            ms = jnp.mean(x * x, axis=-1, keepdims=True)
            hn_ref[rows, :] = (x * lax.rsqrt(ms + EPS) * scale + shift).astype(BF16)
            return carry

        lax.fori_loop(0, tm // rc, body, 0)

    o_ref[...] = _dot(hn_ref[...], w_ref[...].astype(BF16)).astype(BF16)


def _in_proj(x2d, gain, mod, w):
    s, d = x2d.shape
    n = w.shape[1]
    tm, tn = 1024, 1024
    return pl.pallas_call(
        _in_kernel,
        grid=(s // tm, n // tn),
        in_specs=[pl.BlockSpec((tm, d), lambda i, j: (i, 0)),
                  pl.BlockSpec((1, d), lambda i, j: (0, 0)),
                  pl.BlockSpec((6, d), lambda i, j: (0, 0)),
                  pl.BlockSpec((d, tn), lambda i, j: (0, j))],
        out_specs=pl.BlockSpec((tm, tn), lambda i, j: (i, j)),
        out_shape=jax.ShapeDtypeStruct((s, n), BF16),
        scratch_shapes=[pltpu.VMEM((tm, d), BF16)],
        compiler_params=_params(2, 54),
        name="in_proj",
    )(x2d, gain.reshape(1, d), mod, w)


def _rope_kernel(pos_ref, inv_ref, cos_ref, sin_ref):
    ang = pos_ref[...].astype(F32) * inv_ref[...]
    cos_ref[...] = jnp.cos(ang)
    sin_ref[...] = jnp.sin(ang)


def _rope_tables(positions):
    s = positions.shape[0]
    half = RET_DK // 2
    inv_freq = (ROPE_BASE ** (-jnp.arange(0, RET_DK, 2, dtype=F32) / RET_DK)).reshape(1, half)
    tm = 1024
    return pl.pallas_call(
        _rope_kernel,
        grid=(s // tm,),
        in_specs=[pl.BlockSpec((tm, 1), lambda i: (i, 0)),
                  pl.BlockSpec((1, half), lambda i: (0, 0))],
        out_specs=[pl.BlockSpec((tm, half), lambda i: (i, 0)),
                   pl.BlockSpec((tm, half), lambda i: (i, 0))],
        out_shape=[jax.ShapeDtypeStruct((s, half), F32), jax.ShapeDtypeStruct((s, half), F32)],
        compiler_params=_params(1, 32),
        name="rope",
    )(positions.reshape(s, 1), inv_freq)


POOL_HALO = 16


def _pool_kernel(u_ref, halo_ref, pw_ref, ps_ref, o_ref):
    i = pl.program_id(0)
    g = pl.program_id(1)
    tm = u_ref.shape[0]
    w = jnp.left_shift(2, g)
    r = lax.broadcasted_iota(jnp.int32, (tm, tm), 0)
    c = lax.broadcasted_iota(jnp.int32, (tm, tm), 1)
    band = jnp.where(c <= r, jnp.where(c > r - w, 1.0, 0.0), 0.0).astype(BF16)
    rh = lax.broadcasted_iota(jnp.int32, (tm, POOL_HALO), 0)
    ch = lax.broadcasted_iota(jnp.int32, (tm, POOL_HALO), 1)
    halo_on = jnp.where(i > 0, 1.0, 0.0)
    bandh = (jnp.where(ch > rh + POOL_HALO - w, 1.0, 0.0) * halo_on).astype(BF16)
    u = u_ref[...]
    wsum = _dot(band, u) + _dot(bandh, halo_ref[...])
    t = i * tm + lax.broadcasted_iota(jnp.int32, (tm, 1), 0)
    cnt = jnp.minimum(t + 1, w).astype(F32)
    pooled = wsum / cnt - u.astype(F32)
    o_ref[...] = (_dot(pooled.astype(BF16), pw_ref[...]) * ps_ref[...]).astype(BF16)


def _pool(proj, pool_w_bf16, pool_scale):
    s = proj.shape[0]
    tm = 512
    hb = tm // POOL_HALO
    return pl.pallas_call(
        _pool_kernel,
        grid=(s // tm, POOL_GROUPS),
        in_specs=[pl.BlockSpec((tm, POOL_GDIM), lambda i, g: (i, g)),
                  pl.BlockSpec((POOL_HALO, POOL_GDIM), lambda i, g: (jnp.maximum(i * hb - 1, 0), g)),
                  pl.BlockSpec((None, POOL_GDIM, POOL_GDIM), lambda i, g: (g, 0, 0)),
                  pl.BlockSpec((1, POOL_GDIM), lambda i, g: (0, g))],
        out_specs=pl.BlockSpec((tm, POOL_GDIM), lambda i, g: (i, g)),
        out_shape=jax.ShapeDtypeStruct((s, POOL_GROUPS * POOL_GDIM), BF16),
        compiler_params=_params(2, 32),
        name="pool",
    )(proj, proj, pool_w_bf16, pool_scale.reshape(1, -1))


RET_HPS = 4


def _ret_kernel(cdec_ref, q_ref, k_ref, v_ref, g_ref, cos_ref, sin_ref, dec_ref, qd_ref, kd_ref,
                gain_ref, pu_ref, pv_ref, o_ref, pub_ref, pvt_ref, state_ref):
    hp = pl.program_id(0)

    pub_ref[...] = pu_ref[...].astype(BF16)
    pvt_ref[...] = pv_ref[...].T.astype(BF16)

    @pl.when(pl.program_id(1) == 0)
    def _():
        state_ref[...] = jnp.zeros_like(state_ref)

    cos = cos_ref[...]
    sin = sin_ref[...]
    half = RET_DK // 2

    def rot(t_ref, col0):
        t1 = t_ref[:, col0:col0 + half].astype(F32)
        t2 = t_ref[:, col0 + half:col0 + RET_DK].astype(F32)
        return t1 * cos - t2 * sin, t2 * cos + t1 * sin

    for j in range(RET_HPS):
        q1, q2 = rot(q_ref, j * RET_DK)
        k1, k2 = rot(k_ref, j * RET_DK)
        qd = qd_ref[j]
        kd = kd_ref[j]
        qb = jnp.concatenate([q1, q2], axis=1).astype(BF16)
        kb = jnp.concatenate([k1, k2], axis=1).astype(BF16)
        qdb = jnp.concatenate([q1 * qd, q2 * qd], axis=1).astype(BF16)
        kdb = jnp.concatenate([k1 * kd, k2 * kd], axis=1).astype(BF16)
        vcols = slice(j * RET_DV, (j + 1) * RET_DV)
        v = v_ref[:, vcols]
        scores = lax.dot_general(qb, kb, (((1,), (1,)), ((), ())), preferred_element_type=F32) * dec_ref[j]
        inner = _dot(scores.astype(BF16), v)
        state = state_ref[j]
        cross = _dot(qdb, state.astype(BF16))
        state_ref[j] = state * cdec_ref[hp * RET_HPS + j] + lax.dot_general(
            kdb, v, (((0,), (0,)), ((), ())), preferred_element_type=F32)
        y = inner + cross
        mu = jnp.mean(y, axis=-1, keepdims=True)
        yc = y - mu
        var = jnp.mean(yc * yc, axis=-1, keepdims=True)
        yn = yc * lax.rsqrt(var + EPS) * gain_ref[:, vcols]
        gate = g_ref[:, vcols].astype(F32)
        o_ref[:, vcols] = (gate * _sigmoid(gate) * yn).astype(BF16)


def _ret_tables():
    c = RET_CHUNK
    log_g = jnp.log(1.0 - 2.0 ** (-5.0 - jnp.arange(RET_HEADS, dtype=F32)))
    idx = jnp.arange(c, dtype=F32)
    diff = idx[:, None] - idx[None, :]
    kscale = RET_DK ** -0.5
    dec = jnp.where(diff[None] >= 0, jnp.exp(jnp.maximum(diff, 0.0)[None] * log_g[:, None, None]), 0.0) * kscale
    qd = jnp.exp((idx + 1.0)[None, :] * log_g[:, None])
    kd = jnp.exp((c - 1.0 - idx)[None, :] * log_g[:, None]) * kscale
    half = RET_DK // 2
    qd = jnp.broadcast_to(qd[:, :, None], (RET_HEADS, c, half))
    kd = jnp.broadcast_to(kd[:, :, None], (RET_HEADS, c, half))
    cdec = jnp.exp(c * log_g)
    return cdec, dec, qd, kd


def _retention(proj, cos, sin, ret_gain, peer_u, peer_v):
    s = proj.shape[0]
    c = RET_CHUNK
    half = RET_DK // 2
    cdec, dec, qd, kd = _ret_tables()
    wqk, wv = RET_HPS * RET_DK, RET_HPS * RET_DV
    qb0, kb0 = COL_Q // wqk, COL_K // wqk
    vb0, gb0 = COL_V // wv, COL_G // wv
    n_chunks = s // c
    ne, d = peer_u.shape
    rows = ne // ((RET_HEADS // RET_HPS) * n_chunks)
    assert rows % LANES == 0 and PEER_ETILE % rows == 0
    per_tile = PEER_ETILE // rows

    def blk(h, i):
        return h * n_chunks + i

    return pl.pallas_call(
        _ret_kernel,
        grid=(RET_HEADS // RET_HPS, n_chunks),
        in_specs=[pl.BlockSpec(memory_space=pltpu.SMEM),
                  pl.BlockSpec((c, wqk), lambda h, i: (i, qb0 + h)),
                  pl.BlockSpec((c, wqk), lambda h, i: (i, kb0 + h)),
                  pl.BlockSpec((c, wv), lambda h, i: (i, vb0 + h)),
                  pl.BlockSpec((c, wv), lambda h, i: (i, gb0 + h)),
                  pl.BlockSpec((c, half), lambda h, i: (i, 0)),
                  pl.BlockSpec((c, half), lambda h, i: (i, 0)),
                  pl.BlockSpec((RET_HPS, c, c), lambda h, i: (h, 0, 0)),
                  pl.BlockSpec((RET_HPS, c, half), lambda h, i: (h, 0, 0)),
                  pl.BlockSpec((RET_HPS, c, half), lambda h, i: (h, 0, 0)),
                  pl.BlockSpec((1, wv), lambda h, i: (0, h)),
                  pl.BlockSpec((rows, d), lambda h, i: (blk(h, i), 0)),
                  pl.BlockSpec((rows, d), lambda h, i: (blk(h, i), 0))],
        out_specs=[pl.BlockSpec((c, wv), lambda h, i: (i, h)),
                   pl.BlockSpec((rows, d), lambda h, i: (blk(h, i), 0)),
                   pl.BlockSpec((None, d, rows), lambda h, i: (blk(h, i) // per_tile, 0, blk(h, i) % per_tile))],
        out_shape=[jax.ShapeDtypeStruct((s, RET_HEADS * RET_DV), BF16),
                   jax.ShapeDtypeStruct((ne, d), BF16),
                   jax.ShapeDtypeStruct((ne // PEER_ETILE, d, PEER_ETILE), BF16)],
        scratch_shapes=[pltpu.VMEM((RET_HPS, RET_DK, RET_DV), F32)],
        compiler_params=_params(2, 40),
        name="retention",
    )(cdec, proj, proj, proj, proj, cos, sin, dec, qd, kd, ret_gain.reshape(1, -1), peer_u, peer_v)


def _branch_kernel(p_ref, r_ref, ap_ref, ar_ref, wp_ref, wr_ref, o_ref):
    bp = _dot(p_ref[...], wp_ref[...])
    br = _dot(r_ref[...], wr_ref[...])
    ap = ap_ref[...].astype(F32)
    ar = ar_ref[...].astype(F32)
    o_ref[...] = (_sigmoid(ap) * bp + _sigmoid(ar) * br).astype(BF16)


def _branch(pool_out, ret_out, proj, wp, wr):
    s = pool_out.shape[0]
    d = wp.shape[1]
    tm, tn = 512, 1024
    ap0, ar0 = COL_APOOL // tn, COL_ARET // tn
    return pl.pallas_call(
        _branch_kernel,
        grid=(s // tm, d // tn),
        in_specs=[pl.BlockSpec((tm, pool_out.shape[1]), lambda i, j: (i, 0)),
                  pl.BlockSpec((tm, ret_out.shape[1]), lambda i, j: (i, 0)),
                  pl.BlockSpec((tm, tn), lambda i, j: (i, ap0 + j)),
                  pl.BlockSpec((tm, tn), lambda i, j: (i, ar0 + j)),
                  pl.BlockSpec((wp.shape[0], tn), lambda i, j: (0, j)),
                  pl.BlockSpec((wr.shape[0], tn), lambda i, j: (0, j))],
        out_specs=pl.BlockSpec((tm, tn), lambda i, j: (i, j)),
        out_shape=jax.ShapeDtypeStruct((s, d), BF16),
        compiler_params=_params(2, 50),
        name="branch",
    )(pool_out, ret_out, proj, proj, wp.astype(BF16), wr.astype(BF16))


def _out_kernel(m_ref, w_ref, x_ref, mod_ref, gain_ref, x1_ref, fnt_ref):
    x1 = x_ref[...] + mod_ref[2:3, :] * _dot(m_ref[...], w_ref[...])
    x1_ref[...] = x1
    ms = jnp.mean(x1 * x1, axis=-1, keepdims=True)
    fn = x1 * lax.rsqrt(ms + EPS) * (gain_ref[...] * (1.0 + mod_ref[4:5, :])) + mod_ref[3:4, :]
    fnt_ref[...] = fn.T.astype(BF16)


def _out_proj(merged, w_bf16, x2d, mod, gain):
    s, d = x2d.shape
    tm = 512
    return pl.pallas_call(
        _out_kernel,
        grid=(s // tm,),
        in_specs=[pl.BlockSpec((tm, d), lambda i: (i, 0)),
                  pl.BlockSpec((d, d), lambda i: (0, 0)),
                  pl.BlockSpec((tm, d), lambda i: (i, 0)),
                  pl.BlockSpec((6, d), lambda i: (0, 0)),
                  pl.BlockSpec((1, d), lambda i: (0, 0))],
        out_specs=[pl.BlockSpec((tm, d), lambda i: (i, 0)),
                   pl.BlockSpec((d, tm), lambda i: (0, i))],
        out_shape=[jax.ShapeDtypeStruct((s, d), F32), jax.ShapeDtypeStruct((d, s), BF16)],
        compiler_params=_params(1, 56),
        name="out_proj",
    )(merged, w_bf16, x2d, mod, gain.reshape(1, d))


SUBLANES = 8
LOG2E = 1.4426950408889634


def _sort_network(n):
    size = 1
    while size < n:
        size *= 2

    def merge(lo, hi, r):
        step = r * 2
        if step < hi - lo:
            yield from merge(lo, hi, step)
            yield from merge(lo + r, hi, step)
            for i in range(lo + r, hi - r, step):
                yield (i, i + r)
        else:
            yield (lo, lo + r)

    def sort(lo, hi):
        if hi - lo >= 1:
            mid = lo + (hi - lo) // 2
            yield from sort(lo, mid)
            yield from sort(mid + 1, hi)
            yield from merge(lo, hi, 1)

    return [(i, j) for i, j in sort(0, size - 1) if j < n]


def _top_values(slabs, k):
    cols = list(slabs)
    for i, j in _sort_network(len(cols)):
        hi = jnp.maximum(cols[i], cols[j])
        cols[j] = jnp.minimum(cols[i], cols[j])
        cols[i] = hi
    vals = []
    for r in range(k):
        m = jnp.max(cols[0], axis=0, keepdims=True)
        vals.append(m)
        depth = min(len(cols), k - r)
        if r == k - 1:
            break
        hit = cols[0] == m
        for q in range(depth - 1):
            cols[q] = jnp.where(hit, cols[q + 1], cols[q])
        if depth == len(cols):
            cols[depth - 1] = jnp.where(hit, -jnp.inf, cols[depth - 1])
    return vals


def _rows_to_slabs(rows, n_slabs, row_id):
    slabs = []
    for g in range(n_slabs):
        slab = jnp.full(row_id.shape, -jnp.inf, F32)
        for q in range(SUBLANES):
            r = g * SUBLANES + q
            if r < len(rows):
                slab = jnp.where(row_id == q, rows[r], slab)
        slabs.append(slab)
    return slabs


PEER_ETILE = 512
PEER_IB = PEER_ETILE // PEER_NKEYS


PEER_NTOP = PEER_TOPK + 1


def _topk_chunk(s1, s2, row_id):
    n_slabs = PEER_NKEYS // SUBLANES
    v1 = _top_values([s1[g * SUBLANES:(g + 1) * SUBLANES, :] for g in range(n_slabs)], PEER_NTOP)
    v2 = _top_values([s2[g * SUBLANES:(g + 1) * SUBLANES, :] for g in range(n_slabs)], PEER_NTOP)
    n_vs = -(-PEER_NTOP // SUBLANES)
    v1s = _rows_to_slabs(v1, n_vs, row_id)
    v2s = _rows_to_slabs(v2, n_vs, row_id)
    cand = [v1[0] + slab for slab in v2s]
    for r1 in range(1, SUBLANES):
        n_ok = PEER_NTOP // (r1 + 1)
        pair = v1[r1] + v2s[0]
        cand.append(pair if n_ok >= SUBLANES else jnp.where(row_id < n_ok, pair, -jnp.inf))
    for slab in v1s[1:]:
        cand.append(slab + v2[0])
    best = _top_values(cand, PEER_NTOP)
    z = jnp.ones_like(best[0])
    for r in range(1, PEER_TOPK):
        z = z + jnp.exp(best[r] - best[0])
    log_norm = best[0] + jnp.log(z)
    alpha = (s1 - log_norm) * LOG2E
    beta = s2 * LOG2E
    tau = (0.5 * (best[PEER_TOPK - 1] + best[PEER_TOPK]) - log_norm) * LOG2E
    return alpha, beta, tau


def _pscore_topk_kernel(fnt_ref, wqt_ref, keys_ref, alpha_ref, beta_ref, tau_ref, s_ref):
    t = fnt_ref.shape[1]
    n_lc = t // LANES
    qrows = 2 * PEER_HALF
    row_id = lax.broadcasted_iota(jnp.int32, (SUBLANES, LANES), 0)

    def score_pieces(h):
        slot = h % 2
        halves = [None] * (t // MXU_COLS)

        def query(c):
            def run():
                cols = slice(c * MXU_COLS, (c + 1) * MXU_COLS)
                halves[c] = _dot(wqt_ref[h * qrows:(h + 1) * qrows, :], fnt_ref[:, cols]).astype(BF16)
            return run

        def scores():
            q = jnp.concatenate(halves, axis=1)
            for p in range(2):
                sc = _dot(keys_ref[2 * h + p], q[p * PEER_HALF:(p + 1) * PEER_HALF, :])
                for lc in range(n_lc):
                    s_ref[slot, p, lc] = sc[:, lc * LANES:(lc + 1) * LANES]

        return [query(c) for c in range(t // MXU_COLS)] + [scores]

    for run in score_pieces(0):
        run()
    for h in range(PEER_HEADS):
        pending = score_pieces(h + 1) if h + 1 < PEER_HEADS else []
        for lc in range(n_lc):
            if pending:
                pending.pop(0)()
            sl = slice(lc * LANES, (lc + 1) * LANES)
            alpha, beta, tau = _topk_chunk(s_ref[h % 2, 0, lc], s_ref[h % 2, 1, lc], row_id)
            for g in range(PEER_NKEYS // PEER_IB):
                alpha_ref[h, g, :, sl] = alpha[g * PEER_IB:(g + 1) * PEER_IB, :]
            beta_ref[h, lc] = beta
            tau_ref[h, :, sl] = tau
        for run in pending:
            run()


def _peer_scores_topk(fnt, wqt_bf16, keys_bf16):
    d, s = fnt.shape
    t = 512
    nq = wqt_bf16.shape[0]
    nk = PEER_NKEYS
    ng = nk // PEER_IB
    n_lc = t // LANES
    assert n_lc >= t // MXU_COLS + 1
    return pl.pallas_call(
        _pscore_topk_kernel,
        grid=(s // t,),
        in_specs=[pl.BlockSpec((d, t), lambda i: (0, i)),
                  pl.BlockSpec((nq, d), lambda i: (0, 0)),
                  pl.BlockSpec((2 * PEER_HEADS, nk, PEER_HALF), lambda i: (0, 0, 0))],
        out_specs=[pl.BlockSpec((PEER_HEADS, ng, PEER_IB, t), lambda i: (0, 0, 0, i)),
                   pl.BlockSpec((PEER_HEADS, n_lc, nk, LANES), lambda i: (0, i, 0, 0)),
                   pl.BlockSpec((PEER_HEADS, 1, t), lambda i: (0, 0, i))],
        out_shape=[jax.ShapeDtypeStruct((PEER_HEADS, ng, PEER_IB, s), F32),
                   jax.ShapeDtypeStruct((PEER_HEADS, s // LANES, nk, LANES), F32),
                   jax.ShapeDtypeStruct((PEER_HEADS, 1, s), F32)],
        scratch_shapes=[pltpu.VMEM((2, 2, n_lc, nk, LANES), F32)],
        compiler_params=_params(1, 48),
        name="peer_scores_topk",
    )(fnt, wqt_bf16, keys_bf16)


MXU_COLS = 256
Z_ROWS = 128
O_ROWS = 256
GATE_ROWS = 16


def _gelu_tanh(x):
    return 0.5 * x * (1.0 + jnp.tanh(0.7978845608028654 * (x + 0.044715 * (x * x * x))))


def _pdense_step(fnt_ref, u_ref, vt_ref, alpha_ref, beta_ref, tau_ref, o_ref,
                 z_prev_ref, z_next_ref, a_prev_ref, a_next_ref):
    t = fnt_ref.shape[1]
    n_lc = t // LANES
    d_model = vt_ref.shape[0]
    n_exp = u_ref.shape[0]
    mxu_chunks = []
    for c0 in range(0, t, MXU_COLS):
        cols = slice(c0, c0 + MXU_COLS)
        zs = [("z", cols, slice(r0, r0 + Z_ROWS)) for r0 in range(0, n_exp, Z_ROWS)]
        os_ = [("o", cols, slice(r0, r0 + O_ROWS)) for r0 in range(0, d_model, O_ROWS)]
        per_z = len(os_) // len(zs)
        for k, zp in enumerate(zs):
            mxu_chunks.append(zp)
            mxu_chunks.extend(os_[k * per_z:(k + 1) * per_z])
    n_jb = PEER_NKEYS // GATE_ROWS
    n_blocks = n_lc * n_jb
    place = {}
    for k, chunk in enumerate(mxu_chunks):
        place.setdefault(k * n_blocks // len(mxu_chunks), []).append(chunk)
    for lc in range(n_lc):
        sl = slice(lc * LANES, (lc + 1) * LANES)
        for jb in range(n_jb):
            for kind, cols, mrows in place.get(lc * n_jb + jb, ()):
                slabs = range(cols.start // LANES, cols.stop // LANES)
                if kind == "z":
                    zc = _dot(u_ref[mrows, :], fnt_ref[:, cols])
                    for k, slab in enumerate(slabs):
                        z_next_ref[slab, mrows, :] = zc[:, k * LANES:(k + 1) * LANES]
                else:
                    ac = jnp.concatenate([a_prev_ref[slab] for slab in slabs], axis=1)
                    o_ref[mrows, cols] += _dot(vt_ref[mrows, :], ac)
            jrows = slice(jb * GATE_ROWS, (jb + 1) * GATE_ROWS)
            accs = [jnp.zeros((GATE_ROWS, LANES), F32) for _ in range(PEER_IB)]
            for h in range(PEER_HEADS):
                beta = beta_ref[h, lc, jrows, :]
                tau = tau_ref[h, :, sl]
                for il in range(PEER_IB):
                    lg = alpha_ref[h, il:il + 1, sl] + beta
                    accs[il] = accs[il] + jnp.where(lg >= tau, jnp.exp2(lg), 0.0)
            for il in range(PEER_IB):
                rows = slice(il * PEER_NKEYS + jb * GATE_ROWS, il * PEER_NKEYS + (jb + 1) * GATE_ROWS)
                a_next_ref[lc, rows, :] = (accs[il] * _gelu_tanh(z_prev_ref[lc, rows, :])).astype(BF16)


def _pdense_kernel(fnt_ref, u_ref, vt_ref, alpha_ref, beta_ref, tau_ref, x1_ref, mod_ref, gain_ref, y_ref,
                   z0_ref, z1_ref, a0_ref, a1_ref, acc_ref, *, n_et, apply_norm):
    s = pl.program_id(0)
    ins = (fnt_ref, u_ref, vt_ref, alpha_ref, beta_ref, tau_ref, acc_ref)
    e_out = (s + n_et - 2) % n_et

    @pl.when(s == 0)
    def _():
        z1_ref[...] = jnp.zeros_like(z1_ref)
        a1_ref[...] = jnp.zeros_like(a1_ref)

    @pl.when((s == 0) | (e_out == 0))
    def _():
        acc_ref[...] = jnp.zeros_like(acc_ref)

    @pl.when(s % 2 == 0)
    def _():
        _pdense_step(*ins, z1_ref, z0_ref, a1_ref, a0_ref)

    @pl.when(s % 2 == 1)
    def _():
        _pdense_step(*ins, z0_ref, z1_ref, a0_ref, a1_ref)

    @pl.when((s >= 2) & (e_out == n_et - 1))
    def _():
        x2 = x1_ref[...] + mod_ref[5:6, :] * acc_ref[...].T
        if apply_norm:
            ms = jnp.mean(x2 * x2, axis=-1, keepdims=True)
            x2 = x2 * lax.rsqrt(ms + EPS) * gain_ref[...]
        y_ref[...] = x2


def _peer_dense(fnt, u_bf16, vt_bf16, alpha, beta, tau, x1, mod, gain, apply_norm):
    d, s = fnt.shape
    ne = u_bf16.shape[0]
    t = 512
    et = PEER_ETILE
    n_et = ne // et
    assert n_et % 2 == 0
    nk = beta.shape[2]
    n_lc = t // LANES
    n_pairs = (s // t) * n_et

    def pair(step, lag):
        p = jnp.clip(step - lag, 0, n_pairs - 1)
        return p // n_et, p % n_et

    return pl.pallas_call(
        functools.partial(_pdense_kernel, n_et=n_et, apply_norm=apply_norm),
        grid=(n_pairs + 2,),
        in_specs=[pl.BlockSpec((d, t), lambda p: (0, pair(p, 0)[0])),
                  pl.BlockSpec((et, d), lambda p: (pair(p, 0)[1], 0)),
                  pl.BlockSpec((None, d, et), lambda p: (pair(p, 2)[1], 0, 0)),
                  pl.BlockSpec((PEER_HEADS, None, PEER_IB, t), lambda p: (0, pair(p, 1)[1], 0, pair(p, 1)[0])),
                  pl.BlockSpec((PEER_HEADS, n_lc, nk, LANES), lambda p: (0, pair(p, 1)[0], 0, 0)),
                  pl.BlockSpec((PEER_HEADS, 1, t), lambda p: (0, 0, pair(p, 1)[0])),
                  pl.BlockSpec((t, d), lambda p: (pair(p, 2)[0], 0)),
                  pl.BlockSpec((6, d), lambda p: (0, 0)),
                  pl.BlockSpec((1, d), lambda p: (0, 0))],
        out_specs=pl.BlockSpec((t, d), lambda p: (pair(p, 2)[0], 0)),
        out_shape=jax.ShapeDtypeStruct((s, d), F32),
        scratch_shapes=[pltpu.VMEM((n_lc, et, LANES), F32), pltpu.VMEM((n_lc, et, LANES), F32),
                        pltpu.VMEM((n_lc, et, LANES), BF16), pltpu.VMEM((n_lc, et, LANES), BF16),
                        pltpu.VMEM((d, t), F32)],
        compiler_params=_params(1, 56),
        name="peer_dense",
    )(fnt, u_bf16, vt_bf16, alpha, beta, tau, x1, mod, gain.reshape(1, d))


def kernel(x, c, positions, norm_mix_gain, w_ada, b_ada, w_in, pool_w, pool_scale, ret_norm_gain,
           w_branch_pool, w_branch_ret, w_out, norm_ffn_gain, peer_w_query, peer_sub_keys, peer_u, peer_v,
           final_norm_gain):
    batch, s, d = x.shape
    assert batch == 1 and d == D_MODEL and s % 1024 == 0
    depth = w_in.shape[0]
    xs = x.reshape(s, d)
    cos, sin = _rope_tables(positions.reshape(s))
    for l in range(depth):
        mod = _ada(c, w_ada[l], b_ada[l])
        proj = _in_proj(xs, norm_mix_gain[l], mod, w_in[l])
        pool_out = _pool(proj, pool_w[l].astype(BF16), pool_scale[l])
        ret_out, u_bf16, vt_tiles = _retention(proj, cos, sin, ret_norm_gain[l], peer_u[l], peer_v[l])
        merged = _branch(pool_out, ret_out, proj, w_branch_pool[l], w_branch_ret[l])
        x1, fnt = _out_proj(merged, w_out[l].astype(BF16), xs, mod, norm_ffn_gain[l])
        keys = peer_sub_keys[l].reshape(2 * PEER_HEADS, PEER_NKEYS, PEER_HALF).astype(BF16)
        alpha, beta, tau = _peer_scores_topk(fnt, peer_w_query[l].T.astype(BF16), keys)
        xs = _peer_dense(fnt, u_bf16, vt_tiles, alpha, beta, tau, x1, mod, final_norm_gain,
                         apply_norm=(l == depth - 1))
    return xs.reshape(batch, s, d)
```

```python
import functools

import jax
import jax.numpy as jnp
from jax import lax
from jax.experimental import pallas as pl
from jax.experimental.pallas import tpu as pltpu

F32 = jnp.float32
BF16 = jnp.bfloat16

D_MODEL = 2048
EPS = 1e-6
POOL_GROUPS = 4
POOL_GDIM = 512
RET_HEADS = 8
RET_DK = 256
RET_DV = 512
ROPE_BASE = 10000.0
PEER_HEADS = 8
PEER_NKEYS = 128
PEER_HALF = 128
PEER_TOPK = 16

COL_POOL, COL_Q, COL_K, COL_V, COL_G, COL_APOOL, COL_ARET = 0, 2048, 4096, 6144, 10240, 14336, 16384

LANES = 128
MIB = 1024 * 1024

RET_CHUNK = 256


def _params(n_axes, vmem_mib, flags=None):
    return pltpu.CompilerParams(dimension_semantics=("arbitrary",) * n_axes,
                                vmem_limit_bytes=vmem_mib * MIB, flags=flags)


def _dot(a, b):
    return jnp.dot(a, b, preferred_element_type=F32)


def _sigmoid(x):
    return 1.0 / (1.0 + jnp.exp(-x))


def _ada_kernel(c_ref, w_ref, b_ref, o_ref):
    @pl.when(pl.program_id(0) == 0)
    def _():
        o_ref[...] = b_ref[...]

    c = c_ref[...]
    cond = c * _sigmoid(c)
    o_ref[...] += jnp.sum(cond * w_ref[...], axis=0, keepdims=True)


def _ada(c, w, b):
    d, n = w.shape
    tk = 256
    out = pl.pallas_call(
        _ada_kernel,
        grid=(d // tk,),
        in_specs=[pl.BlockSpec((tk, 1), lambda k: (k, 0)),
                  pl.BlockSpec((tk, n), lambda k: (k, 0)),
                  pl.BlockSpec((1, n), lambda k: (0, 0))],
        out_specs=pl.BlockSpec((1, n), lambda k: (0, 0)),
        out_shape=jax.ShapeDtypeStruct((1, n), F32),
        compiler_params=_params(1, 40),
        name="ada",
    )(c.reshape(d, 1), w, b.reshape(1, n))
    return out.reshape(6, d)


def _in_kernel(x_ref, gain_ref, mod_ref, w_ref, o_ref, hn_ref):
    tm = x_ref.shape[0]
    rc = 256

    @pl.when(pl.program_id(1) == 0)
    def _():
        scale = gain_ref[...] * (1.0 + mod_ref[1:2, :])
        shift = mod_ref[0:1, :]

        def body(r, carry):
            rows = pl.ds(pl.multiple_of(r * rc, rc), rc)
            x = x_ref[rows, :]
            ms = jnp.mean(x * x, axis=-1, keepdims=True)
            hn_ref[rows, :] = (x * lax.rsqrt(ms + EPS) * scale + shift).astype(BF16)
            return carry

        lax.fori_loop(0, tm // rc, body, 0)

    o_ref[...] = _dot(hn_ref[...], w_ref[...].astype(BF16)).astype(BF16)


def _in_proj(x2d, gain, mod, w):
    s, d = x2d.shape
    n = w.shape[1]
    tm, tn = 1024, 1024
    return pl.pallas_call(
        _in_kernel,
        grid=(s // tm, n // tn),
        in_specs=[pl.BlockSpec((tm, d), lambda i, j: (i, 0)),
                  pl.BlockSpec((1, d), lambda i, j: (0, 0)),
                  pl.BlockSpec((6, d), lambda i, j: (0, 0)),
                  pl.BlockSpec((d, tn), lambda i, j: (0, j))],
        out_specs=pl.BlockSpec((tm, tn), lambda i, j: (i, j)),
        out_shape=jax.ShapeDtypeStruct((s, n), BF16),
        scratch_shapes=[pltpu.VMEM((tm, d), BF16)],
        compiler_params=_params(2, 54),
        name="in_proj",
    )(x2d, gain.reshape(1, d), mod, w)


def _rope_kernel(pos_ref, inv_ref, cos_ref, sin_ref):
    ang = pos_ref[...].astype(F32) * inv_ref[...]
    cos_ref[...] = jnp.cos(ang)
    sin_ref[...] = jnp.sin(ang)


def _rope_tables(positions):
    s = positions.shape[0]
    half = RET_DK // 2
    inv_freq = (ROPE_BASE ** (-jnp.arange(0, RET_DK, 2, dtype=F32) / RET_DK)).reshape(1, half)
    tm = 1024
    return pl.pallas_call(
        _rope_kernel,
        grid=(s // tm,),
        in_specs=[pl.BlockSpec((tm, 1), lambda i: (i, 0)),
                  pl.BlockSpec((1, half), lambda i: (0, 0))],
        out_specs=[pl.BlockSpec((tm, half), lambda i: (i, 0)),
                   pl.BlockSpec((tm, half), lambda i: (i, 0))],
        out_shape=[jax.ShapeDtypeStruct((s, half), F32), jax.ShapeDtypeStruct((s, half), F32)],
        compiler_params=_params(1, 32),
        name="rope",
    )(positions.reshape(s, 1), inv_freq)


RET_HPS = 4


def _ret_kernel(cdec_ref, q_ref, k_ref, v_ref, g_ref, cos_ref, sin_ref, dec_ref, qd_ref, kd_ref,
                gain_ref, pu_ref, pv_ref, o_ref, pub_ref, pvt_ref, state_ref):
    hp = pl.program_id(0)

    pub_ref[...] = pu_ref[...].astype(BF16)
    pvt_ref[...] = pv_ref[...].T.astype(BF16)

    @pl.when(pl.program_id(1) == 0)
    def _():
        state_ref[...] = jnp.zeros_like(state_ref)

    cos = cos_ref[...]
    sin = sin_ref[...]
    half = RET_DK // 2

    def rot(t_ref, col0):
        t1 = t_ref[:, col0:col0 + half].astype(F32)
        t2 = t_ref[:, col0 + half:col0 + RET_DK].astype(F32)
        return t1 * cos - t2 * sin, t2 * cos + t1 * sin

    for j in range(RET_HPS):
        q1, q2 = rot(q_ref, j * RET_DK)
        k1, k2 = rot(k_ref, j * RET_DK)
        qd = qd_ref[j]
        kd = kd_ref[j]
        qb = jnp.concatenate([q1, q2], axis=1).astype(BF16)
        kb = jnp.concatenate([k1, k2], axis=1).astype(BF16)
        qdb = jnp.concatenate([q1 * qd, q2 * qd], axis=1).astype(BF16)
        kdb = jnp.concatenate([k1 * kd, k2 * kd], axis=1).astype(BF16)
        vcols = slice(j * RET_DV, (j + 1) * RET_DV)
        v = v_ref[:, vcols]
        scores = lax.dot_general(qb, kb, (((1,), (1,)), ((), ())), preferred_element_type=F32) * dec_ref[j]
        inner = _dot(scores.astype(BF16), v)
        state = state_ref[j]
        cross = _dot(qdb, state.astype(BF16))
        state_ref[j] = state * cdec_ref[hp * RET_HPS + j] + lax.dot_general(
            kdb, v, (((0,), (0,)), ((), ())), preferred_element_type=F32)
        y = inner + cross
        mu = jnp.mean(y, axis=-1, keepdims=True)
        yc = y - mu
        var = jnp.mean(yc * yc, axis=-1, keepdims=True)
        yn = yc * lax.rsqrt(var + EPS) * gain_ref[:, vcols]
        gate = g_ref[:, vcols].astype(F32)
        o_ref[:, vcols] = (gate * _sigmoid(gate) * yn).astype(BF16)


def _ret_tables():
    c = RET_CHUNK
    log_g = jnp.log(1.0 - 2.0 ** (-5.0 - jnp.arange(RET_HEADS, dtype=F32)))
    idx = jnp.arange(c, dtype=F32)
    diff = idx[:, None] - idx[None, :]
    kscale = RET_DK ** -0.5
    dec = jnp.where(diff[None] >= 0, jnp.exp(jnp.maximum(diff, 0.0)[None] * log_g[:, None, None]), 0.0) * kscale
    qd = jnp.exp((idx + 1.0)[None, :] * log_g[:, None])
    kd = jnp.exp((c - 1.0 - idx)[None, :] * log_g[:, None]) * kscale
    half = RET_DK // 2
    qd = jnp.broadcast_to(qd[:, :, None], (RET_HEADS, c, half))
    kd = jnp.broadcast_to(kd[:, :, None], (RET_HEADS, c, half))
    cdec = jnp.exp(c * log_g)
    return cdec, dec, qd, kd


def _retention(proj, cos, sin, ret_gain, peer_u, peer_v):
    s = proj.shape[0]
    c = RET_CHUNK
    half = RET_DK // 2
    cdec, dec, qd, kd = _ret_tables()
    wqk, wv = RET_HPS * RET_DK, RET_HPS * RET_DV
    qb0, kb0 = COL_Q // wqk, COL_K // wqk
    vb0, gb0 = COL_V // wv, COL_G // wv
    n_chunks = s // c
    ne, d = peer_u.shape
    rows = ne // ((RET_HEADS // RET_HPS) * n_chunks)
    assert rows % LANES == 0 and PEER_ETILE % rows == 0
    per_tile = PEER_ETILE // rows

    def blk(h, i):
        return h * n_chunks + i

    return pl.pallas_call(
        _ret_kernel,
        grid=(RET_HEADS // RET_HPS, n_chunks),
        in_specs=[pl.BlockSpec(memory_space=pltpu.SMEM),
                  pl.BlockSpec((c, wqk), lambda h, i: (i, qb0 + h)),
                  pl.BlockSpec((c, wqk), lambda h, i: (i, kb0 + h)),
                  pl.BlockSpec((c, wv), lambda h, i: (i, vb0 + h)),
                  pl.BlockSpec((c, wv), lambda h, i: (i, gb0 + h)),
                  pl.BlockSpec((c, half), lambda h, i: (i, 0)),
                  pl.BlockSpec((c, half), lambda h, i: (i, 0)),
                  pl.BlockSpec((RET_HPS, c, c), lambda h, i: (h, 0, 0)),
                  pl.BlockSpec((RET_HPS, c, half), lambda h, i: (h, 0, 0)),
                  pl.BlockSpec((RET_HPS, c, half), lambda h, i: (h, 0, 0)),
                  pl.BlockSpec((1, wv), lambda h, i: (0, h)),
                  pl.BlockSpec((rows, d), lambda h, i: (blk(h, i), 0)),
                  pl.BlockSpec((rows, d), lambda h, i: (blk(h, i), 0))],
        out_specs=[pl.BlockSpec((c, wv), lambda h, i: (i, h)),
                   pl.BlockSpec((rows, d), lambda h, i: (blk(h, i), 0)),
                   pl.BlockSpec((None, d, rows), lambda h, i: (blk(h, i) // per_tile, 0, blk(h, i) % per_tile))],
        out_shape=[jax.ShapeDtypeStruct((s, RET_HEADS * RET_DV), BF16),
                   jax.ShapeDtypeStruct((ne, d), BF16),
                   jax.ShapeDtypeStruct((ne // PEER_ETILE, d, PEER_ETILE), BF16)],
        scratch_shapes=[pltpu.VMEM((RET_HPS, RET_DK, RET_DV), F32)],
        compiler_params=_params(2, 40),
        name="retention",
    )(cdec, proj, proj, proj, proj, cos, sin, dec, qd, kd, ret_gain.reshape(1, -1), peer_u, peer_v)


POOL_HALO = 16
POOL_WINDOWS = (2, 4, 8, 16)


def _pool_tile(u_ref, halo_ref, pw_ref, ps_ref, pool_ref):
    i = pl.program_id(0)
    tm = u_ref.shape[0]
    r = lax.broadcasted_iota(jnp.int32, (tm, tm), 0)
    c = lax.broadcasted_iota(jnp.int32, (tm, tm), 1)
    rh = lax.broadcasted_iota(jnp.int32, (tm, POOL_HALO), 0)
    ch = lax.broadcasted_iota(jnp.int32, (tm, POOL_HALO), 1)
    halo_on = jnp.where(i > 0, 1.0, 0.0)
    t = i * tm + lax.broadcasted_iota(jnp.int32, (tm, 1), 0)
    for g, w in enumerate(POOL_WINDOWS):
        cols = slice(g * POOL_GDIM, (g + 1) * POOL_GDIM)
        band = jnp.where(c <= r, jnp.where(c > r - w, 1.0, 0.0), 0.0).astype(BF16)
        bandh = (jnp.where(ch > rh + POOL_HALO - w, 1.0, 0.0) * halo_on).astype(BF16)
        u = u_ref[:, cols]
        wsum = _dot(band, u) + _dot(bandh, halo_ref[:, cols])
        cnt = jnp.minimum(t + 1, w).astype(F32)
        pooled = wsum / cnt - u.astype(F32)
        pool_ref[:, cols] = (_dot(pooled.astype(BF16), pw_ref[g]) * ps_ref[:, cols]).astype(BF16)


def _branch_kernel(u_ref, halo_ref, pw_ref, ps_ref, r_ref, ap_ref, ar_ref, wp_ref, wr_ref, o_ref, pool_ref):
    @pl.when(pl.program_id(1) == 0)
    def _():
        _pool_tile(u_ref, halo_ref, pw_ref, ps_ref, pool_ref)

    bp = _dot(pool_ref[...], wp_ref[...])
    br = _dot(r_ref[...], wr_ref[...])
    ap = ap_ref[...].astype(F32)
    ar = ar_ref[...].astype(F32)
    o_ref[...] = (_sigmoid(ap) * bp + _sigmoid(ar) * br).astype(BF16)


def _branch(ret_out, proj, pool_w_bf16, pool_scale, wp, wr):
    s = ret_out.shape[0]
    d = wp.shape[1]
    pw = POOL_GROUPS * POOL_GDIM
    tm, tn = 512, 1024
    hb = tm // POOL_HALO
    ap0, ar0 = COL_APOOL // tn, COL_ARET // tn
    return pl.pallas_call(
        _branch_kernel,
        grid=(s // tm, d // tn),
        in_specs=[pl.BlockSpec((tm, pw), lambda i, j: (i, COL_POOL // pw)),
                  pl.BlockSpec((POOL_HALO, pw), lambda i, j: (jnp.maximum(i * hb - 1, 0), COL_POOL // pw)),
                  pl.BlockSpec((POOL_GROUPS, POOL_GDIM, POOL_GDIM), lambda i, j: (0, 0, 0)),
                  pl.BlockSpec((1, pw), lambda i, j: (0, 0)),
                  pl.BlockSpec((tm, ret_out.shape[1]), lambda i, j: (i, 0)),
                  pl.BlockSpec((tm, tn), lambda i, j: (i, ap0 + j)),
                  pl.BlockSpec((tm, tn), lambda i, j: (i, ar0 + j)),
                  pl.BlockSpec((wp.shape[0], tn), lambda i, j: (0, j)),
                  pl.BlockSpec((wr.shape[0], tn), lambda i, j: (0, j))],
        out_specs=pl.BlockSpec((tm, tn), lambda i, j: (i, j)),
        out_shape=jax.ShapeDtypeStruct((s, d), BF16),
        scratch_shapes=[pltpu.VMEM((tm, pw), BF16)],
        compiler_params=_params(2, 56),
        name="branch",
    )(proj, proj, pool_w_bf16, pool_scale.reshape(1, -1), ret_out, proj, proj, wp.astype(BF16), wr.astype(BF16))


def _out_kernel(m_ref, w_ref, x_ref, mod_ref, gain_ref, x1_ref, fnt_ref):
    x1 = x_ref[...] + mod_ref[2:3, :] * _dot(m_ref[...], w_ref[...])
    x1_ref[...] = x1
    ms = jnp.mean(x1 * x1, axis=-1, keepdims=True)
    fn = x1 * lax.rsqrt(ms + EPS) * (gain_ref[...] * (1.0 + mod_ref[4:5, :])) + mod_ref[3:4, :]
    fnt_ref[...] = fn.T.astype(BF16)


def _out_proj(merged, w_bf16, x2d, mod, gain):
    s, d = x2d.shape
    tm = 512
    return pl.pallas_call(
        _out_kernel,
        grid=(s // tm,),
        in_specs=[pl.BlockSpec((tm, d), lambda i: (i, 0)),
                  pl.BlockSpec((d, d), lambda i: (0, 0)),
                  pl.BlockSpec((tm, d), lambda i: (i, 0)),
                  pl.BlockSpec((6, d), lambda i: (0, 0)),
                  pl.BlockSpec((1, d), lambda i: (0, 0))],
        out_specs=[pl.BlockSpec((tm, d), lambda i: (i, 0)),
                   pl.BlockSpec((d, tm), lambda i: (0, i))],
        out_shape=[jax.ShapeDtypeStruct((s, d), F32), jax.ShapeDtypeStruct((d, s), BF16)],
        compiler_params=_params(1, 56),
        name="out_proj",
    )(merged, w_bf16, x2d, mod, gain.reshape(1, d))


SUBLANES = 8
LOG2E = 1.4426950408889634


def _sort_network(n):
    size = 1
    while size < n:
        size *= 2

    def merge(lo, hi, r):
        step = r * 2
        if step < hi - lo:
            yield from merge(lo, hi, step)
            yield from merge(lo + r, hi, step)
            for i in range(lo + r, hi - r, step):
                yield (i, i + r)
        else:
            yield (lo, lo + r)

    def sort(lo, hi):
        if hi - lo >= 1:
            mid = lo + (hi - lo) // 2
            yield from sort(lo, mid)
            yield from sort(mid + 1, hi)
            yield from merge(lo, hi, 1)

    return [(i, j) for i, j in sort(0, size - 1) if j < n]


def _top_values(slabs, k):
    cols = list(slabs)
    for i, j in _sort_network(len(cols)):
        hi = jnp.maximum(cols[i], cols[j])
        cols[j] = jnp.minimum(cols[i], cols[j])
        cols[i] = hi
    vals = []
    for r in range(k):
        m = jnp.max(cols[0], axis=0, keepdims=True)
        vals.append(m)
        depth = min(len(cols), k - r)
        if r == k - 1:
            break
        hit = cols[0] == m
        for q in range(depth - 1):
            cols[q] = jnp.where(hit, cols[q + 1], cols[q])
        if depth == len(cols):
            cols[depth - 1] = jnp.where(hit, -jnp.inf, cols[depth - 1])
    return vals


def _rows_to_slabs(rows, n_slabs, row_id):
    slabs = []
    for g in range(n_slabs):
        slab = jnp.full(row_id.shape, -jnp.inf, F32)
        for q in range(SUBLANES):
            r = g * SUBLANES + q
            if r < len(rows):
                slab = jnp.where(row_id == q, rows[r], slab)
        slabs.append(slab)
    return slabs


PEER_ETILE = 512
PEER_IB = PEER_ETILE // PEER_NKEYS


PEER_NTOP = PEER_TOPK + 1


def _topk_chunk(s1, s2, row_id):
    n_slabs = PEER_NKEYS // SUBLANES
    v1 = _top_values([s1[g * SUBLANES:(g + 1) * SUBLANES, :] for g in range(n_slabs)], PEER_NTOP)
    v2 = _top_values([s2[g * SUBLANES:(g + 1) * SUBLANES, :] for g in range(n_slabs)], PEER_NTOP)
    n_vs = -(-PEER_NTOP // SUBLANES)
    v1s = _rows_to_slabs(v1, n_vs, row_id)
    v2s = _rows_to_slabs(v2, n_vs, row_id)
    cand = [v1[0] + slab for slab in v2s]
    for r1 in range(1, SUBLANES):
        n_ok = PEER_NTOP // (r1 + 1)
        pair = v1[r1] + v2s[0]
        cand.append(pair if n_ok >= SUBLANES else jnp.where(row_id < n_ok, pair, -jnp.inf))
    for slab in v1s[1:]:
        cand.append(slab + v2[0])
    best = _top_values(cand, PEER_NTOP)
    z = jnp.ones_like(best[0])
    for r in range(1, PEER_TOPK):
        z = z + jnp.exp(best[r] - best[0])
    log_norm = best[0] + jnp.log(z)
    alpha = (s1 - log_norm) * LOG2E
    beta = s2 * LOG2E
    tau = (0.5 * (best[PEER_TOPK - 1] + best[PEER_TOPK]) - log_norm) * LOG2E
    return alpha, beta, tau


def _pscore_topk_kernel(fnt_ref, wqt_ref, keys_ref, alpha_ref, beta_ref, tau_ref, s_ref):
    t = fnt_ref.shape[1]
    n_lc = t // LANES
    qrows = 2 * PEER_HALF
    row_id = lax.broadcasted_iota(jnp.int32, (SUBLANES, LANES), 0)

    def score_pieces(h):
        slot = h % 2
        halves = [None] * (t // MXU_COLS)

        def query(c):
            def run():
                cols = slice(c * MXU_COLS, (c + 1) * MXU_COLS)
                halves[c] = _dot(wqt_ref[h * qrows:(h + 1) * qrows, :], fnt_ref[:, cols]).astype(BF16)
            return run

        def scores():
            q = jnp.concatenate(halves, axis=1)
            for p in range(2):
                sc = _dot(keys_ref[2 * h + p], q[p * PEER_HALF:(p + 1) * PEER_HALF, :])
                for lc in range(n_lc):
                    s_ref[slot, p, lc] = sc[:, lc * LANES:(lc + 1) * LANES]

        return [query(c) for c in range(t // MXU_COLS)] + [scores]

    for run in score_pieces(0):
        run()
    for h in range(PEER_HEADS):
        pending = score_pieces(h + 1) if h + 1 < PEER_HEADS else []
        for lc in range(n_lc):
            if pending:
                pending.pop(0)()
            sl = slice(lc * LANES, (lc + 1) * LANES)
            alpha, beta, tau = _topk_chunk(s_ref[h % 2, 0, lc], s_ref[h % 2, 1, lc], row_id)
            for g in range(PEER_NKEYS // PEER_IB):
                alpha_ref[h, g, :, sl] = alpha[g * PEER_IB:(g + 1) * PEER_IB, :]
            beta_ref[h, lc] = beta
            tau_ref[h, :, sl] = tau
        for run in pending:
            run()


def _peer_scores_topk(fnt, wqt_bf16, keys_bf16):
    d, s = fnt.shape
    t = 512
    nq = wqt_bf16.shape[0]
    nk = PEER_NKEYS
    ng = nk // PEER_IB
    n_lc = t // LANES
    assert n_lc >= t // MXU_COLS + 1
    return pl.pallas_call(
        _pscore_topk_kernel,
        grid=(s // t,),
        in_specs=[pl.BlockSpec((d, t), lambda i: (0, i)),
                  pl.BlockSpec((nq, d), lambda i: (0, 0)),
                  pl.BlockSpec((2 * PEER_HEADS, nk, PEER_HALF), lambda i: (0, 0, 0))],
        out_specs=[pl.BlockSpec((PEER_HEADS, ng, PEER_IB, t), lambda i: (0, 0, 0, i)),
                   pl.BlockSpec((PEER_HEADS, n_lc, nk, LANES), lambda i: (0, i, 0, 0)),
                   pl.BlockSpec((PEER_HEADS, 1, t), lambda i: (0, 0, i))],
        out_shape=[jax.ShapeDtypeStruct((PEER_HEADS, ng, PEER_IB, s), F32),
                   jax.ShapeDtypeStruct((PEER_HEADS, s // LANES, nk, LANES), F32),
                   jax.ShapeDtypeStruct((PEER_HEADS, 1, s), F32)],
        scratch_shapes=[pltpu.VMEM((2, 2, n_lc, nk, LANES), F32)],
        compiler_params=_params(1, 48),
        name="peer_scores_topk",
    )(fnt, wqt_bf16, keys_bf16)


MXU_COLS = 256
Z_ROWS = 128
O_ROWS = 256
GATE_ROWS = 16


def _gelu_tanh(x):
    return 0.5 * x * (1.0 + jnp.tanh(0.7978845608028654 * (x + 0.044715 * (x * x * x))))


def _pdense_step(fnt_ref, u_ref, vt_ref, alpha_ref, beta_ref, tau_ref, o_ref,
                 z_prev_ref, z_next_ref, a_prev_ref, a_next_ref):
    t = fnt_ref.shape[1]
    n_lc = t // LANES
    d_model = vt_ref.shape[0]
    n_exp = u_ref.shape[0]
    mxu_chunks = []
    for c0 in range(0, t, MXU_COLS):
        cols = slice(c0, c0 + MXU_COLS)
        zs = [("z", cols, slice(r0, r0 + Z_ROWS)) for r0 in range(0, n_exp, Z_ROWS)]
        os_ = [("o", cols, slice(r0, r0 + O_ROWS)) for r0 in range(0, d_model, O_ROWS)]
        per_z = len(os_) // len(zs)
        for k, zp in enumerate(zs):
            mxu_chunks.append(zp)
            mxu_chunks.extend(os_[k * per_z:(k + 1) * per_z])
    n_jb = PEER_NKEYS // GATE_ROWS
    n_blocks = n_lc * n_jb
    place = {}
    for k, chunk in enumerate(mxu_chunks):
        place.setdefault(k * n_blocks // len(mxu_chunks), []).append(chunk)
    for lc in range(n_lc):
        sl = slice(lc * LANES, (lc + 1) * LANES)
        for jb in range(n_jb):
            for kind, cols, mrows in place.get(lc * n_jb + jb, ()):
                slabs = range(cols.start // LANES, cols.stop // LANES)
                if kind == "z":
                    zc = _dot(u_ref[mrows, :], fnt_ref[:, cols])
                    for k, slab in enumerate(slabs):
                        z_next_ref[slab, mrows, :] = zc[:, k * LANES:(k + 1) * LANES]
                else:
                    ac = jnp.concatenate([a_prev_ref[slab] for slab in slabs], axis=1)
                    o_ref[mrows, cols] += _dot(vt_ref[mrows, :], ac)
            jrows = slice(jb * GATE_ROWS, (jb + 1) * GATE_ROWS)
            accs = [jnp.zeros((GATE_ROWS, LANES), F32) for _ in range(PEER_IB)]
            for h in range(PEER_HEADS):
                beta = beta_ref[h, lc, jrows, :]
                tau = tau_ref[h, :, sl]
                for il in range(PEER_IB):
                    lg = alpha_ref[h, il:il + 1, sl] + beta
                    accs[il] = accs[il] + jnp.where(lg >= tau, jnp.exp2(lg), 0.0)
            for il in range(PEER_IB):
                rows = slice(il * PEER_NKEYS + jb * GATE_ROWS, il * PEER_NKEYS + (jb + 1) * GATE_ROWS)
                a_next_ref[lc, rows, :] = (accs[il] * _gelu_tanh(z_prev_ref[lc, rows, :])).astype(BF16)


def _pdense_kernel(fnt_ref, u_ref, vt_ref, alpha_ref, beta_ref, tau_ref, x1_ref, mod_ref, gain_ref, y_ref,
                   z0_ref, z1_ref, a0_ref, a1_ref, acc_ref, *, n_et, apply_norm):
    s = pl.program_id(0)
    ins = (fnt_ref, u_ref, vt_ref, alpha_ref, beta_ref, tau_ref, acc_ref)
    e_out = (s + n_et - 2) % n_et

    @pl.when(s == 0)
    def _():
        z1_ref[...] = jnp.zeros_like(z1_ref)
        a1_ref[...] = jnp.zeros_like(a1_ref)

    @pl.when((s == 0) | (e_out == 0))
    def _():
        acc_ref[...] = jnp.zeros_like(acc_ref)

    @pl.when(s % 2 == 0)
    def _():
        _pdense_step(*ins, z1_ref, z0_ref, a1_ref, a0_ref)

    @pl.when(s % 2 == 1)
    def _():
        _pdense_step(*ins, z0_ref, z1_ref, a0_ref, a1_ref)

    @pl.when((s >= 2) & (e_out == n_et - 1))
    def _():
        x2 = x1_ref[...] + mod_ref[5:6, :] * acc_ref[...].T
        if apply_norm:
            ms = jnp.mean(x2 * x2, axis=-1, keepdims=True)
            x2 = x2 * lax.rsqrt(ms + EPS) * gain_ref[...]
        y_ref[...] = x2


def _peer_dense(fnt, u_bf16, vt_bf16, alpha, beta, tau, x1, mod, gain, apply_norm):
    d, s = fnt.shape
    ne = u_bf16.shape[0]
    t = 512
    et = PEER_ETILE
    n_et = ne // et
    assert n_et % 2 == 0
    nk = beta.shape[2]
    n_lc = t // LANES
    n_pairs = (s // t) * n_et

    def pair(step, lag):
        p = jnp.clip(step - lag, 0, n_pairs - 1)
        return p // n_et, p % n_et

    return pl.pallas_call(
        functools.partial(_pdense_kernel, n_et=n_et, apply_norm=apply_norm),
        grid=(n_pairs + 2,),
        in_specs=[pl.BlockSpec((d, t), lambda p: (0, pair(p, 0)[0])),
                  pl.BlockSpec((et, d), lambda p: (pair(p, 0)[1], 0)),
                  pl.BlockSpec((None, d, et), lambda p: (pair(p, 2)[1], 0, 0)),
                  pl.BlockSpec((PEER_HEADS, None, PEER_IB, t), lambda p: (0, pair(p, 1)[1], 0, pair(p, 1)[0])),
                  pl.BlockSpec((PEER_HEADS, n_lc, nk, LANES), lambda p: (0, pair(p, 1)[0], 0, 0)),
                  pl.BlockSpec((PEER_HEADS, 1, t), lambda p: (0, 0, pair(p, 1)[0])),
                  pl.BlockSpec((t, d), lambda p: (pair(p, 2)[0], 0)),
                  pl.BlockSpec((6, d), lambda p: (0, 0)),
                  pl.BlockSpec((1, d), lambda p: (0, 0))],
        out_specs=pl.BlockSpec((t, d), lambda p: (pair(p, 2)[0], 0)),
        out_shape=jax.ShapeDtypeStruct((s, d), F32),
        scratch_shapes=[pltpu.VMEM((n_lc, et, LANES), F32), pltpu.VMEM((n_lc, et, LANES), F32),
                        pltpu.VMEM((n_lc, et, LANES), BF16), pltpu.VMEM((n_lc, et, LANES), BF16),
                        pltpu.VMEM((d, t), F32)],
        compiler_params=_params(1, 56),
        name="peer_dense",
    )(fnt, u_bf16, vt_bf16, alpha, beta, tau, x1, mod, gain.reshape(1, d))


def kernel(x, c, positions, norm_mix_gain, w_ada, b_ada, w_in, pool_w, pool_scale, ret_norm_gain,
           w_branch_pool, w_branch_ret, w_out, norm_ffn_gain, peer_w_query, peer_sub_keys, peer_u, peer_v,
           final_norm_gain):
    batch, s, d = x.shape
    assert batch == 1 and d == D_MODEL and s % 1024 == 0
    depth = w_in.shape[0]
    xs = x.reshape(s, d)
    cos, sin = _rope_tables(positions.reshape(s))
    for l in range(depth):
        mod = _ada(c, w_ada[l], b_ada[l])
        proj = _in_proj(xs, norm_mix_gain[l], mod, w_in[l])
        ret_out, u_bf16, vt_tiles = _retention(proj, cos, sin, ret_norm_gain[l], peer_u[l], peer_v[l])
        merged = _branch(ret_out, proj, pool_w[l].astype(BF16), pool_scale[l], w_branch_pool[l], w_branch_ret[l])
        x1, fnt = _out_proj(merged, w_out[l].astype(BF16), xs, mod, norm_ffn_gain[l])
        keys = peer_sub_keys[l].reshape(2 * PEER_HEADS, PEER_NKEYS, PEER_HALF).astype(BF16)
        alpha, beta, tau = _peer_scores_topk(fnt, peer_w_query[l].T.astype(BF16), keys)
        xs = _peer_dense(fnt, u_bf16, vt_tiles, alpha, beta, tau, x1, mod, final_norm_gain,
                         apply_norm=(l == depth - 1))
    return xs.reshape(batch, s, d)
```

```python
import functools

import jax
import jax.numpy as jnp
from jax import lax
from jax.experimental import pallas as pl
from jax.experimental.pallas import tpu as pltpu

F32 = jnp.float32
BF16 = jnp.bfloat16

D_MODEL = 2048
EPS = 1e-6
POOL_GROUPS = 4
POOL_GDIM = 512
RET_HEADS = 8
RET_DK = 256
RET_DV = 512
ROPE_BASE = 10000.0
PEER_HEADS = 8
PEER_NKEYS = 128
PEER_HALF = 128
PEER_TOPK = 16

COL_POOL, COL_Q, COL_K, COL_V, COL_G, COL_APOOL, COL_ARET = 0, 2048, 4096, 6144, 10240, 14336, 16384

LANES = 128
MIB = 1024 * 1024

RET_CHUNK = 256


def _params(n_axes, vmem_mib, flags=None):
    return pltpu.CompilerParams(dimension_semantics=("arbitrary",) * n_axes,
                                vmem_limit_bytes=vmem_mib * MIB, flags=flags)


def _dot(a, b):
    return jnp.dot(a, b, preferred_element_type=F32)


def _sigmoid(x):
    return 1.0 / (1.0 + jnp.exp(-x))


def _ada_kernel(c_ref, w_ref, b_ref, o_ref):
    @pl.when(pl.program_id(0) == 0)
    def _():
        o_ref[...] = b_ref[...]

    c = c_ref[...]
    cond = c * _sigmoid(c)
    o_ref[...] += jnp.sum(cond * w_ref[...], axis=0, keepdims=True)


def _ada(c, w, b):
    d, n = w.shape
    tk = 256
    out = pl.pallas_call(
        _ada_kernel,
        grid=(d // tk,),
        in_specs=[pl.BlockSpec((tk, 1), lambda k: (k, 0)),
                  pl.BlockSpec((tk, n), lambda k: (k, 0)),
                  pl.BlockSpec((1, n), lambda k: (0, 0))],
        out_specs=pl.BlockSpec((1, n), lambda k: (0, 0)),
        out_shape=jax.ShapeDtypeStruct((1, n), F32),
        compiler_params=_params(1, 40),
        name="ada",
    )(c.reshape(d, 1), w, b.reshape(1, n))
    return out.reshape(6, d)


def _in_kernel(x_ref, gain_ref, mod_ref, w_ref, o_ref, hn_ref):
    tm = x_ref.shape[0]
    rc = 256

    @pl.when(pl.program_id(1) == 0)
    def _():
        scale = gain_ref[...] * (1.0 + mod_ref[1:2, :])
        shift = mod_ref[0:1, :]

        def body(r, carry):
            rows = pl.ds(pl.multiple_of(r * rc, rc), rc)
            x = x_ref[rows, :]
            ms = jnp.mean(x * x, axis=-1, keepdims=True)
            hn_ref[rows, :] = (x * lax.rsqrt(ms + EPS) * scale + shift).astype(BF16)
            return carry

        lax.fori_loop(0, tm // rc, body, 0)

    o_ref[...] = _dot(hn_ref[...], w_ref[...].astype(BF16)).astype(BF16)


def _in_proj(x2d, gain, mod, w):
    s, d = x2d.shape
    n = w.shape[1]
    tm, tn = 1024, 1024
    return pl.pallas_call(
        _in_kernel,
        grid=(s // tm, n // tn),
        in_specs=[pl.BlockSpec((tm, d), lambda i, j: (i, 0)),
                  pl.BlockSpec((1, d), lambda i, j: (0, 0)),
                  pl.BlockSpec((6, d), lambda i, j: (0, 0)),
                  pl.BlockSpec((d, tn), lambda i, j: (0, j))],
        out_specs=pl.BlockSpec((tm, tn), lambda i, j: (i, j)),
        out_shape=jax.ShapeDtypeStruct((s, n), BF16),
        scratch_shapes=[pltpu.VMEM((tm, d), BF16)],
        compiler_params=_params(2, 54),
        name="in_proj",
    )(x2d, gain.reshape(1, d), mod, w)


def _rope_kernel(pos_ref, inv_ref, cos_ref, sin_ref):
    ang = pos_ref[...].astype(F32) * inv_ref[...]
    cos_ref[...] = jnp.cos(ang)
    sin_ref[...] = jnp.sin(ang)


def _rope_tables(positions):
    s = positions.shape[0]
    half = RET_DK // 2
    inv_freq = (ROPE_BASE ** (-jnp.arange(0, RET_DK, 2, dtype=F32) / RET_DK)).reshape(1, half)
    tm = 1024
    return pl.pallas_call(
        _rope_kernel,
        grid=(s // tm,),
        in_specs=[pl.BlockSpec((tm, 1), lambda i: (i, 0)),
                  pl.BlockSpec((1, half), lambda i: (0, 0))],
        out_specs=[pl.BlockSpec((tm, half), lambda i: (i, 0)),
                   pl.BlockSpec((tm, half), lambda i: (i, 0))],
        out_shape=[jax.ShapeDtypeStruct((s, half), F32), jax.ShapeDtypeStruct((s, half), F32)],
        compiler_params=_params(1, 32),
        name="rope",
    )(positions.reshape(s, 1), inv_freq)


RET_HPS = 4


def _ret_kernel(cdec_ref, q_ref, k_ref, v_ref, g_ref, cos_ref, sin_ref, dec_ref, qd_ref, kd_ref,
                gain_ref, pu_ref, pv_ref, o_ref, pub_ref, pvt_ref, state_ref):
    hp = pl.program_id(0)

    pub_ref[...] = pu_ref[...].astype(BF16)
    pvt_ref[...] = pv_ref[...].T.astype(BF16)

    @pl.when(pl.program_id(1) == 0)
    def _():
        state_ref[...] = jnp.zeros_like(state_ref)

    cos = cos_ref[...]
    sin = sin_ref[...]
    half = RET_DK // 2

    def rot(t_ref, col0):
        t1 = t_ref[:, col0:col0 + half].astype(F32)
        t2 = t_ref[:, col0 + half:col0 + RET_DK].astype(F32)
        return t1 * cos - t2 * sin, t2 * cos + t1 * sin

    for j in range(RET_HPS):
        q1, q2 = rot(q_ref, j * RET_DK)
        k1, k2 = rot(k_ref, j * RET_DK)
        qd = qd_ref[j]
        kd = kd_ref[j]
        qb = jnp.concatenate([q1, q2], axis=1).astype(BF16)
        kb = jnp.concatenate([k1, k2], axis=1).astype(BF16)
        qdb = jnp.concatenate([q1 * qd, q2 * qd], axis=1).astype(BF16)
        kdb = jnp.concatenate([k1 * kd, k2 * kd], axis=1).astype(BF16)
        vcols = slice(j * RET_DV, (j + 1) * RET_DV)
        v = v_ref[:, vcols]
        scores = lax.dot_general(qb, kb, (((1,), (1,)), ((), ())), preferred_element_type=F32) * dec_ref[j]
        inner = _dot(scores.astype(BF16), v)
        state = state_ref[j]
        cross = _dot(qdb, state.astype(BF16))
        state_ref[j] = state * cdec_ref[hp * RET_HPS + j] + lax.dot_general(
            kdb, v, (((0,), (0,)), ((), ())), preferred_element_type=F32)
        y = inner + cross
        mu = jnp.mean(y, axis=-1, keepdims=True)
        yc = y - mu
        var = jnp.mean(yc * yc, axis=-1, keepdims=True)
        yn = yc * lax.rsqrt(var + EPS) * gain_ref[:, vcols]
        gate = g_ref[:, vcols].astype(F32)
        o_ref[:, vcols] = (gate * _sigmoid(gate) * yn).astype(BF16)


def _ret_tables():
    c = RET_CHUNK
    log_g = jnp.log(1.0 - 2.0 ** (-5.0 - jnp.arange(RET_HEADS, dtype=F32)))
    idx = jnp.arange(c, dtype=F32)
    diff = idx[:, None] - idx[None, :]
    kscale = RET_DK ** -0.5
    dec = jnp.where(diff[None] >= 0, jnp.exp(jnp.maximum(diff, 0.0)[None] * log_g[:, None, None]), 0.0) * kscale
    qd = jnp.exp((idx + 1.0)[None, :] * log_g[:, None])
    kd = jnp.exp((c - 1.0 - idx)[None, :] * log_g[:, None]) * kscale
    half = RET_DK // 2
    qd = jnp.broadcast_to(qd[:, :, None], (RET_HEADS, c, half))
    kd = jnp.broadcast_to(kd[:, :, None], (RET_HEADS, c, half))
    cdec = jnp.exp(c * log_g)
    return cdec, dec, qd, kd


def _retention(proj, cos, sin, ret_gain, peer_u, peer_v):
    s = proj.shape[0]
    c = RET_CHUNK
    half = RET_DK // 2
    cdec, dec, qd, kd = _ret_tables()
    wqk, wv = RET_HPS * RET_DK, RET_HPS * RET_DV
    assert COL_Q % wqk == 0 and COL_K % wqk == 0 and COL_V % wv == 0 and COL_G % wv == 0
    qb0, kb0 = COL_Q // wqk, COL_K // wqk
    vb0, gb0 = COL_V // wv, COL_G // wv
    n_chunks = s // c
    ne, d = peer_u.shape
    rows = ne // ((RET_HEADS // RET_HPS) * n_chunks)
    assert rows % LANES == 0 and PEER_ETILE % rows == 0
    per_tile = PEER_ETILE // rows

    def blk(h, i):
        return h * n_chunks + i

    return pl.pallas_call(
        _ret_kernel,
        grid=(RET_HEADS // RET_HPS, n_chunks),
        in_specs=[pl.BlockSpec(memory_space=pltpu.SMEM),
                  pl.BlockSpec((c, wqk), lambda h, i: (i, qb0 + h)),
                  pl.BlockSpec((c, wqk), lambda h, i: (i, kb0 + h)),
                  pl.BlockSpec((c, wv), lambda h, i: (i, vb0 + h)),
                  pl.BlockSpec((c, wv), lambda h, i: (i, gb0 + h)),
                  pl.BlockSpec((c, half), lambda h, i: (i, 0)),
                  pl.BlockSpec((c, half), lambda h, i: (i, 0)),
                  pl.BlockSpec((RET_HPS, c, c), lambda h, i: (h, 0, 0)),
                  pl.BlockSpec((RET_HPS, c, half), lambda h, i: (h, 0, 0)),
                  pl.BlockSpec((RET_HPS, c, half), lambda h, i: (h, 0, 0)),
                  pl.BlockSpec((1, wv), lambda h, i: (0, h)),
                  pl.BlockSpec((rows, d), lambda h, i: (blk(h, i), 0)),
                  pl.BlockSpec((rows, d), lambda h, i: (blk(h, i), 0))],
        out_specs=[pl.BlockSpec((c, wv), lambda h, i: (i, h)),
                   pl.BlockSpec((rows, d), lambda h, i: (blk(h, i), 0)),
                   pl.BlockSpec((None, d, rows), lambda h, i: (blk(h, i) // per_tile, 0, blk(h, i) % per_tile))],
        out_shape=[jax.ShapeDtypeStruct((s, RET_HEADS * RET_DV), BF16),
                   jax.ShapeDtypeStruct((ne, d), BF16),
                   jax.ShapeDtypeStruct((ne // PEER_ETILE, d, PEER_ETILE), BF16)],
        scratch_shapes=[pltpu.VMEM((RET_HPS, RET_DK, RET_DV), F32)],
        compiler_params=_params(2, 40),
        name="retention",
    )(cdec, proj, proj, proj, proj, cos, sin, dec, qd, kd, ret_gain.reshape(1, -1), peer_u, peer_v)


POOL_HALO = 16
POOL_WINDOWS = (2, 4, 8, 16)


def _pool_tile(u_ref, halo_ref, pw_ref, ps_ref, pool_ref):
    i = pl.program_id(0)
    tm = u_ref.shape[0]
    r = lax.broadcasted_iota(jnp.int32, (tm, tm), 0)
    c = lax.broadcasted_iota(jnp.int32, (tm, tm), 1)
    rh = lax.broadcasted_iota(jnp.int32, (tm, POOL_HALO), 0)
    ch = lax.broadcasted_iota(jnp.int32, (tm, POOL_HALO), 1)
    halo_on = jnp.where(i > 0, 1.0, 0.0)
    t = i * tm + lax.broadcasted_iota(jnp.int32, (tm, 1), 0)
    for g, w in enumerate(POOL_WINDOWS):
        cols = slice(g * POOL_GDIM, (g + 1) * POOL_GDIM)
        band = jnp.where(c <= r, jnp.where(c > r - w, 1.0, 0.0), 0.0).astype(BF16)
        bandh = (jnp.where(ch > rh + POOL_HALO - w, 1.0, 0.0) * halo_on).astype(BF16)
        u = u_ref[:, cols]
        wsum = _dot(band, u) + _dot(bandh, halo_ref[:, cols])
        cnt = jnp.minimum(t + 1, w).astype(F32)
        pooled = wsum / cnt - u.astype(F32)
        pool_ref[:, cols] = (_dot(pooled.astype(BF16), pw_ref[g]) * ps_ref[:, cols]).astype(BF16)


def _branch_kernel(u_ref, halo_ref, pw_ref, ps_ref, r_ref, ap_ref, ar_ref, wp_ref, wr_ref, o_ref, pool_ref):
    @pl.when(pl.program_id(1) == 0)
    def _():
        _pool_tile(u_ref, halo_ref, pw_ref, ps_ref, pool_ref)

    bp = _dot(pool_ref[...], wp_ref[...])
    br = _dot(r_ref[...], wr_ref[...])
    ap = ap_ref[...].astype(F32)
    ar = ar_ref[...].astype(F32)
    o_ref[...] = (_sigmoid(ap) * bp + _sigmoid(ar) * br).astype(BF16)


def _branch(ret_out, proj, pool_w_bf16, pool_scale, wp, wr):
    s = ret_out.shape[0]
    d = wp.shape[1]
    pw = POOL_GROUPS * POOL_GDIM
    tm, tn = 512, 1024
    hb = tm // POOL_HALO
    assert COL_APOOL % tn == 0 and COL_ARET % tn == 0 and COL_POOL % pw == 0
    ap0, ar0 = COL_APOOL // tn, COL_ARET // tn
    return pl.pallas_call(
        _branch_kernel,
        grid=(s // tm, d // tn),
        in_specs=[pl.BlockSpec((tm, pw), lambda i, j: (i, COL_POOL // pw)),
                  pl.BlockSpec((POOL_HALO, pw), lambda i, j: (jnp.maximum(i * hb - 1, 0), COL_POOL // pw)),
                  pl.BlockSpec((POOL_GROUPS, POOL_GDIM, POOL_GDIM), lambda i, j: (0, 0, 0)),
                  pl.BlockSpec((1, pw), lambda i, j: (0, 0)),
                  pl.BlockSpec((tm, ret_out.shape[1]), lambda i, j: (i, 0)),
                  pl.BlockSpec((tm, tn), lambda i, j: (i, ap0 + j)),
                  pl.BlockSpec((tm, tn), lambda i, j: (i, ar0 + j)),
                  pl.BlockSpec((wp.shape[0], tn), lambda i, j: (0, j)),
                  pl.BlockSpec((wr.shape[0], tn), lambda i, j: (0, j))],
        out_specs=pl.BlockSpec((tm, tn), lambda i, j: (i, j)),
        out_shape=jax.ShapeDtypeStruct((s, d), BF16),
        scratch_shapes=[pltpu.VMEM((tm, pw), BF16)],
        compiler_params=_params(2, 56),
        name="branch",
    )(proj, proj, pool_w_bf16, pool_scale.reshape(1, -1), ret_out, proj, proj, wp.astype(BF16), wr.astype(BF16))


def _out_kernel(m_ref, w_ref, x_ref, mod_ref, gain_ref, x1_ref, fnt_ref):
    x1 = x_ref[...] + mod_ref[2:3, :] * _dot(m_ref[...], w_ref[...])
    x1_ref[...] = x1
    ms = jnp.mean(x1 * x1, axis=-1, keepdims=True)
    fn = x1 * lax.rsqrt(ms + EPS) * (gain_ref[...] * (1.0 + mod_ref[4:5, :])) + mod_ref[3:4, :]
    fnt_ref[...] = fn.T.astype(BF16)


def _out_proj(merged, w_bf16, x2d, mod, gain):
    s, d = x2d.shape
    tm = 512
    return pl.pallas_call(
        _out_kernel,
        grid=(s // tm,),
        in_specs=[pl.BlockSpec((tm, d), lambda i: (i, 0)),
                  pl.BlockSpec((d, d), lambda i: (0, 0)),
                  pl.BlockSpec((tm, d), lambda i: (i, 0)),
                  pl.BlockSpec((6, d), lambda i: (0, 0)),
                  pl.BlockSpec((1, d), lambda i: (0, 0))],
        out_specs=[pl.BlockSpec((tm, d), lambda i: (i, 0)),
                   pl.BlockSpec((d, tm), lambda i: (0, i))],
        out_shape=[jax.ShapeDtypeStruct((s, d), F32), jax.ShapeDtypeStruct((d, s), BF16)],
        compiler_params=_params(1, 56),
        name="out_proj",
    )(merged, w_bf16, x2d, mod, gain.reshape(1, d))


SUBLANES = 8
LOG2E = 1.4426950408889634


def _sort_network(n):
    size = 1
    while size < n:
        size *= 2

    def merge(lo, hi, r):
        step = r * 2
        if step < hi - lo:
            yield from merge(lo, hi, step)
            yield from merge(lo + r, hi, step)
            for i in range(lo + r, hi - r, step):
                yield (i, i + r)
        else:
            yield (lo, lo + r)

    def sort(lo, hi):
        if hi - lo >= 1:
            mid = lo + (hi - lo) // 2
            yield from sort(lo, mid)
            yield from sort(mid + 1, hi)
            yield from merge(lo, hi, 1)

    return [(i, j) for i, j in sort(0, size - 1) if j < n]


def _top_values(slabs, k):
    cols = list(slabs)
    for i, j in _sort_network(len(cols)):
        hi = jnp.maximum(cols[i], cols[j])
        cols[j] = jnp.minimum(cols[i], cols[j])
        cols[i] = hi
    vals = []
    for r in range(k):
        m = jnp.max(cols[0], axis=0, keepdims=True)
        vals.append(m)
        depth = min(len(cols), k - r)
        if r == k - 1:
            break
        hit = cols[0] == m
        for q in range(depth - 1):
            cols[q] = jnp.where(hit, cols[q + 1], cols[q])
        if depth == len(cols):
            cols[depth - 1] = jnp.where(hit, -jnp.inf, cols[depth - 1])
    return vals


def _rows_to_slabs(rows, n_slabs, row_id):
    slabs = []
    for g in range(n_slabs):
        slab = jnp.full(row_id.shape, -jnp.inf, F32)
        for q in range(SUBLANES):
            r = g * SUBLANES + q
            if r < len(rows):
                slab = jnp.where(row_id == q, rows[r], slab)
        slabs.append(slab)
    return slabs


PEER_ETILE = 512
PEER_IB = PEER_ETILE // PEER_NKEYS


PEER_NTOP = PEER_TOPK + 1


def _topk_chunk(s1, s2, row_id):
    n_slabs = PEER_NKEYS // SUBLANES
    v1 = _top_values([s1[g * SUBLANES:(g + 1) * SUBLANES, :] for g in range(n_slabs)], PEER_NTOP)
    v2 = _top_values([s2[g * SUBLANES:(g + 1) * SUBLANES, :] for g in range(n_slabs)], PEER_NTOP)
    n_vs = -(-PEER_NTOP // SUBLANES)
    v1s = _rows_to_slabs(v1, n_vs, row_id)
    v2s = _rows_to_slabs(v2, n_vs, row_id)
    cand = [v1[0] + slab for slab in v2s]
    for r1 in range(1, SUBLANES):
        n_ok = PEER_NTOP // (r1 + 1)
        pair = v1[r1] + v2s[0]
        cand.append(pair if n_ok >= SUBLANES else jnp.where(row_id < n_ok, pair, -jnp.inf))
    for slab in v1s[1:]:
        cand.append(slab + v2[0])
    best = _top_values(cand, PEER_NTOP)
    z = jnp.ones_like(best[0])
    for r in range(1, PEER_TOPK):
        z = z + jnp.exp(best[r] - best[0])
    log_norm = best[0] + jnp.log(z)
    alpha = (s1 - log_norm) * LOG2E
    beta = s2 * LOG2E
    tau = (0.5 * (best[PEER_TOPK - 1] + best[PEER_TOPK]) - log_norm) * LOG2E
    return alpha, beta, tau


def _pscore_topk_kernel(fnt_ref, wqt_ref, keys_ref, alpha_ref, beta_ref, tau_ref, s_ref):
    t = fnt_ref.shape[1]
    n_lc = t // LANES
    qrows = 2 * PEER_HALF
    row_id = lax.broadcasted_iota(jnp.int32, (SUBLANES, LANES), 0)

    def score_pieces(h):
        slot = h % 2
        halves = [None] * (t // MXU_COLS)

        def query(c):
            def run():
                cols = slice(c * MXU_COLS, (c + 1) * MXU_COLS)
                halves[c] = _dot(wqt_ref[h * qrows:(h + 1) * qrows, :], fnt_ref[:, cols]).astype(BF16)
            return run

        def scores():
            q = jnp.concatenate(halves, axis=1)
            for p in range(2):
                sc = _dot(keys_ref[2 * h + p], q[p * PEER_HALF:(p + 1) * PEER_HALF, :])
                for lc in range(n_lc):
                    s_ref[slot, p, lc] = sc[:, lc * LANES:(lc + 1) * LANES]

        return [query(c) for c in range(t // MXU_COLS)] + [scores]

    for run in score_pieces(0):
        run()
    for h in range(PEER_HEADS):
        pending = score_pieces(h + 1) if h + 1 < PEER_HEADS else []
        for lc in range(n_lc):
            if pending:
                pending.pop(0)()
            sl = slice(lc * LANES, (lc + 1) * LANES)
            alpha, beta, tau = _topk_chunk(s_ref[h % 2, 0, lc], s_ref[h % 2, 1, lc], row_id)
            for g in range(PEER_NKEYS // PEER_IB):
                alpha_ref[h, g, :, sl] = alpha[g * PEER_IB:(g + 1) * PEER_IB, :]
            beta_ref[h, lc] = beta
            tau_ref[h, :, sl] = tau
        for run in pending:
            run()


def _peer_scores_topk(fnt, wqt_bf16, keys_bf16):
    d, s = fnt.shape
    t = 512
    nq = wqt_bf16.shape[0]
    nk = PEER_NKEYS
    ng = nk // PEER_IB
    n_lc = t // LANES
    assert n_lc >= t // MXU_COLS + 1
    return pl.pallas_call(
        _pscore_topk_kernel,
        grid=(s // t,),
        in_specs=[pl.BlockSpec((d, t), lambda i: (0, i)),
                  pl.BlockSpec((nq, d), lambda i: (0, 0)),
                  pl.BlockSpec((2 * PEER_HEADS, nk, PEER_HALF), lambda i: (0, 0, 0))],
        out_specs=[pl.BlockSpec((PEER_HEADS, ng, PEER_IB, t), lambda i: (0, 0, 0, i)),
                   pl.BlockSpec((PEER_HEADS, n_lc, nk, LANES), lambda i: (0, i, 0, 0)),
                   pl.BlockSpec((PEER_HEADS, 1, t), lambda i: (0, 0, i))],
        out_shape=[jax.ShapeDtypeStruct((PEER_HEADS, ng, PEER_IB, s), F32),
                   jax.ShapeDtypeStruct((PEER_HEADS, s // LANES, nk, LANES), F32),
                   jax.ShapeDtypeStruct((PEER_HEADS, 1, s), F32)],
        scratch_shapes=[pltpu.VMEM((2, 2, n_lc, nk, LANES), F32)],
        compiler_params=_params(1, 48),
        name="peer_scores_topk",
    )(fnt, wqt_bf16, keys_bf16)


MXU_COLS = 256
Z_ROWS = 128
O_ROWS = 256
GATE_ROWS = 16


def _gelu_tanh(x):
    return 0.5 * x * (1.0 + jnp.tanh(0.7978845608028654 * (x + 0.044715 * (x * x * x))))


def _pdense_step(fnt_ref, u_ref, vt_ref, alpha_ref, beta_ref, tau_ref, o_ref,
                 z_prev_ref, z_next_ref, a_prev_ref, a_next_ref):
    t = fnt_ref.shape[1]
    n_lc = t // LANES
    d_model = vt_ref.shape[0]
    n_exp = u_ref.shape[0]
    mxu_chunks = []
    for c0 in range(0, t, MXU_COLS):
        cols = slice(c0, c0 + MXU_COLS)
        zs = [("z", cols, slice(r0, r0 + Z_ROWS)) for r0 in range(0, n_exp, Z_ROWS)]
        os_ = [("o", cols, slice(r0, r0 + O_ROWS)) for r0 in range(0, d_model, O_ROWS)]
        per_z = len(os_) // len(zs)
        for k, zp in enumerate(zs):
            mxu_chunks.append(zp)
            mxu_chunks.extend(os_[k * per_z:(k + 1) * per_z])
    n_jb = PEER_NKEYS // GATE_ROWS
    n_blocks = n_lc * n_jb
    place = {}
    for k, chunk in enumerate(mxu_chunks):
        place.setdefault(k * n_blocks // len(mxu_chunks), []).append(chunk)
    for lc in range(n_lc):
        sl = slice(lc * LANES, (lc + 1) * LANES)
        for jb in range(n_jb):
            for kind, cols, mrows in place.get(lc * n_jb + jb, ()):
                slabs = range(cols.start // LANES, cols.stop // LANES)
                if kind == "z":
                    zc = _dot(u_ref[mrows, :], fnt_ref[:, cols])
                    for k, slab in enumerate(slabs):
                        z_next_ref[slab, mrows, :] = zc[:, k * LANES:(k + 1) * LANES]
                else:
                    ac = jnp.concatenate([a_prev_ref[slab] for slab in slabs], axis=1)
                    o_ref[mrows, cols] += _dot(vt_ref[mrows, :], ac)
            jrows = slice(jb * GATE_ROWS, (jb + 1) * GATE_ROWS)
            accs = [jnp.zeros((GATE_ROWS, LANES), F32) for _ in range(PEER_IB)]
            for h in range(PEER_HEADS):
                beta = beta_ref[h, lc, jrows, :]
                tau = tau_ref[h, :, sl]
                for il in range(PEER_IB):
                    lg = alpha_ref[h, il:il + 1, sl] + beta
                    accs[il] = accs[il] + jnp.where(lg >= tau, jnp.exp2(lg), 0.0)
            for il in range(PEER_IB):
                rows = slice(il * PEER_NKEYS + jb * GATE_ROWS, il * PEER_NKEYS + (jb + 1) * GATE_ROWS)
                a_next_ref[lc, rows, :] = (accs[il] * _gelu_tanh(z_prev_ref[lc, rows, :])).astype(BF16)


def _pdense_kernel(fnt_ref, u_ref, vt_ref, alpha_ref, beta_ref, tau_ref, x1_ref, mod_ref, gain_ref, y_ref,
                   z0_ref, z1_ref, a0_ref, a1_ref, acc_ref, *, n_et, apply_norm):
    s = pl.program_id(0)
    ins = (fnt_ref, u_ref, vt_ref, alpha_ref, beta_ref, tau_ref, acc_ref)
    e_out = (s + n_et - 2) % n_et

    @pl.when(s == 0)
    def _():
        z1_ref[...] = jnp.zeros_like(z1_ref)
        a1_ref[...] = jnp.zeros_like(a1_ref)

    @pl.when((s == 0) | (e_out == 0))
    def _():
        acc_ref[...] = jnp.zeros_like(acc_ref)

    @pl.when(s % 2 == 0)
    def _():
        _pdense_step(*ins, z1_ref, z0_ref, a1_ref, a0_ref)

    @pl.when(s % 2 == 1)
    def _():
        _pdense_step(*ins, z0_ref, z1_ref, a0_ref, a1_ref)

    @pl.when((s >= 2) & (e_out == n_et - 1))
    def _():
        x2 = x1_ref[...] + mod_ref[5:6, :] * acc_ref[...].T
        if apply_norm:
            ms = jnp.mean(x2 * x2, axis=-1, keepdims=True)
            x2 = x2 * lax.rsqrt(ms + EPS) * gain_ref[...]
        y_ref[...] = x2


def _peer_dense(fnt, u_bf16, vt_bf16, alpha, beta, tau, x1, mod, gain, apply_norm):
    d, s = fnt.shape
    ne = u_bf16.shape[0]
    t = 512
    et = PEER_ETILE
    n_et = ne // et
    assert n_et % 2 == 0
    nk = beta.shape[2]
    n_lc = t // LANES
    n_pairs = (s // t) * n_et

    def pair(step, lag):
        p = jnp.clip(step - lag, 0, n_pairs - 1)
        return p // n_et, p % n_et

    return pl.pallas_call(
        functools.partial(_pdense_kernel, n_et=n_et, apply_norm=apply_norm),
        grid=(n_pairs + 2,),
        in_specs=[pl.BlockSpec((d, t), lambda p: (0, pair(p, 0)[0])),
                  pl.BlockSpec((et, d), lambda p: (pair(p, 0)[1], 0)),
                  pl.BlockSpec((None, d, et), lambda p: (pair(p, 2)[1], 0, 0)),
                  pl.BlockSpec((PEER_HEADS, None, PEER_IB, t), lambda p: (0, pair(p, 1)[1], 0, pair(p, 1)[0])),
                  pl.BlockSpec((PEER_HEADS, n_lc, nk, LANES), lambda p: (0, pair(p, 1)[0], 0, 0)),
                  pl.BlockSpec((PEER_HEADS, 1, t), lambda p: (0, 0, pair(p, 1)[0])),
                  pl.BlockSpec((t, d), lambda p: (pair(p, 2)[0], 0)),
                  pl.BlockSpec((6, d), lambda p: (0, 0)),
                  pl.BlockSpec((1, d), lambda p: (0, 0))],
        out_specs=pl.BlockSpec((t, d), lambda p: (pair(p, 2)[0], 0)),
        out_shape=jax.ShapeDtypeStruct((s, d), F32),
        scratch_shapes=[pltpu.VMEM((n_lc, et, LANES), F32), pltpu.VMEM((n_lc, et, LANES), F32),
                        pltpu.VMEM((n_lc, et, LANES), BF16), pltpu.VMEM((n_lc, et, LANES), BF16),
                        pltpu.VMEM((d, t), F32)],
        compiler_params=_params(1, 56),
        name="peer_dense",
    )(fnt, u_bf16, vt_bf16, alpha, beta, tau, x1, mod, gain.reshape(1, d))


def kernel(x, c, positions, norm_mix_gain, w_ada, b_ada, w_in, pool_w, pool_scale, ret_norm_gain,
           w_branch_pool, w_branch_ret, w_out, norm_ffn_gain, peer_w_query, peer_sub_keys, peer_u, peer_v,
           final_norm_gain):
    batch, s, d = x.shape
    assert batch == 1 and d == D_MODEL and s % 1024 == 0
    depth = w_in.shape[0]
    xs = x.reshape(s, d)
    cos, sin = _rope_tables(positions.reshape(s))
    for l in range(depth):
        mod = _ada(c, w_ada[l], b_ada[l])
        proj = _in_proj(xs, norm_mix_gain[l], mod, w_in[l])
        ret_out, u_bf16, vt_tiles = _retention(proj, cos, sin, ret_norm_gain[l], peer_u[l], peer_v[l])
        merged = _branch(ret_out, proj, pool_w[l].astype(BF16), pool_scale[l], w_branch_pool[l], w_branch_ret[l])
        x1, fnt = _out_proj(merged, w_out[l].astype(BF16), xs, mod, norm_ffn_gain[l])
        keys = peer_sub_keys[l].reshape(2 * PEER_HEADS, PEER_NKEYS, PEER_HALF).astype(BF16)
        alpha, beta, tau = _peer_scores_topk(fnt, peer_w_query[l].T.astype(BF16), keys)
        xs = _peer_dense(fnt, u_bf16, vt_tiles, alpha, beta, tau, x1, mod, final_norm_gain,
                         apply_norm=(l == depth - 1))
    return xs.reshape(batch, s, d)
```

```python
import functools

import jax
import jax.numpy as jnp
from jax import lax
from jax.experimental import pallas as pl
from jax.experimental.pallas import tpu as pltpu

F32 = jnp.float32
BF16 = jnp.bfloat16

D_MODEL = 2048
EPS = 1e-6
POOL_GROUPS = 4
POOL_GDIM = 512
RET_HEADS = 8
RET_DK = 256
RET_DV = 512
ROPE_BASE = 10000.0
PEER_HEADS = 8
PEER_NKEYS = 128
PEER_HALF = 128
PEER_TOPK = 16

COL_POOL, COL_Q, COL_K, COL_V, COL_G, COL_APOOL, COL_ARET = 0, 2048, 4096, 6144, 10240, 14336, 16384

LANES = 128
MIB = 1024 * 1024

RET_CHUNK = 256


def _params(n_axes, vmem_mib, flags=None):
    return pltpu.CompilerParams(dimension_semantics=("arbitrary",) * n_axes,
                                vmem_limit_bytes=vmem_mib * MIB, flags=flags)


def _dot(a, b):
    return jnp.dot(a, b, preferred_element_type=F32)


def _sigmoid(x):
    return 1.0 / (1.0 + jnp.exp(-x))


def _ada_kernel(c_ref, w_ref, b_ref, o_ref):
    @pl.when(pl.program_id(0) == 0)
    def _():
        o_ref[...] = b_ref[...]

    c = c_ref[...]
    cond = c * _sigmoid(c)
    o_ref[...] += jnp.sum(cond * w_ref[...], axis=0, keepdims=True)


def _ada(c, w, b):
    d, n = w.shape
    tk = 256
    out = pl.pallas_call(
        _ada_kernel,
        grid=(d // tk,),
        in_specs=[pl.BlockSpec((tk, 1), lambda k: (k, 0)),
                  pl.BlockSpec((tk, n), lambda k: (k, 0)),
                  pl.BlockSpec((1, n), lambda k: (0, 0))],
        out_specs=pl.BlockSpec((1, n), lambda k: (0, 0)),
        out_shape=jax.ShapeDtypeStruct((1, n), F32),
        compiler_params=_params(1, 40),
        name="ada",
    )(c.reshape(d, 1), w, b.reshape(1, n))
    return out.reshape(6, d)


def _in_kernel(x_ref, gain_ref, mod_ref, w_ref, o_ref, hn_ref):
    tm = x_ref.shape[0]
    rc = 256

    @pl.when(pl.program_id(1) == 0)
    def _():
        scale = gain_ref[...] * (1.0 + mod_ref[1:2, :])
        shift = mod_ref[0:1, :]

        def body(r, carry):
            rows = pl.ds(pl.multiple_of(r * rc, rc), rc)
            x = x_ref[rows, :]
            ms = jnp.mean(x * x, axis=-1, keepdims=True)
            hn_ref[rows, :] = (x * lax.rsqrt(ms + EPS) * scale + shift).astype(BF16)
            return carry

        lax.fori_loop(0, tm // rc, body, 0)

    o_ref[...] = _dot(hn_ref[...], w_ref[...].astype(BF16)).astype(BF16)


def _in_proj(x2d, gain, mod, w):
    s, d = x2d.shape
    n = w.shape[1]
    tm, tn = 1024, 1536
    assert n % tn == 0
    return pl.pallas_call(
        _in_kernel,
        grid=(s // tm, n // tn),
        in_specs=[pl.BlockSpec((tm, d), lambda i, j: (i, 0), pipeline_mode=pl.Buffered(1)),
                  pl.BlockSpec((1, d), lambda i, j: (0, 0)),
                  pl.BlockSpec((6, d), lambda i, j: (0, 0)),
                  pl.BlockSpec((d, tn), lambda i, j: (0, j))],
        out_specs=pl.BlockSpec((tm, tn), lambda i, j: (i, j)),
        out_shape=jax.ShapeDtypeStruct((s, n), BF16),
        scratch_shapes=[pltpu.VMEM((tm, d), BF16)],
        compiler_params=_params(2, 58),
        name="in_proj",
    )(x2d, gain.reshape(1, d), mod, w)


def _rope_kernel(pos_ref, inv_ref, cos_ref, sin_ref):
    ang = pos_ref[...].astype(F32) * inv_ref[...]
    cos_ref[...] = jnp.cos(ang)
    sin_ref[...] = jnp.sin(ang)


def _rope_tables(positions):
    s = positions.shape[0]
    half = RET_DK // 2
    inv_freq = (ROPE_BASE ** (-jnp.arange(0, RET_DK, 2, dtype=F32) / RET_DK)).reshape(1, half)
    tm = 1024
    return pl.pallas_call(
        _rope_kernel,
        grid=(s // tm,),
        in_specs=[pl.BlockSpec((tm, 1), lambda i: (i, 0)),
                  pl.BlockSpec((1, half), lambda i: (0, 0))],
        out_specs=[pl.BlockSpec((tm, half), lambda i: (i, 0)),
                   pl.BlockSpec((tm, half), lambda i: (i, 0))],
        out_shape=[jax.ShapeDtypeStruct((s, half), F32), jax.ShapeDtypeStruct((s, half), F32)],
        compiler_params=_params(1, 32),
        name="rope",
    )(positions.reshape(s, 1), inv_freq)


RET_HPS = 4


def _ret_kernel(cdec_ref, q_ref, k_ref, v_ref, g_ref, cos_ref, sin_ref, dec_ref, qd_ref, kd_ref,
                gain_ref, pu_ref, pv_ref, o_ref, pub_ref, pvt_ref, state_ref):
    hp = pl.program_id(0)

    pub_ref[...] = pu_ref[...].astype(BF16)
    pvt_ref[...] = pv_ref[...].T.astype(BF16)

    @pl.when(pl.program_id(1) == 0)
    def _():
        state_ref[...] = jnp.zeros_like(state_ref)

    cos = cos_ref[...]
    sin = sin_ref[...]
    half = RET_DK // 2

    def rot(t_ref, col0):
        t1 = t_ref[:, col0:col0 + half].astype(F32)
        t2 = t_ref[:, col0 + half:col0 + RET_DK].astype(F32)
        return t1 * cos - t2 * sin, t2 * cos + t1 * sin

    for j in range(RET_HPS):
        q1, q2 = rot(q_ref, j * RET_DK)
        k1, k2 = rot(k_ref, j * RET_DK)
        qd = qd_ref[j]
        kd = kd_ref[j]
        qb = jnp.concatenate([q1, q2], axis=1).astype(BF16)
        kb = jnp.concatenate([k1, k2], axis=1).astype(BF16)
        qdb = jnp.concatenate([q1 * qd, q2 * qd], axis=1).astype(BF16)
        kdb = jnp.concatenate([k1 * kd, k2 * kd], axis=1).astype(BF16)
        vcols = slice(j * RET_DV, (j + 1) * RET_DV)
        v = v_ref[:, vcols]
        scores = lax.dot_general(qb, kb, (((1,), (1,)), ((), ())), preferred_element_type=F32) * dec_ref[j]
        inner = _dot(scores.astype(BF16), v)
        state = state_ref[j]
        cross = _dot(qdb, state.astype(BF16))
        state_ref[j] = state * cdec_ref[hp * RET_HPS + j] + lax.dot_general(
            kdb, v, (((0,), (0,)), ((), ())), preferred_element_type=F32)
        y = inner + cross
        mu = jnp.mean(y, axis=-1, keepdims=True)
        yc = y - mu
        var = jnp.mean(yc * yc, axis=-1, keepdims=True)
        yn = yc * lax.rsqrt(var + EPS) * gain_ref[:, vcols]
        gate = g_ref[:, vcols].astype(F32)
        o_ref[:, vcols] = (gate * _sigmoid(gate) * yn).astype(BF16)


def _ret_tables():
    c = RET_CHUNK
    log_g = jnp.log(1.0 - 2.0 ** (-5.0 - jnp.arange(RET_HEADS, dtype=F32)))
    idx = jnp.arange(c, dtype=F32)
    diff = idx[:, None] - idx[None, :]
    kscale = RET_DK ** -0.5
    dec = jnp.where(diff[None] >= 0, jnp.exp(jnp.maximum(diff, 0.0)[None] * log_g[:, None, None]), 0.0) * kscale
    qd = jnp.exp((idx + 1.0)[None, :] * log_g[:, None])
    kd = jnp.exp((c - 1.0 - idx)[None, :] * log_g[:, None]) * kscale
    half = RET_DK // 2
    qd = jnp.broadcast_to(qd[:, :, None], (RET_HEADS, c, half))
    kd = jnp.broadcast_to(kd[:, :, None], (RET_HEADS, c, half))
    cdec = jnp.exp(c * log_g)
    return cdec, dec, qd, kd


def _retention(proj, cos, sin, ret_gain, peer_u, peer_v):
    s = proj.shape[0]
    c = RET_CHUNK
    half = RET_DK // 2
    cdec, dec, qd, kd = _ret_tables()
    wqk, wv = RET_HPS * RET_DK, RET_HPS * RET_DV
    assert COL_Q % wqk == 0 and COL_K % wqk == 0 and COL_V % wv == 0 and COL_G % wv == 0
    qb0, kb0 = COL_Q // wqk, COL_K // wqk
    vb0, gb0 = COL_V // wv, COL_G // wv
    n_chunks = s // c
    ne, d = peer_u.shape
    rows = ne // ((RET_HEADS // RET_HPS) * n_chunks)
    assert rows % LANES == 0 and PEER_ETILE % rows == 0
    per_tile = PEER_ETILE // rows

    def blk(h, i):
        return h * n_chunks + i

    return pl.pallas_call(
        _ret_kernel,
        grid=(RET_HEADS // RET_HPS, n_chunks),
        in_specs=[pl.BlockSpec(memory_space=pltpu.SMEM),
                  pl.BlockSpec((c, wqk), lambda h, i: (i, qb0 + h)),
                  pl.BlockSpec((c, wqk), lambda h, i: (i, kb0 + h)),
                  pl.BlockSpec((c, wv), lambda h, i: (i, vb0 + h)),
                  pl.BlockSpec((c, wv), lambda h, i: (i, gb0 + h)),
                  pl.BlockSpec((c, half), lambda h, i: (i, 0)),
                  pl.BlockSpec((c, half), lambda h, i: (i, 0)),
                  pl.BlockSpec((RET_HPS, c, c), lambda h, i: (h, 0, 0)),
                  pl.BlockSpec((RET_HPS, c, half), lambda h, i: (h, 0, 0)),
                  pl.BlockSpec((RET_HPS, c, half), lambda h, i: (h, 0, 0)),
                  pl.BlockSpec((1, wv), lambda h, i: (0, h)),
                  pl.BlockSpec((rows, d), lambda h, i: (blk(h, i), 0)),
                  pl.BlockSpec((rows, d), lambda h, i: (blk(h, i), 0))],
        out_specs=[pl.BlockSpec((c, wv), lambda h, i: (i, h)),
                   pl.BlockSpec((rows, d), lambda h, i: (blk(h, i), 0)),
                   pl.BlockSpec((None, d, rows), lambda h, i: (blk(h, i) // per_tile, 0, blk(h, i) % per_tile))],
        out_shape=[jax.ShapeDtypeStruct((s, RET_HEADS * RET_DV), BF16),
                   jax.ShapeDtypeStruct((ne, d), BF16),
                   jax.ShapeDtypeStruct((ne // PEER_ETILE, d, PEER_ETILE), BF16)],
        scratch_shapes=[pltpu.VMEM((RET_HPS, RET_DK, RET_DV), F32)],
        compiler_params=_params(2, 40),
        name="retention",
    )(cdec, proj, proj, proj, proj, cos, sin, dec, qd, kd, ret_gain.reshape(1, -1), peer_u, peer_v)


POOL_HALO = 16
POOL_WINDOWS = (2, 4, 8, 16)


def _pool_tile(u_ref, halo_ref, pw_ref, ps_ref, pool_ref):
    i = pl.program_id(0)
    tm = u_ref.shape[0]
    r = lax.broadcasted_iota(jnp.int32, (tm, tm), 0)
    c = lax.broadcasted_iota(jnp.int32, (tm, tm), 1)
    rh = lax.broadcasted_iota(jnp.int32, (tm, POOL_HALO), 0)
    ch = lax.broadcasted_iota(jnp.int32, (tm, POOL_HALO), 1)
    halo_on = jnp.where(i > 0, 1.0, 0.0)
    t = i * tm + lax.broadcasted_iota(jnp.int32, (tm, 1), 0)
    for g, w in enumerate(POOL_WINDOWS):
        cols = slice(g * POOL_GDIM, (g + 1) * POOL_GDIM)
        band = jnp.where(c <= r, jnp.where(c > r - w, 1.0, 0.0), 0.0).astype(BF16)
        bandh = (jnp.where(ch > rh + POOL_HALO - w, 1.0, 0.0) * halo_on).astype(BF16)
        u = u_ref[:, cols]
        wsum = _dot(band, u) + _dot(bandh, halo_ref[:, cols])
        cnt = jnp.minimum(t + 1, w).astype(F32)
        pooled = wsum / cnt - u.astype(F32)
        pool_ref[:, cols] = (_dot(pooled.astype(BF16), pw_ref[g]) * ps_ref[:, cols]).astype(BF16)


def _branch_kernel(u_ref, halo_ref, pw_ref, ps_ref, r_ref, ap_ref, ar_ref, wp_ref, wr_ref, o_ref, pool_ref):
    @pl.when(pl.program_id(1) == 0)
    def _():
        _pool_tile(u_ref, halo_ref, pw_ref, ps_ref, pool_ref)

    bp = _dot(pool_ref[...], wp_ref[...])
    br = _dot(r_ref[...], wr_ref[...])
    ap = ap_ref[...].astype(F32)
    ar = ar_ref[...].astype(F32)
    o_ref[...] = (_sigmoid(ap) * bp + _sigmoid(ar) * br).astype(BF16)


def _branch(ret_out, proj, pool_w_bf16, pool_scale, wp, wr):
    s = ret_out.shape[0]
    d = wp.shape[1]
    pw = POOL_GROUPS * POOL_GDIM
    tm, tn = 512, 1024
    hb = tm // POOL_HALO
    assert COL_APOOL % tn == 0 and COL_ARET % tn == 0 and COL_POOL % pw == 0
    ap0, ar0 = COL_APOOL // tn, COL_ARET // tn
    return pl.pallas_call(
        _branch_kernel,
        grid=(s // tm, d // tn),
        in_specs=[pl.BlockSpec((tm, pw), lambda i, j: (i, COL_POOL // pw)),
                  pl.BlockSpec((POOL_HALO, pw), lambda i, j: (jnp.maximum(i * hb - 1, 0), COL_POOL // pw)),
                  pl.BlockSpec((POOL_GROUPS, POOL_GDIM, POOL_GDIM), lambda i, j: (0, 0, 0)),
                  pl.BlockSpec((1, pw), lambda i, j: (0, 0)),
                  pl.BlockSpec((tm, ret_out.shape[1]), lambda i, j: (i, 0)),
                  pl.BlockSpec((tm, tn), lambda i, j: (i, ap0 + j)),
                  pl.BlockSpec((tm, tn), lambda i, j: (i, ar0 + j)),
                  pl.BlockSpec((wp.shape[0], tn), lambda i, j: (0, j)),
                  pl.BlockSpec((wr.shape[0], tn), lambda i, j: (0, j))],
        out_specs=pl.BlockSpec((tm, tn), lambda i, j: (i, j)),
        out_shape=jax.ShapeDtypeStruct((s, d), BF16),
        scratch_shapes=[pltpu.VMEM((tm, pw), BF16)],
        compiler_params=_params(2, 56),
        name="branch",
    )(proj, proj, pool_w_bf16, pool_scale.reshape(1, -1), ret_out, proj, proj, wp.astype(BF16), wr.astype(BF16))


def _out_kernel(m_ref, w_ref, x_ref, mod_ref, gain_ref, x1_ref, fnt_ref):
    x1 = x_ref[...] + mod_ref[2:3, :] * _dot(m_ref[...], w_ref[...])
    x1_ref[...] = x1
    ms = jnp.mean(x1 * x1, axis=-1, keepdims=True)
    fn = x1 * lax.rsqrt(ms + EPS) * (gain_ref[...] * (1.0 + mod_ref[4:5, :])) + mod_ref[3:4, :]
    fnt_ref[...] = fn.T.astype(BF16)


def _out_proj(merged, w_bf16, x2d, mod, gain):
    s, d = x2d.shape
    tm = 512
    return pl.pallas_call(
        _out_kernel,
        grid=(s // tm,),
        in_specs=[pl.BlockSpec((tm, d), lambda i: (i, 0)),
                  pl.BlockSpec((d, d), lambda i: (0, 0)),
                  pl.BlockSpec((tm, d), lambda i: (i, 0)),
                  pl.BlockSpec((6, d), lambda i: (0, 0)),
                  pl.BlockSpec((1, d), lambda i: (0, 0))],
        out_specs=[pl.BlockSpec((tm, d), lambda i: (i, 0)),
                   pl.BlockSpec((d, tm), lambda i: (0, i))],
        out_shape=[jax.ShapeDtypeStruct((s, d), F32), jax.ShapeDtypeStruct((d, s), BF16)],
        compiler_params=_params(1, 56),
        name="out_proj",
    )(merged, w_bf16, x2d, mod, gain.reshape(1, d))


SUBLANES = 8
LOG2E = 1.4426950408889634


def _sort_network(n):
    size = 1
    while size < n:
        size *= 2

    def merge(lo, hi, r):
        step = r * 2
        if step < hi - lo:
            yield from merge(lo, hi, step)
            yield from merge(lo + r, hi, step)
            for i in range(lo + r, hi - r, step):
                yield (i, i + r)
        else:
            yield (lo, lo + r)

    def sort(lo, hi):
        if hi - lo >= 1:
            mid = lo + (hi - lo) // 2
            yield from sort(lo, mid)
            yield from sort(mid + 1, hi)
            yield from merge(lo, hi, 1)

    return [(i, j) for i, j in sort(0, size - 1) if j < n]


def _top_values(slabs, k):
    cols = list(slabs)
    for i, j in _sort_network(len(cols)):
        hi = jnp.maximum(cols[i], cols[j])
        cols[j] = jnp.minimum(cols[i], cols[j])
        cols[i] = hi
    vals = []
    for r in range(k):
        m = jnp.max(cols[0], axis=0, keepdims=True)
        vals.append(m)
        depth = min(len(cols), k - r)
        if r == k - 1:
            break
        hit = cols[0] == m
        for q in range(depth - 1):
            cols[q] = jnp.where(hit, cols[q + 1], cols[q])
        if depth == len(cols):
            cols[depth - 1] = jnp.where(hit, -jnp.inf, cols[depth - 1])
    return vals


def _rows_to_slabs(rows, n_slabs, row_id):
    slabs = []
    for g in range(n_slabs):
        slab = jnp.full(row_id.shape, -jnp.inf, F32)
        for q in range(SUBLANES):
            r = g * SUBLANES + q
            if r < len(rows):
                slab = jnp.where(row_id == q, rows[r], slab)
        slabs.append(slab)
    return slabs


PEER_ETILE = 512
PEER_IB = PEER_ETILE // PEER_NKEYS


PEER_NTOP = PEER_TOPK + 1


def _topk_chunk(s1, s2, row_id):
    n_slabs = PEER_NKEYS // SUBLANES
    v1 = _top_values([s1[g * SUBLANES:(g + 1) * SUBLANES, :] for g in range(n_slabs)], PEER_NTOP)
    v2 = _top_values([s2[g * SUBLANES:(g + 1) * SUBLANES, :] for g in range(n_slabs)], PEER_NTOP)
    n_vs = -(-PEER_NTOP // SUBLANES)
    v1s = _rows_to_slabs(v1, n_vs, row_id)
    v2s = _rows_to_slabs(v2, n_vs, row_id)
    cand = [v1[0] + slab for slab in v2s]
    for r1 in range(1, SUBLANES):
        n_ok = PEER_NTOP // (r1 + 1)
        pair = v1[r1] + v2s[0]
        cand.append(pair if n_ok >= SUBLANES else jnp.where(row_id < n_ok, pair, -jnp.inf))
    for slab in v1s[1:]:
        cand.append(slab + v2[0])
    best = _top_values(cand, PEER_NTOP)
    z = jnp.ones_like(best[0])
    for r in range(1, PEER_TOPK):
        z = z + jnp.exp(best[r] - best[0])
    log_norm = best[0] + jnp.log(z)
    alpha = (s1 - log_norm) * LOG2E
    beta = s2 * LOG2E
    tau = (0.5 * (best[PEER_TOPK - 1] + best[PEER_TOPK]) - log_norm) * LOG2E
    return alpha, beta, tau


def _pscore_topk_kernel(fnt_ref, wqt_ref, keys_ref, alpha_ref, beta_ref, tau_ref, s_ref):
    t = fnt_ref.shape[1]
    n_lc = t // LANES
    qrows = 2 * PEER_HALF
    row_id = lax.broadcasted_iota(jnp.int32, (SUBLANES, LANES), 0)

    def score_pieces(h):
        slot = h % 2
        halves = [None] * (t // MXU_COLS)

        def query(c):
            def run():
                cols = slice(c * MXU_COLS, (c + 1) * MXU_COLS)
                halves[c] = _dot(wqt_ref[h * qrows:(h + 1) * qrows, :], fnt_ref[:, cols]).astype(BF16)
            return run

        def scores():
            q = jnp.concatenate(halves, axis=1)
            for p in range(2):
                sc = _dot(keys_ref[2 * h + p], q[p * PEER_HALF:(p + 1) * PEER_HALF, :])
                for lc in range(n_lc):
                    s_ref[slot, p, lc] = sc[:, lc * LANES:(lc + 1) * LANES]

        return [query(c) for c in range(t // MXU_COLS)] + [scores]

    for run in score_pieces(0):
        run()
    for h in range(PEER_HEADS):
        pending = score_pieces(h + 1) if h + 1 < PEER_HEADS else []
        for lc in range(n_lc):
            if pending:
                pending.pop(0)()
            sl = slice(lc * LANES, (lc + 1) * LANES)
            alpha, beta, tau = _topk_chunk(s_ref[h % 2, 0, lc], s_ref[h % 2, 1, lc], row_id)
            for g in range(PEER_NKEYS // PEER_IB):
                alpha_ref[h, g, :, sl] = alpha[g * PEER_IB:(g + 1) * PEER_IB, :]
            beta_ref[h, lc] = beta
            tau_ref[h, :, sl] = tau
        for run in pending:
            run()


def _peer_scores_topk(fnt, wqt_bf16, keys_bf16):
    d, s = fnt.shape
    t = 512
    nq = wqt_bf16.shape[0]
    nk = PEER_NKEYS
    ng = nk // PEER_IB
    n_lc = t // LANES
    assert n_lc >= t // MXU_COLS + 1
    return pl.pallas_call(
        _pscore_topk_kernel,
        grid=(s // t,),
        in_specs=[pl.BlockSpec((d, t), lambda i: (0, i)),
                  pl.BlockSpec((nq, d), lambda i: (0, 0)),
                  pl.BlockSpec((2 * PEER_HEADS, nk, PEER_HALF), lambda i: (0, 0, 0))],
        out_specs=[pl.BlockSpec((PEER_HEADS, ng, PEER_IB, t), lambda i: (0, 0, 0, i)),
                   pl.BlockSpec((PEER_HEADS, n_lc, nk, LANES), lambda i: (0, i, 0, 0)),
                   pl.BlockSpec((PEER_HEADS, 1, t), lambda i: (0, 0, i))],
        out_shape=[jax.ShapeDtypeStruct((PEER_HEADS, ng, PEER_IB, s), F32),
                   jax.ShapeDtypeStruct((PEER_HEADS, s // LANES, nk, LANES), F32),
                   jax.ShapeDtypeStruct((PEER_HEADS, 1, s), F32)],
        scratch_shapes=[pltpu.VMEM((2, 2, n_lc, nk, LANES), F32)],
        compiler_params=_params(1, 48),
        name="peer_scores_topk",
    )(fnt, wqt_bf16, keys_bf16)


MXU_COLS = 256
Z_ROWS = 128
O_ROWS = 256
GATE_ROWS = 16


def _gelu_tanh(x):
    return 0.5 * x * (1.0 + jnp.tanh(0.7978845608028654 * (x + 0.044715 * (x * x * x))))


def _pdense_step(fnt_ref, u_ref, vt_ref, alpha_ref, beta_ref, tau_ref, o_ref,
                 z_prev_ref, z_next_ref, a_prev_ref, a_next_ref):
    t = fnt_ref.shape[1]
    n_lc = t // LANES
    d_model = vt_ref.shape[0]
    n_exp = u_ref.shape[0]
    mxu_chunks = []
    for c0 in range(0, t, MXU_COLS):
        cols = slice(c0, c0 + MXU_COLS)
        zs = [("z", cols, slice(r0, r0 + Z_ROWS)) for r0 in range(0, n_exp, Z_ROWS)]
        os_ = [("o", cols, slice(r0, r0 + O_ROWS)) for r0 in range(0, d_model, O_ROWS)]
        per_z = len(os_) // len(zs)
        for k, zp in enumerate(zs):
            mxu_chunks.append(zp)
            mxu_chunks.extend(os_[k * per_z:(k + 1) * per_z])
    n_jb = PEER_NKEYS // GATE_ROWS
    n_blocks = n_lc * n_jb
    place = {}
    for k, chunk in enumerate(mxu_chunks):
        place.setdefault(k * n_blocks // len(mxu_chunks), []).append(chunk)
    for lc in range(n_lc):
        sl = slice(lc * LANES, (lc + 1) * LANES)
        for jb in range(n_jb):
            for kind, cols, mrows in place.get(lc * n_jb + jb, ()):
                slabs = range(cols.start // LANES, cols.stop // LANES)
                if kind == "z":
                    zc = _dot(u_ref[mrows, :], fnt_ref[:, cols])
                    for k, slab in enumerate(slabs):
                        z_next_ref[slab, mrows, :] = zc[:, k * LANES:(k + 1) * LANES]
                else:
                    ac = jnp.concatenate([a_prev_ref[slab] for slab in slabs], axis=1)
                    o_ref[mrows, cols] += _dot(vt_ref[mrows, :], ac)
            jrows = slice(jb * GATE_ROWS, (jb + 1) * GATE_ROWS)
            accs = [jnp.zeros((GATE_ROWS, LANES), F32) for _ in range(PEER_IB)]
            for h in range(PEER_HEADS):
                beta = beta_ref[h, lc, jrows, :]
                tau = tau_ref[h, :, sl]
                for il in range(PEER_IB):
                    lg = alpha_ref[h, il:il + 1, sl] + beta
                    accs[il] = accs[il] + jnp.where(lg >= tau, jnp.exp2(lg), 0.0)
            for il in range(PEER_IB):
                rows = slice(il * PEER_NKEYS + jb * GATE_ROWS, il * PEER_NKEYS + (jb + 1) * GATE_ROWS)
                a_next_ref[lc, rows, :] = (accs[il] * _gelu_tanh(z_prev_ref[lc, rows, :])).astype(BF16)


def _pdense_kernel(fnt_ref, u_ref, vt_ref, alpha_ref, beta_ref, tau_ref, x1_ref, mod_ref, gain_ref, y_ref,
                   z0_ref, z1_ref, a0_ref, a1_ref, acc_ref, *, n_et, apply_norm):
    s = pl.program_id(0)
    ins = (fnt_ref, u_ref, vt_ref, alpha_ref, beta_ref, tau_ref, acc_ref)
    e_out = (s + n_et - 2) % n_et

    @pl.when(s == 0)
    def _():
        z1_ref[...] = jnp.zeros_like(z1_ref)
        a1_ref[...] = jnp.zeros_like(a1_ref)

    @pl.when((s == 0) | (e_out == 0))
    def _():
        acc_ref[...] = jnp.zeros_like(acc_ref)

    @pl.when(s % 2 == 0)
    def _():
        _pdense_step(*ins, z1_ref, z0_ref, a1_ref, a0_ref)

    @pl.when(s % 2 == 1)
    def _():
        _pdense_step(*ins, z0_ref, z1_ref, a0_ref, a1_ref)

    @pl.when((s >= 2) & (e_out == n_et - 1))
    def _():
        x2 = x1_ref[...] + mod_ref[5:6, :] * acc_ref[...].T
        if apply_norm:
            ms = jnp.mean(x2 * x2, axis=-1, keepdims=True)
            x2 = x2 * lax.rsqrt(ms + EPS) * gain_ref[...]
        y_ref[...] = x2


def _peer_dense(fnt, u_bf16, vt_bf16, alpha, beta, tau, x1, mod, gain, apply_norm):
    d, s = fnt.shape
    ne = u_bf16.shape[0]
    t = 512
    et = PEER_ETILE
    n_et = ne // et
    assert n_et % 2 == 0
    nk = beta.shape[2]
    n_lc = t // LANES
    n_pairs = (s // t) * n_et

    def pair(step, lag):
        p = jnp.clip(step - lag, 0, n_pairs - 1)
        return p // n_et, p % n_et

    return pl.pallas_call(
        functools.partial(_pdense_kernel, n_et=n_et, apply_norm=apply_norm),
        grid=(n_pairs + 2,),
        in_specs=[pl.BlockSpec((d, t), lambda p: (0, pair(p, 0)[0])),
                  pl.BlockSpec((et, d), lambda p: (pair(p, 0)[1], 0)),
                  pl.BlockSpec((None, d, et), lambda p: (pair(p, 2)[1], 0, 0)),
                  pl.BlockSpec((PEER_HEADS, None, PEER_IB, t), lambda p: (0, pair(p, 1)[1], 0, pair(p, 1)[0])),
                  pl.BlockSpec((PEER_HEADS, n_lc, nk, LANES), lambda p: (0, pair(p, 1)[0], 0, 0)),
                  pl.BlockSpec((PEER_HEADS, 1, t), lambda p: (0, 0, pair(p, 1)[0])),
                  pl.BlockSpec((t, d), lambda p: (pair(p, 2)[0], 0)),
                  pl.BlockSpec((6, d), lambda p: (0, 0)),
                  pl.BlockSpec((1, d), lambda p: (0, 0))],
        out_specs=pl.BlockSpec((t, d), lambda p: (pair(p, 2)[0], 0)),
        out_shape=jax.ShapeDtypeStruct((s, d), F32),
        scratch_shapes=[pltpu.VMEM((n_lc, et, LANES), F32), pltpu.VMEM((n_lc, et, LANES), F32),
                        pltpu.VMEM((n_lc, et, LANES), BF16), pltpu.VMEM((n_lc, et, LANES), BF16),
                        pltpu.VMEM((d, t), F32)],
        compiler_params=_params(1, 56),
        name="peer_dense",
    )(fnt, u_bf16, vt_bf16, alpha, beta, tau, x1, mod, gain.reshape(1, d))


def kernel(x, c, positions, norm_mix_gain, w_ada, b_ada, w_in, pool_w, pool_scale, ret_norm_gain,
           w_branch_pool, w_branch_ret, w_out, norm_ffn_gain, peer_w_query, peer_sub_keys, peer_u, peer_v,
           final_norm_gain):
    batch, s, d = x.shape
    assert batch == 1 and d == D_MODEL and s % 1024 == 0
    depth = w_in.shape[0]
    xs = x.reshape(s, d)
    cos, sin = _rope_tables(positions.reshape(s))
    for l in range(depth):
        mod = _ada(c, w_ada[l], b_ada[l])
        proj = _in_proj(xs, norm_mix_gain[l], mod, w_in[l])
        ret_out, u_bf16, vt_tiles = _retention(proj, cos, sin, ret_norm_gain[l], peer_u[l], peer_v[l])
        merged = _branch(ret_out, proj, pool_w[l].astype(BF16), pool_scale[l], w_branch_pool[l], w_branch_ret[l])
        x1, fnt = _out_proj(merged, w_out[l].astype(BF16), xs, mod, norm_ffn_gain[l])
        keys = peer_sub_keys[l].reshape(2 * PEER_HEADS, PEER_NKEYS, PEER_HALF).astype(BF16)
        alpha, beta, tau = _peer_scores_topk(fnt, peer_w_query[l].T.astype(BF16), keys)
        xs = _peer_dense(fnt, u_bf16, vt_tiles, alpha, beta, tau, x1, mod, final_norm_gain,
                         apply_norm=(l == depth - 1))
    return xs.reshape(batch, s, d)
```
